```python
import jax
import jax.numpy as jnp
from jax import lax
import numpy as np

D_MODEL = 1024
BATCH = 16
SEQ = 4096
DEPTH = 4

N_MIXERS = 2
MLA_HEADS = 8
MLA_Q_RANK = 256
MLA_KV_RANK = 256
MLA_NOPE = 128
MLA_ROPE = 64
MLA_V = 128
ROPE_THETA = 10000.0
Q_BLOCK = 128
MAX_POS_OFFSET = 2048
MLSTM_PF = 2
MLSTM_INNER = MLSTM_PF * D_MODEL
MLSTM_HEADS = 4
MLSTM_HD = MLSTM_INNER // MLSTM_HEADS
MLSTM_CONV = 4
MLSTM_QKV_BLOCK = 4
MLSTM_CHUNK = 64
N_EXPERTS = 16
N_GROUPS = 4
EXPERTS_PER_GROUP = N_EXPERTS // N_GROUPS
TOP_K = 2
EXPERT_FF = 512
DN_ALPHA = (2.0 * DEPTH) ** 0.25
DN_BETA = (8.0 * DEPTH) ** -0.25
LN_EPS = 1e-5
N_MLA_LAYERS = (DEPTH + 1) // 2
N_MLSTM_LAYERS = DEPTH // 2

kernel_name = "hybrid_mla_mlstm_grouped_moe_deepnorm"


def layer_norm(x, g, b):
    xf = x.astype(jnp.float32)
    mu = jnp.mean(xf, axis=-1, keepdims=True)
    var = jnp.mean(jnp.square(xf - mu), axis=-1, keepdims=True)
    return ((xf - mu) * lax.rsqrt(var + LN_EPS) * g.astype(jnp.float32) + b.astype(jnp.float32)).astype(x.dtype)


def rms_norm(x, g):
    xf = x.astype(jnp.float32)
    y = xf * lax.rsqrt(jnp.mean(jnp.square(xf), axis=-1, keepdims=True) + LN_EPS)
    return (y * g.astype(jnp.float32)).astype(x.dtype)


def rope_tables(positions):
    inv_freq = 1.0 / (ROPE_THETA ** (jnp.arange(0, MLA_ROPE, 2, dtype=jnp.float32) / MLA_ROPE))
    ang = positions.astype(jnp.float32)[..., None] * inv_freq
    return jnp.cos(ang), jnp.sin(ang)


def apply_rope(x, cos, sin):
    x1, x2 = jnp.split(x.astype(jnp.float32), 2, axis=-1)
    return jnp.concatenate([x1 * cos - x2 * sin, x1 * sin + x2 * cos], axis=-1).astype(x.dtype)


def mla_mixer(x, cos, sin, w_in, g_q, w_uq, g_kv, w_ukv, w_o):
    B, S, _ = x.shape
    c = x @ w_in
    c_q, c_kv, k_pe = jnp.split(c, [MLA_Q_RANK, MLA_Q_RANK + MLA_KV_RANK], axis=-1)
    q = (rms_norm(c_q, g_q) @ w_uq).reshape(B, S, MLA_HEADS, MLA_NOPE + MLA_ROPE)
    q_nope, q_pe = jnp.split(q, [MLA_NOPE], axis=-1)
    q_pe = apply_rope(q_pe, cos[:, :, None, :], sin[:, :, None, :])
    k_pe = apply_rope(k_pe, cos, sin)
    kv = (rms_norm(c_kv, g_kv) @ w_ukv).reshape(B, S, MLA_HEADS, MLA_NOPE + MLA_V)
    k_nope, v = jnp.split(kv, [MLA_NOPE], axis=-1)
    scale = (MLA_NOPE + MLA_ROPE) ** -0.5
    n_blk = S // Q_BLOCK
    key_idx = jnp.arange(S)

    def one_block(args):
        qn, qp, start = args
        s = jnp.einsum('bqhd,bkhd->bhqk', qn, k_nope) + jnp.einsum('bqhr,bkr->bhqk', qp, k_pe)
        s = s.astype(jnp.float32) * scale
        q_idx = start + jnp.arange(Q_BLOCK)
        s = jnp.where(key_idx[None, :] <= q_idx[:, None], s, -jnp.inf)
        p = jax.nn.softmax(s, axis=-1).astype(v.dtype)
        return jnp.einsum('bhqk,bkhd->bqhd', p, v)

    qn_b = jnp.moveaxis(q_nope.reshape(B, n_blk, Q_BLOCK, MLA_HEADS, MLA_NOPE), 1, 0)
    qp_b = jnp.moveaxis(q_pe.reshape(B, n_blk, Q_BLOCK, MLA_HEADS, MLA_ROPE), 1, 0)
    starts = jnp.arange(n_blk) * Q_BLOCK
    o = lax.map(one_block, (qn_b, qp_b, starts))
    o = jnp.moveaxis(o, 0, 1).reshape(B, S, MLA_HEADS * MLA_V)
    return o @ w_o


def causal_depthwise_conv(x, w, b):
    K, C = w.shape
    y = lax.conv_general_dilated(x, w[:, None, :], window_strides=(1,), padding=[(K - 1, 0)],
                                 dimension_numbers=('NWC', 'WIO', 'NWC'), feature_group_count=C)
    return y + b


def blockdiag_proj(x, w):
    B, S, _ = x.shape
    xb = x.reshape(B, S, w.shape[0], MLSTM_QKV_BLOCK)
    return jnp.einsum('bsgi,gij->bsgj', xb, w).reshape(B, S, -1)


def mlstm_chunkwise(q, k, v, ig, lf):
    B, H, S, DH = q.shape
    L = MLSTM_CHUNK
    nc = S // L
    q = q.astype(jnp.float32)
    k = k.astype(jnp.float32) * (DH ** -0.5)
    v = v.astype(jnp.float32)

    def chunks(a):
        return jnp.moveaxis(a.reshape((B, H, nc, L) + a.shape[3:]), 2, 0)

    causal = jnp.tril(jnp.ones((L, L), dtype=bool))

    def step(carry, inp):
        C, n, m = carry
        qi, ki, vi, igi, lfi = inp
        b = jnp.cumsum(lfi, axis=-1)
        d = jnp.where(causal, b[..., :, None] - b[..., None, :] + igi[..., None, :], -jnp.inf)
        inter = b + m[..., None]
        m_i = jnp.maximum(inter, jnp.max(d, axis=-1))
        w_intra = jnp.exp(d - m_i[..., None])
        w_inter = jnp.exp(inter - m_i)
        s = jnp.einsum('bhid,bhjd->bhij', qi, ki) * w_intra
        num = jnp.einsum('bhij,bhje->bhie', s, vi) + w_inter[..., None] * jnp.einsum('bhid,bhde->bhie', qi, C)
        den = jnp.sum(s, axis=-1) + w_inter * jnp.einsum('bhid,bhd->bhi', qi, n)
        h = num / jnp.maximum(jnp.abs(den), jnp.exp(-m_i))[..., None]
        b_last = b[..., -1]
        g = b_last[..., None] - b + igi
        m_new = jnp.maximum(b_last + m, jnp.max(g, axis=-1))
        w_state = jnp.exp(g - m_new[..., None])
        decay = jnp.exp(b_last + m - m_new)
        C_new = decay[..., None, None] * C + jnp.einsum('bhj,bhjd,bhje->bhde', w_state, ki, vi)
        n_new = decay[..., None] * n + jnp.einsum('bhj,bhjd->bhd', w_state, ki)
        return (C_new, n_new, m_new), h

    init = (jnp.zeros((B, H, DH, DH), jnp.float32), jnp.zeros((B, H, DH), jnp.float32),
            jnp.zeros((B, H), jnp.float32))
    _, h = lax.scan(step, init, (chunks(q), chunks(k), chunks(v), chunks(ig), chunks(lf)))
    return jnp.moveaxis(h, 0, 2).reshape(B, H, S, DH)


def mlstm_mixer(x, w_in, conv_w, conv_b, w_q, w_k, w_v, w_ig, b_ig, w_fg, b_fg, g_out, skip, w_out):
    B, S, _ = x.shape
    xm, z = jnp.split(x @ w_in, 2, axis=-1)
    xc = jax.nn.silu(causal_depthwise_conv(xm, conv_w, conv_b))
    q = blockdiag_proj(xc, w_q)
    k = blockdiag_proj(xc, w_k)
    v = blockdiag_proj(xm, w_v)
    qkv = jnp.concatenate([q, k, v], axis=-1)
    ig = (qkv @ w_ig + b_ig).astype(jnp.float32)
    lf = jax.nn.log_sigmoid((qkv @ w_fg + b_fg).astype(jnp.float32))

    def heads(a):
        return a.reshape(B, S, MLSTM_HEADS, MLSTM_HD).transpose(0, 2, 1, 3)

    h = mlstm_chunkwise(heads(q), heads(k), heads(v), ig.transpose(0, 2, 1), lf.transpose(0, 2, 1))
    h = h.transpose(0, 2, 1, 3)
    mu = jnp.mean(h, axis=-1, keepdims=True)
    var = jnp.mean(jnp.square(h - mu), axis=-1, keepdims=True)
    h = ((h - mu) * lax.rsqrt(var + LN_EPS)).reshape(B, S, MLSTM_INNER) * g_out.astype(jnp.float32)
    h = (h.astype(x.dtype) + skip * xc) * jax.nn.silu(z)
    return h @ w_out


def grouped_moe(x, router_w, router_b, w_gate, w_up, w_down):
    B, S, _ = x.shape
    scores = jax.nn.sigmoid((x @ router_w).astype(jnp.float32))
    sel = (scores + router_b.astype(jnp.float32)).reshape(B, S, N_GROUPS, EXPERTS_PER_GROUP)
    top_vals, top_idx = lax.top_k(sel, TOP_K)
    group = jnp.argmax(jnp.sum(top_vals, axis=-1), axis=-1)
    idx = jnp.take_along_axis(top_idx, group[..., None, None], axis=2)[..., 0, :]
    expert_ids = group[..., None] * EXPERTS_PER_GROUP + idx
    w = jnp.take_along_axis(scores, expert_ids, axis=-1)
    w = w / jnp.sum(w, axis=-1, keepdims=True)
    gate = jnp.einsum('bsk,bske->bse', w, jax.nn.one_hot(expert_ids, N_EXPERTS, dtype=jnp.float32)).astype(x.dtype)
    y = jnp.zeros_like(x)
    for e in range(N_EXPERTS):
        h = jax.nn.silu(x @ w_gate[e]) * (x @ w_up[e])
        y = y + gate[..., e:e + 1] * (h @ w_down[e])
    return y


def setup_inputs(seed: int = 0) -> dict:
    key = jax.random.key(seed)
    ks = jax.random.split(key, 32)

    def nrm(k, shape, scale):
        return jax.random.normal(k, shape, jnp.float32) * scale

    nA, nB = N_MLA_LAYERS, N_MLSTM_LAYERS
    qk_dim = MLA_NOPE + MLA_ROPE
    offset = jax.random.randint(ks[1], (BATCH, 1), 0, MAX_POS_OFFSET, dtype=jnp.int32)
    positions = offset + jnp.arange(SEQ, dtype=jnp.int32)[None, :]
    b_fg = jnp.broadcast_to(jnp.linspace(3.0, 6.0, MLSTM_HEADS, dtype=jnp.float32), (nB, MLSTM_HEADS))
    return {
        "x": nrm(ks[0], (BATCH, SEQ, D_MODEL), 1.0),
        "positions": positions,
        "router_w": nrm(ks[2], (D_MODEL, N_EXPERTS), D_MODEL ** -0.5),
        "router_b": nrm(ks[3], (N_EXPERTS,), 0.01),
        "mla_w_in": nrm(ks[4], (nA, D_MODEL, MLA_Q_RANK + MLA_KV_RANK + MLA_ROPE), D_MODEL ** -0.5),
        "mla_g_q": 1.0 + nrm(ks[5], (nA, MLA_Q_RANK), 0.02),
        "mla_w_uq": nrm(ks[6], (nA, MLA_Q_RANK, MLA_HEADS * qk_dim), MLA_Q_RANK ** -0.5),
        "mla_g_kv": 1.0 + nrm(ks[7], (nA, MLA_KV_RANK), 0.02),
        "mla_w_ukv": nrm(ks[8], (nA, MLA_KV_RANK, MLA_HEADS * (MLA_NOPE + MLA_V)), MLA_KV_RANK ** -0.5),
        "mla_w_o": nrm(ks[9], (nA, MLA_HEADS * MLA_V, D_MODEL), DN_BETA * (MLA_HEADS * MLA_V) ** -0.5),
        "mlstm_w_in": nrm(ks[10], (nB, D_MODEL, 2 * MLSTM_INNER), D_MODEL ** -0.5),
        "mlstm_conv_w": nrm(ks[11], (nB, MLSTM_CONV, MLSTM_INNER), MLSTM_CONV ** -0.5),
        "mlstm_conv_b": nrm(ks[12], (nB, MLSTM_INNER), 0.02),
        "mlstm_w_q": nrm(ks[13], (nB, MLSTM_INNER // MLSTM_QKV_BLOCK, MLSTM_QKV_BLOCK, MLSTM_QKV_BLOCK), MLSTM_QKV_BLOCK ** -0.5),
        "mlstm_w_k": nrm(ks[14], (nB, MLSTM_INNER // MLSTM_QKV_BLOCK, MLSTM_QKV_BLOCK, MLSTM_QKV_BLOCK), MLSTM_QKV_BLOCK ** -0.5),
        "mlstm_w_v": nrm(ks[15], (nB, MLSTM_INNER // MLSTM_QKV_BLOCK, MLSTM_QKV_BLOCK, MLSTM_QKV_BLOCK), MLSTM_QKV_BLOCK ** -0.5),
        "mlstm_w_ig": nrm(ks[16], (nB, 3 * MLSTM_INNER, MLSTM_HEADS), 0.1 * (3 * MLSTM_INNER) ** -0.5),
        "mlstm_b_ig": nrm(ks[17], (nB, MLSTM_HEADS), 0.1),
        "mlstm_w_fg": nrm(ks[18], (nB, 3 * MLSTM_INNER, MLSTM_HEADS), 0.1 * (3 * MLSTM_INNER) ** -0.5),
        "mlstm_b_fg": b_fg + nrm(ks[19], (nB, MLSTM_HEADS), 0.1),
        "mlstm_g_out": 1.0 + nrm(ks[20], (nB, MLSTM_INNER), 0.02),
        "mlstm_skip": 1.0 + nrm(ks[21], (nB, MLSTM_INNER), 0.02),
        "mlstm_w_out": nrm(ks[22], (nB, MLSTM_INNER, D_MODEL), DN_BETA * MLSTM_INNER ** -0.5),
        "moe_w_gate": nrm(ks[23], (DEPTH, N_EXPERTS, D_MODEL, EXPERT_FF), D_MODEL ** -0.5),
        "moe_w_up": nrm(ks[24], (DEPTH, N_EXPERTS, D_MODEL, EXPERT_FF), D_MODEL ** -0.5),
        "moe_w_down": nrm(ks[25], (DEPTH, N_EXPERTS, EXPERT_FF, D_MODEL), DN_BETA * EXPERT_FF ** -0.5),
        "ln_g": 1.0 + nrm(ks[26], (DEPTH, 2, D_MODEL), 0.02),
        "ln_b": nrm(ks[27], (DEPTH, 2, D_MODEL), 0.02),
    }


def reference(x, positions, router_w, router_b,
              mla_w_in, mla_g_q, mla_w_uq, mla_g_kv, mla_w_ukv, mla_w_o,
              mlstm_w_in, mlstm_conv_w, mlstm_conv_b, mlstm_w_q, mlstm_w_k, mlstm_w_v,
              mlstm_w_ig, mlstm_b_ig, mlstm_w_fg, mlstm_b_fg, mlstm_g_out, mlstm_skip, mlstm_w_out,
              moe_w_gate, moe_w_up, moe_w_down, ln_g, ln_b):
    cos, sin = rope_tables(positions)
    for i in range(DEPTH):
        j = i // N_MIXERS
        if i % N_MIXERS == 0:
            y = mla_mixer(x, cos, sin, mla_w_in[j], mla_g_q[j], mla_w_uq[j], mla_g_kv[j],
                          mla_w_ukv[j], mla_w_o[j])
        else:
            y = mlstm_mixer(x, mlstm_w_in[j], mlstm_conv_w[j], mlstm_conv_b[j], mlstm_w_q[j],
                            mlstm_w_k[j], mlstm_w_v[j], mlstm_w_ig[j], mlstm_b_ig[j], mlstm_w_fg[j],
                            mlstm_b_fg[j], mlstm_g_out[j], mlstm_skip[j], mlstm_w_out[j])
        x = layer_norm(DN_ALPHA * x + y, ln_g[i, 0], ln_b[i, 0])
        y = grouped_moe(x, router_w, router_b, moe_w_gate[i], moe_w_up[i], moe_w_down[i])
        x = layer_norm(DN_ALPHA * x + y, ln_g[i, 1], ln_b[i, 1])
    return x
```

```python
import functools

import jax
import jax.numpy as jnp
from jax import lax
from jax.experimental import pallas as pl
from jax.experimental.pallas import tpu as pltpu

D_MODEL = 1024
DEPTH = 4
N_MIXERS = 2
MLA_HEADS = 8
MLA_Q_RANK = 256
MLA_KV_RANK = 256
MLA_NOPE = 128
MLA_ROPE = 64
MLA_V = 128
ROPE_THETA = 10000.0
MLSTM_INNER = 2 * D_MODEL
MLSTM_HEADS = 4
MLSTM_HD = MLSTM_INNER // MLSTM_HEADS
MLSTM_CONV = 4
MLSTM_QKV_BLOCK = 4
N_EXPERTS = 16
N_GROUPS = 4
EXPERTS_PER_GROUP = N_EXPERTS // N_GROUPS
EXPERT_FF = 512
DN_ALPHA = (2.0 * DEPTH) ** 0.25
LN_EPS = 1e-5

LANES = 128
MXU_DIM = 256
HEAD_SLOT = 2 * LANES
N_PAIRS = EXPERTS_PER_GROUP * (EXPERTS_PER_GROUP - 1) // 2
N_CLASSES = N_GROUPS * N_PAIRS
VMEM_LIMIT = 56 * 1024 * 1024

F32 = jnp.float32
BF16 = jnp.bfloat16


def _tiles(seq):
    return dict(
        rope=min(seq, 2048),
        proj=min(seq, 512),
        attn=min(seq, 512),
        post=min(seq, 512),
        pre=min(seq, 256),
        chunk=min(seq, 256),
        moe=256,
    )


def _params(*sem):
    return pltpu.CompilerParams(dimension_semantics=sem, vmem_limit_bytes=VMEM_LIMIT)


def _dot(a, b):
    return jnp.dot(a, b, preferred_element_type=F32)


def _dot_nt(a, b):
    return lax.dot_general(a, b, (((1,), (1,)), ((), ())), preferred_element_type=F32)


def _dot_tn(a, b):
    return lax.dot_general(a, b, (((0,), (0,)), ((), ())), preferred_element_type=F32)


def _layer_norm(v, g, b):
    mu = jnp.mean(v, axis=-1, keepdims=True)
    c = v - mu
    var = jnp.mean(c * c, axis=-1, keepdims=True)
    return c * lax.rsqrt(var + LN_EPS) * g + b


def _rms_norm(v, g):
    return v * lax.rsqrt(jnp.mean(v * v, axis=-1, keepdims=True) + LN_EPS) * g


def _sigmoid(v):
    return 1.0 / (1.0 + jnp.exp(-v))


def _silu(v):
    return v * _sigmoid(v)


def _rope_kernel(pos_ref, invf_ref, cos_ref, sin_ref):
    ang = pos_ref[...].astype(F32) * invf_ref[...]
    cos_ref[...] = jnp.cos(ang)
    sin_ref[...] = jnp.sin(ang)


def _rope_tables(positions, tm):
    t = positions.size
    inv_freq = 1.0 / (ROPE_THETA ** (jnp.arange(0, MLA_ROPE, 2, dtype=F32) / MLA_ROPE))
    invf = jnp.tile(inv_freq, LANES // (MLA_ROPE // 2)).reshape(1, LANES)
    out = jax.ShapeDtypeStruct((t, LANES), F32)
    return pl.pallas_call(
        _rope_kernel,
        grid=(t // tm,),
        in_specs=[pl.BlockSpec((tm, 1), lambda i: (i, 0)),
                  pl.BlockSpec((1, LANES), lambda i: (0, 0))],
        out_specs=[pl.BlockSpec((tm, LANES), lambda i: (i, 0))] * 2,
        out_shape=[out, out],
        compiler_params=_params("parallel"),
    )(positions.reshape(t, 1), invf)


def _mla_proj_kernel(x_ref, cos_ref, sin_ref, w_in_ref, gq_ref, gkv_ref, wq1_ref, wq2_ref,
                     wkv_ref, q_ref, k_ref, v_ref):
    xb = x_ref[...].astype(BF16)
    c = _dot(xb, w_in_ref[...])
    cq = _rms_norm(c[:, :MLA_Q_RANK], gq_ref[...]).astype(BF16)
    ckv = _rms_norm(c[:, MLA_Q_RANK:MLA_Q_RANK + MLA_KV_RANK], gkv_ref[...]).astype(BF16)
    cos = cos_ref[...]
    sin = sin_ref[...]
    base = MLA_Q_RANK + MLA_KV_RANK
    k_pe = (c[:, base:base + LANES] * cos + c[:, base + LANES:base + 2 * LANES] * sin).astype(BF16)
    scale = (MLA_NOPE + MLA_ROPE) ** -0.5
    cos_s = cos * scale
    sin_s = sin * scale
    q1 = _dot(cq, wq1_ref[...])
    q2 = _dot(cq, wq2_ref[...])
    kv = _dot(ckv, wkv_ref[...])
    for h in range(MLA_HEADS):
        s0 = h * HEAD_SLOT
        q_ref[:, s0:s0 + LANES] = (q1[:, s0:s0 + LANES] * scale).astype(BF16)
        q_ref[:, s0 + LANES:s0 + HEAD_SLOT] = (
            q1[:, s0 + LANES:s0 + HEAD_SLOT] * cos_s
            + q2[:, h * LANES:(h + 1) * LANES] * sin_s).astype(BF16)
        k_ref[:, s0:s0 + LANES] = kv[:, h * LANES:(h + 1) * LANES].astype(BF16)
        k_ref[:, s0 + LANES:s0 + HEAD_SLOT] = k_pe
    v_ref[...] = kv[:, MLA_HEADS * MLA_NOPE:].astype(BF16)


def _mla_proj(x, cos, sin, w, tm):
    t = x.shape[0]
    row = lambda n: pl.BlockSpec((tm, n), lambda i: (i, 0))
    full = lambda a: pl.BlockSpec(a.shape, lambda i: (0,) * a.ndim)
    consts = (w["w_in"], w["g_q"], w["g_kv"], w["w_q1"], w["w_q2"], w["w_kv"])
    return pl.pallas_call(
        _mla_proj_kernel,
        grid=(t // tm,),
        in_specs=[row(D_MODEL), row(LANES), row(LANES)] + [full(a) for a in consts],
        out_specs=[row(MLA_HEADS * HEAD_SLOT), row(MLA_HEADS * HEAD_SLOT), row(MLA_HEADS * MLA_V)],
        out_shape=[jax.ShapeDtypeStruct((t, MLA_HEADS * HEAD_SLOT), BF16),
                   jax.ShapeDtypeStruct((t, MLA_HEADS * HEAD_SLOT), BF16),
                   jax.ShapeDtypeStruct((t, MLA_HEADS * MLA_V), BF16)],
        compiler_params=_params("parallel"),
    )(x, cos, sin, *consts)


def _attn_kernel(q_ref, k_ref, v_ref, o_ref, m_ref, l_ref, acc_ref, *, tile):
    qi = pl.program_id(2)
    q = q_ref[...]
    m_ref[...] = jnp.full(m_ref.shape, -jnp.inf, F32)
    l_ref[...] = jnp.zeros(l_ref.shape, F32)
    acc_ref[...] = jnp.zeros(acc_ref.shape, F32)

    def block(ki, masked):
        start = pl.multiple_of(ki * tile, tile)
        s = _dot_nt(q, k_ref[pl.ds(start, tile), :])
        if masked:
            rows = lax.broadcasted_iota(jnp.int32, s.shape, 0)
            cols = lax.broadcasted_iota(jnp.int32, s.shape, 1)
            s = jnp.where(cols <= rows, s, -jnp.inf)
        m_old = m_ref[...]
        m_new = jnp.maximum(m_old, jnp.max(s, axis=-1, keepdims=True))
        p = jnp.exp(s - m_new)
        alpha = jnp.exp(m_old - m_new)
        l_ref[...] = alpha * l_ref[...] + jnp.sum(p, axis=-1, keepdims=True)
        acc_ref[...] = alpha * acc_ref[...] + _dot(p.astype(BF16), v_ref[pl.ds(start, tile), :])
        m_ref[...] = m_new

    def body(ki, carry):
        block(ki, False)
        return carry

    lax.fori_loop(0, qi, body, 0)
    block(qi, True)
    o_ref[...] = (acc_ref[...] / l_ref[...]).astype(o_ref.dtype)


def _attention(q, k, v, batch, seq, tile):
    t = q.shape[0]
    nq = seq // tile
    return pl.pallas_call(
        functools.partial(_attn_kernel, tile=tile),
        grid=(batch, MLA_HEADS, nq),
        in_specs=[pl.BlockSpec((tile, HEAD_SLOT), lambda b, h, i: (b * nq + i, h)),
                  pl.BlockSpec((seq, HEAD_SLOT), lambda b, h, i: (b, h)),
                  pl.BlockSpec((seq, MLA_V), lambda b, h, i: (b, h))],
        out_specs=pl.BlockSpec((tile, MLA_V), lambda b, h, i: (b * nq + i, h)),
        out_shape=jax.ShapeDtypeStruct((t, MLA_HEADS * MLA_V), BF16),
        scratch_shapes=[pltpu.VMEM((tile, 1), F32), pltpu.VMEM((tile, 1), F32),
                        pltpu.VMEM((tile, MLA_V), F32)],
        compiler_params=_params("parallel", "parallel", "arbitrary"),
    )(q, k, v)


def _route_rows(sel):
    best = None
    for g in range(N_GROUPS):
        a = [sel[g * EXPERTS_PER_GROUP + j:g * EXPERTS_PER_GROUP + j + 1, :]
             for j in range(EXPERTS_PER_GROUP)]
        v1 = functools.reduce(jnp.maximum, a)
        i1 = jnp.full(v1.shape, EXPERTS_PER_GROUP - 1, jnp.int32)
        for j in reversed(range(EXPERTS_PER_GROUP - 1)):
            i1 = jnp.where(a[j] == v1, j, i1)
        r = [jnp.where(i1 == j, -jnp.inf, a[j]) for j in range(EXPERTS_PER_GROUP)]
        v2 = functools.reduce(jnp.maximum, r)
        i2 = jnp.full(v1.shape, EXPERTS_PER_GROUP - 1, jnp.int32)
        for j in reversed(range(EXPERTS_PER_GROUP - 1)):
            i2 = jnp.where(r[j] == v2, j, i2)
        total = v1 + v2
        if best is None:
            best = (total, jnp.zeros_like(i1), i1, i2)
        else:
            better = total > best[0]
            best = (jnp.where(better, total, best[0]), jnp.where(better, g, best[1]),
                    jnp.where(better, i1, best[2]), jnp.where(better, i2, best[3]))
    _, grp, i1, i2 = best
    lo = jnp.minimum(i1, i2)
    hi = jnp.maximum(i1, i2)
    pair = jnp.where(lo == 0, hi - 1, jnp.where(lo == 1, hi + 1, N_PAIRS - 1))
    return grp * N_PAIRS + pair


def _post_kernel(a_ref, w_ref, x_ref, g_ref, b_ref, rwt_ref, rb_ref, y_ref, cls_ref):
    y = _dot(a_ref[...], w_ref[...])
    x1 = _layer_norm(DN_ALPHA * x_ref[...] + y, g_ref[...], b_ref[...])
    y_ref[...] = x1
    logits = _dot_nt(rwt_ref[...], x1.astype(BF16))
    cls_ref[...] = _route_rows(_sigmoid(logits) + rb_ref[...])


def _post(a, w, x, g, b, rwt, rb, tm):
    t, k = a.shape
    row = lambda n: pl.BlockSpec((tm, n), lambda i: (i, 0))
    full = lambda arr: pl.BlockSpec(arr.shape, lambda i: (0,) * arr.ndim)
    return pl.pallas_call(
        _post_kernel,
        grid=(t // tm,),
        in_specs=[row(k), full(w), row(D_MODEL), full(g), full(b), full(rwt), full(rb)],
        out_specs=[row(D_MODEL), pl.BlockSpec((1, tm), lambda i: (0, i))],
        out_shape=[jax.ShapeDtypeStruct((t, D_MODEL), F32),
                   jax.ShapeDtypeStruct((1, t), jnp.int32)],
        compiler_params=_params("parallel"),
    )(a, w, x, g, b, rwt, rb)


def _expert_kernel(ea_ref, eb_ref, nblk_ref, x_ref, rw_ref, wga_ref, wua_ref, wda_ref,
                   wgb_ref, wub_ref, wdb_ref, g_ref, b_ref, o_ref):
    i = pl.program_id(0)

    @pl.when(i < nblk_ref[0])
    def _():
        x = x_ref[...]
        xb = x.astype(BF16)
        sig = _sigmoid(_dot(xb, rw_ref[...]))
        lane = lax.broadcasted_iota(jnp.int32, sig.shape, 1)
        sa = jnp.sum(jnp.where(lane == ea_ref[i], sig, 0.0), axis=-1, keepdims=True)
        sb = jnp.sum(jnp.where(lane == eb_ref[i], sig, 0.0), axis=-1, keepdims=True)
        tot = sa + sb
        ha = _silu(_dot(xb, wga_ref[...])) * _dot(xb, wua_ref[...])
        hb = _silu(_dot(xb, wgb_ref[...])) * _dot(xb, wub_ref[...])
        y = (sa / tot) * _dot(ha.astype(BF16), wda_ref[...]) \
            + (sb / tot) * _dot(hb.astype(BF16), wdb_ref[...])
        o_ref[...] = _layer_norm(DN_ALPHA * x + y, g_ref[...], b_ref[...])

    @pl.when(i >= nblk_ref[0])
    def _():
        o_ref[...] = jnp.zeros(o_ref.shape, o_ref.dtype)


def _experts(xs, ea, eb, nblk, rw, wg, wu, wd, g, b, bm):
    p = xs.shape[0]
    sel = lambda which: (lambda i, ea, eb, nb: ((ea if which == 0 else eb)[i], 0, 0))
    up = lambda which: pl.BlockSpec((None, D_MODEL, EXPERT_FF), sel(which))
    down = lambda which: pl.BlockSpec((None, EXPERT_FF, D_MODEL), sel(which))
    full = lambda arr: pl.BlockSpec(arr.shape, lambda i, ea, eb, nb: (0,) * arr.ndim)
    row = pl.BlockSpec((bm, D_MODEL), lambda i, ea, eb, nb: (i, 0))
    return pl.pallas_call(
        _expert_kernel,
        grid_spec=pltpu.PrefetchScalarGridSpec(
            num_scalar_prefetch=3,
            grid=(p // bm,),
            in_specs=[row, full(rw), up(0), up(0), down(0), up(1), up(1), down(1), full(g), full(b)],
            out_specs=row),
        out_shape=jax.ShapeDtypeStruct((p, D_MODEL), F32),
        compiler_params=_params("arbitrary"),
    )(ea, eb, nblk, xs, rw, wg, wu, wd, wg, wu, wd, g, b)


def _row_gather(src, idx):
    return jnp.take(src, idx, axis=0)


def _moe(x1, cls, rw, wg, wu, wd, g, b, bm):
    t = x1.shape[0]
    p_rows = t + N_CLASSES * bm
    nblk = p_rows // bm
    classes = jnp.arange(N_CLASSES, dtype=jnp.int32)
    counts = jnp.sum((cls[:, None] == classes[None, :]).astype(jnp.int32), axis=0)
    padded = ((counts + bm - 1) // bm) * bm
    off = jnp.cumsum(counts) - counts
    pend = jnp.cumsum(padded)
    poff = pend - padded
    order = jnp.argsort(cls).astype(jnp.int32)
    blk_start = jnp.arange(nblk, dtype=jnp.int32) * bm
    blk_cls = jnp.minimum(jnp.searchsorted(pend, blk_start, side="right"), N_CLASSES - 1).astype(jnp.int32)
    n_used = (pend[-1] // bm).astype(jnp.int32).reshape(1)
    pos_cls = jnp.repeat(blk_cls, bm)
    j = jnp.arange(p_rows, dtype=jnp.int32) - poff[pos_cls]
    src = order[jnp.clip(off[pos_cls] + jnp.minimum(j, counts[pos_cls] - 1), 0, t - 1)]
    cls_sorted = cls[order]
    dest_sorted = poff[cls_sorted] + jnp.arange(t, dtype=jnp.int32) - off[cls_sorted]
    dest = jnp.zeros((t,), jnp.int32).at[order].set(dest_sorted)
    grp = blk_cls // N_PAIRS
    pair = blk_cls % N_PAIRS
    lo = jnp.where(pair < 3, 0, jnp.where(pair < 5, 1, 2))
    hi = jnp.where(pair < 3, pair + 1, jnp.where(pair < 5, pair - 1, 3))
    ea = (grp * EXPERTS_PER_GROUP + lo).astype(jnp.int32)
    eb = (grp * EXPERTS_PER_GROUP + hi).astype(jnp.int32)
    xs = _row_gather(x1, src)
    ys = _experts(xs, ea, eb, n_used, rw, wg, wu, wd, g, b, bm)
    return _row_gather(ys, dest)


def _mlstm_pre_kernel(x_ref, w_in_ref, cw_ref, cb_ref, bdqk_ref, bdv_ref, wgate_ref, bgate_ref,
                      q_ref, k_ref, v_ref, xc_ref, sz_ref, gate_ref, ext_ref, *, tm):
    halo = 8
    xb = x_ref[...].astype(BF16)

    @pl.when(pl.program_id(1) == 0)
    def _():
        ext_ref[0:halo, :] = jnp.zeros((halo, MLSTM_INNER), F32)

    @pl.when(pl.program_id(1) != 0)
    def _():
        ext_ref[0:halo, :] = ext_ref[tm:tm + halo, :]

    xm = _dot(xb, w_in_ref[:, :MLSTM_INNER])
    ext_ref[halo:halo + tm, :] = xm
    conv = cb_ref[...] + cw_ref[MLSTM_CONV - 1:MLSTM_CONV, :] * xm
    for kk in range(MLSTM_CONV - 1):
        lag = MLSTM_CONV - 1 - kk
        conv = conv + cw_ref[kk:kk + 1, :] * ext_ref[halo - lag:halo - lag + tm, :]
    xc = _silu(conv)
    xc_b = xc.astype(BF16)
    xm_b = xm.astype(BF16)
    xc_ref[...] = xc_b
    sz_ref[...] = _silu(_dot(xb, w_in_ref[:, MLSTM_INNER:])).astype(BF16)

    gates = bgate_ref[...]
    k_scale = MLSTM_HD ** -0.5
    for c in range(MLSTM_INNER // MXU_DIM):
        lo, hi = c * MXU_DIM, (c + 1) * MXU_DIM
        qk = _dot(xc_b[:, lo:hi], bdqk_ref[c])
        vv = _dot(xm_b[:, lo:hi], bdv_ref[c])
        qb = qk[:, :MXU_DIM].astype(BF16)
        kb = qk[:, MXU_DIM:].astype(BF16)
        vb = vv.astype(BF16)
        q_ref[:, lo:hi] = qb
        k_ref[:, lo:hi] = (qk[:, MXU_DIM:] * k_scale).astype(BF16)
        v_ref[:, lo:hi] = vb
        gates = gates + _dot(qb, wgate_ref[lo:hi, :]) \
            + _dot(kb, wgate_ref[MLSTM_INNER + lo:MLSTM_INNER + hi, :]) \
            + _dot(vb, wgate_ref[2 * MLSTM_INNER + lo:2 * MLSTM_INNER + hi, :])
    fg = gates[:, LANES:]
    gate_ref[:, :LANES] = gates[:, :LANES]
    gate_ref[:, LANES:] = jnp.minimum(fg, 0.0) - jnp.log(1.0 + jnp.exp(-jnp.abs(fg)))


def _mlstm_pre(x, w, batch, seq, tm):
    t = x.shape[0]
    ns = seq // tm
    row = lambda n: pl.BlockSpec((tm, n), lambda b, s: (b * ns + s, 0))
    full = lambda a: pl.BlockSpec(a.shape, lambda b, s: (0,) * a.ndim)
    consts = (w["w_in"], w["conv_w"], w["conv_b"], w["bd_qk"], w["bd_v"], w["w_gate"], w["b_gate"])
    act = jax.ShapeDtypeStruct((t, MLSTM_INNER), BF16)
    return pl.pallas_call(
        functools.partial(_mlstm_pre_kernel, tm=tm),
        grid=(batch, ns),
        in_specs=[row(D_MODEL)] + [full(a) for a in consts],
        out_specs=[row(MLSTM_INNER)] * 5 + [row(2 * LANES)],
        out_shape=[act] * 5 + [jax.ShapeDtypeStruct((t, 2 * LANES), F32)],
        scratch_shapes=[pltpu.VMEM((tm + 8, MLSTM_INNER), F32)],
        compiler_params=_params("parallel", "arbitrary"),
    )(x, *consts)


def _split3(v):
    a = v.astype(BF16)
    r = v - a.astype(F32)
    b = r.astype(BF16)
    c = (r - b.astype(F32)).astype(BF16)
    return a, b, c


def _mlstm_chunk_kernel(q_ref, k_ref, v_ref, gate_ref, xc_ref, sz_ref, gout_ref, skip_ref, o_ref,
                        c_ref, n_ref, m_ref, *, chunk):
    @pl.when(pl.program_id(1) == 0)
    def _():
        c_ref[...] = jnp.zeros(c_ref.shape, F32)
        n_ref[...] = jnp.zeros(n_ref.shape, F32)
        m_ref[...] = jnp.zeros(m_ref.shape, F32)

    ig_all = gate_ref[:, :LANES]
    lf_all = gate_ref[:, LANES:]
    rows = lax.broadcasted_iota(jnp.int32, (chunk, chunk), 0)
    cols = lax.broadcasted_iota(jnp.int32, (chunk, chunk), 1)
    causal = cols <= rows
    tril = jnp.where(causal, 1.0, 0.0).astype(BF16)
    b_all = functools.reduce(lambda u, w: u + w, [_dot(tril, piece) for piece in _split3(lf_all)])
    row_all = jnp.transpose(ig_all - b_all)

    for h in range(MLSTM_HEADS):
        lo, hi = h * MLSTM_HD, (h + 1) * MLSTM_HD
        q = q_ref[:, lo:hi]
        k = k_ref[:, lo:hi]
        v = v_ref[:, lo:hi]
        b = b_all[:, h:h + 1]
        ig = ig_all[:, h:h + 1]
        m_prev = m_ref[h]
        d = jnp.where(causal, b + row_all[h:h + 1, :], -jnp.inf)
        inter = b + m_prev
        m_i = jnp.maximum(inter, jnp.max(d, axis=-1, keepdims=True))
        w_intra = jnp.exp(d - m_i)
        w_inter = jnp.exp(inter - m_i)
        s = _dot_nt(q, k) * w_intra
        c_old = c_ref[h]
        n_old = n_ref[h]
        num = _dot(s.astype(BF16), v) + w_inter * _dot(q, c_old.astype(BF16))
        qn = jnp.sum(q.astype(F32) * n_old, axis=-1, keepdims=True)
        den = jnp.sum(s, axis=-1, keepdims=True) + w_inter * qn
        hh = num / jnp.maximum(jnp.abs(den), jnp.exp(-m_i))

        b_last = b[chunk - 1:chunk, :]
        gg = b_last - b + ig
        m_new = jnp.maximum(b_last + m_prev, jnp.max(gg, axis=0, keepdims=True))
        kw = k.astype(F32) * jnp.exp(gg - m_new)
        decay = jnp.exp(b_last + m_prev - m_new)
        c_ref[h] = decay * c_old + _dot_tn(kw.astype(BF16), v)
        n_ref[h] = decay * n_old + jnp.sum(kw, axis=0, keepdims=True)
        m_ref[h] = m_new

        mu = jnp.mean(hh, axis=-1, keepdims=True)
        cen = hh - mu
        var = jnp.mean(cen * cen, axis=-1, keepdims=True)
        hn = cen * lax.rsqrt(var + LN_EPS) * gout_ref[:, lo:hi]
        o_ref[:, lo:hi] = ((hn + skip_ref[:, lo:hi] * xc_ref[:, lo:hi].astype(F32))
                           * sz_ref[:, lo:hi].astype(F32)).astype(o_ref.dtype)


def _mlstm_chunks(q, k, v, gates, xc, sz, g_out, skip, batch, seq, chunk):
    t = q.shape[0]
    nc = seq // chunk
    row = lambda n: pl.BlockSpec((chunk, n), lambda b, c: (b * nc + c, 0))
    full = lambda a: pl.BlockSpec(a.shape, lambda b, c: (0,) * a.ndim)
    return pl.pallas_call(
        functools.partial(_mlstm_chunk_kernel, chunk=chunk),
        grid=(batch, nc),
        in_specs=[row(MLSTM_INNER)] * 3 + [row(2 * LANES)] + [row(MLSTM_INNER)] * 2
                 + [full(g_out), full(skip)],
        out_specs=row(MLSTM_INNER),
        out_shape=jax.ShapeDtypeStruct((t, MLSTM_INNER), BF16),
        scratch_shapes=[pltpu.VMEM((MLSTM_HEADS, MLSTM_HD, MLSTM_HD), F32),
                        pltpu.VMEM((MLSTM_HEADS, 1, MLSTM_HD), F32),
                        pltpu.VMEM((MLSTM_HEADS, 1, 1), F32)],
        compiler_params=_params("parallel", "arbitrary"),
    )(q, k, v, gates, xc, sz, g_out, skip)


def _rot_half(w):
    half = MLA_ROPE // 2
    return jnp.concatenate([-w[..., half:], w[..., :half]], axis=-1)


def _prep_mla(w_in, g_q, w_uq, g_kv, w_ukv):
    pad = jnp.zeros((D_MODEL, LANES - MLA_ROPE), F32)
    w_pe = w_in[:, MLA_Q_RANK + MLA_KV_RANK:]
    w_in_p = jnp.concatenate([w_in[:, :MLA_Q_RANK + MLA_KV_RANK], w_pe, pad, _rot_half(w_pe), pad], axis=1)
    uq = w_uq.reshape(MLA_Q_RANK, MLA_HEADS, MLA_NOPE + MLA_ROPE)
    zq = jnp.zeros((MLA_Q_RANK, MLA_HEADS, LANES - MLA_ROPE), F32)
    w_q1 = jnp.concatenate([uq, zq], axis=-1).reshape(MLA_Q_RANK, MLA_HEADS * HEAD_SLOT)
    w_q2 = jnp.concatenate([_rot_half(uq[..., MLA_NOPE:]), zq], axis=-1).reshape(MLA_Q_RANK, MLA_HEADS * LANES)
    ukv = w_ukv.reshape(MLA_KV_RANK, MLA_HEADS, MLA_NOPE + MLA_V)
    w_kv = jnp.concatenate([ukv[..., :MLA_NOPE].reshape(MLA_KV_RANK, -1),
                            ukv[..., MLA_NOPE:].reshape(MLA_KV_RANK, -1)], axis=1)
    return dict(w_in=w_in_p.astype(BF16), g_q=g_q.reshape(1, -1), g_kv=g_kv.reshape(1, -1),
                w_q1=w_q1.astype(BF16), w_q2=w_q2.astype(BF16), w_kv=w_kv.astype(BF16))


def _block_diag_tiles(w):
    per = MXU_DIM // MLSTM_QKV_BLOCK
    w = w.reshape(MLSTM_INNER // MXU_DIM, per, MLSTM_QKV_BLOCK, MLSTM_QKV_BLOCK)
    eye = jnp.eye(per, dtype=w.dtype)
    return jnp.einsum("cgij,gh->cgihj", w, eye).reshape(MLSTM_INNER // MXU_DIM, MXU_DIM, MXU_DIM)


def _prep_mlstm(w_in, conv_w, conv_b, w_q, w_k, w_v, w_ig, b_ig, w_fg, b_fg):
    bd_qk = jnp.concatenate([_block_diag_tiles(w_q), _block_diag_tiles(w_k)], axis=-1)
    zw = jnp.zeros((3 * MLSTM_INNER, LANES - MLSTM_HEADS), F32)
    zb = jnp.zeros((LANES - MLSTM_HEADS,), F32)
    w_gate = jnp.concatenate([w_ig, zw, w_fg, zw], axis=1)
    b_gate = jnp.concatenate([b_ig, zb, b_fg, zb]).reshape(1, 2 * LANES)
    return dict(w_in=w_in.astype(BF16), conv_w=conv_w, conv_b=conv_b.reshape(1, -1),
                bd_qk=bd_qk.astype(BF16), bd_v=_block_diag_tiles(w_v).astype(BF16),
                w_gate=w_gate.astype(BF16), b_gate=b_gate)


def kernel(x, positions, router_w, router_b, mla_w_in, mla_g_q, mla_w_uq, mla_g_kv, mla_w_ukv, mla_w_o, mlstm_w_in, mlstm_conv_w, mlstm_conv_b, mlstm_w_q, mlstm_w_k, mlstm_w_v, mlstm_w_ig, mlstm_b_ig, mlstm_w_fg, mlstm_b_fg, mlstm_g_out, mlstm_skip, mlstm_w_out, moe_w_gate, moe_w_up, moe_w_down, ln_g, ln_b):
    batch, seq, _ = x.shape
    t = batch * seq
    tiles = _tiles(seq)
    cos, sin = _rope_tables(positions, tiles["rope"])
    rwt = router_w.T.astype(BF16)
    rb = router_b.reshape(N_EXPERTS, 1)
    rw_pad = jnp.concatenate([router_w, jnp.zeros((D_MODEL, LANES - N_EXPERTS), F32)], axis=1).astype(BF16)
    xt = x.reshape(t, D_MODEL)
    for i in range(DEPTH):
        j = i // N_MIXERS
        g0, b0 = ln_g[i, 0].reshape(1, -1), ln_b[i, 0].reshape(1, -1)
        g1, b1 = ln_g[i, 1].reshape(1, -1), ln_b[i, 1].reshape(1, -1)
        if i % N_MIXERS == 0:
            w = _prep_mla(mla_w_in[j], mla_g_q[j], mla_w_uq[j], mla_g_kv[j], mla_w_ukv[j])
            q, k, v = _mla_proj(xt, cos, sin, w, tiles["proj"])
            a = _attention(q, k, v, batch, seq, tiles["attn"])
            w_out = mla_w_o[j].astype(BF16)
        else:
            w = _prep_mlstm(mlstm_w_in[j], mlstm_conv_w[j], mlstm_conv_b[j], mlstm_w_q[j], mlstm_w_k[j],
                            mlstm_w_v[j], mlstm_w_ig[j], mlstm_b_ig[j], mlstm_w_fg[j], mlstm_b_fg[j])
            q, k, v, xc, sz, gates = _mlstm_pre(xt, w, batch, seq, tiles["pre"])
            a = _mlstm_chunks(q, k, v, gates, xc, sz, mlstm_g_out[j].reshape(1, -1),
                              mlstm_skip[j].reshape(1, -1), batch, seq, tiles["chunk"])
            w_out = mlstm_w_out[j].astype(BF16)
        x1, cls = _post(a, w_out, xt, g0, b0, rwt, rb, tiles["post"])
        xt = _moe(x1, cls.reshape(t), rw_pad, moe_w_gate[i].astype(BF16), moe_w_up[i].astype(BF16),
                  moe_w_down[i].astype(BF16), g1, b1, tiles["moe"])
    return xt.reshape(batch, seq, D_MODEL)
```

```python
import functools

import jax
import jax.numpy as jnp
from jax import lax
from jax.experimental import pallas as pl
from jax.experimental.pallas import tpu as pltpu

D_MODEL = 1024
DEPTH = 4
N_MIXERS = 2
MLA_HEADS = 8
MLA_Q_RANK = 256
MLA_KV_RANK = 256
MLA_NOPE = 128
MLA_ROPE = 64
MLA_V = 128
ROPE_THETA = 10000.0
MLSTM_INNER = 2 * D_MODEL
MLSTM_HEADS = 4
MLSTM_HD = MLSTM_INNER // MLSTM_HEADS
MLSTM_CONV = 4
MLSTM_QKV_BLOCK = 4
N_EXPERTS = 16
N_GROUPS = 4
EXPERTS_PER_GROUP = N_EXPERTS // N_GROUPS
EXPERT_FF = 512
DN_ALPHA = (2.0 * DEPTH) ** 0.25
LN_EPS = 1e-5
LOG2_E = 1.4426950408889634

LANES = 128
MXU_DIM = 256
HEAD_SLOT = 2 * LANES
ATTN_HEADS_PER_STEP = 4
N_PAIRS = EXPERTS_PER_GROUP * (EXPERTS_PER_GROUP - 1) // 2
N_CLASSES = N_GROUPS * N_PAIRS
VMEM_LIMIT = 56 * 1024 * 1024

F32 = jnp.float32
BF16 = jnp.bfloat16


def _tiles(seq):
    return dict(
        rope=min(seq, 2048),
        attn=min(seq, 512),
        post=min(seq, 512),
        pre=min(seq, 256),
        chunk=min(seq, 256),
        moe=256,
    )


def _params(*sem):
    return pltpu.CompilerParams(dimension_semantics=sem, vmem_limit_bytes=VMEM_LIMIT)


def _dot(a, b):
    return jnp.dot(a, b, preferred_element_type=F32)


def _dot_nt(a, b):
    return lax.dot_general(a, b, (((1,), (1,)), ((), ())), preferred_element_type=F32)


def _dot_tn(a, b):
    return lax.dot_general(a, b, (((0,), (0,)), ((), ())), preferred_element_type=F32)


def _layer_norm(v, g, b):
    mu = jnp.mean(v, axis=-1, keepdims=True)
    c = v - mu
    var = jnp.mean(c * c, axis=-1, keepdims=True)
    return c * lax.rsqrt(var + LN_EPS) * g + b


def _rms_norm(v, g):
    return v * lax.rsqrt(jnp.mean(v * v, axis=-1, keepdims=True) + LN_EPS) * g


def _sigmoid(v):
    return 1.0 / (1.0 + jnp.exp(-v))


def _silu(v):
    return v * _sigmoid(v)


def _rope_kernel(pos_ref, invf_ref, cos_ref, sin_ref):
    ang = pos_ref[...].astype(F32) * invf_ref[...]
    cos_ref[...] = jnp.cos(ang)
    sin_ref[...] = jnp.sin(ang)


def _rope_tables(positions, tm):
    t = positions.size
    inv_freq = 1.0 / (ROPE_THETA ** (jnp.arange(0, MLA_ROPE, 2, dtype=F32) / MLA_ROPE))
    invf = jnp.tile(inv_freq, LANES // (MLA_ROPE // 2)).reshape(1, LANES)
    out = jax.ShapeDtypeStruct((t, LANES), F32)
    return pl.pallas_call(
        _rope_kernel,
        grid=(t // tm,),
        in_specs=[pl.BlockSpec((tm, 1), lambda i: (i, 0)),
                  pl.BlockSpec((1, LANES), lambda i: (0, 0))],
        out_specs=[pl.BlockSpec((tm, LANES), lambda i: (i, 0))] * 2,
        out_shape=[out, out],
        compiler_params=_params("parallel"),
        name="rope_tables",
    )(positions.reshape(t, 1), invf)


def _mla_proj_kernel(x_ref, cos_ref, sin_ref, w_in_ref, gq_ref, gkv_ref, wq1_ref, wq2_ref,
                     wk_ref, wvt_ref, q_ref, k_ref, vt_ref):
    xb = x_ref[...].astype(BF16)
    c = _dot(xb, w_in_ref[...])
    cq = _rms_norm(c[:, :MLA_Q_RANK], gq_ref[...]).astype(BF16)
    ckv = _rms_norm(c[:, MLA_Q_RANK:MLA_Q_RANK + MLA_KV_RANK], gkv_ref[...]).astype(BF16)
    cos = cos_ref[...]
    sin = sin_ref[...]
    base = MLA_Q_RANK + MLA_KV_RANK
    k_pe = (c[:, base:base + LANES] * cos + c[:, base + LANES:base + 2 * LANES] * sin).astype(BF16)
    scale = (MLA_NOPE + MLA_ROPE) ** -0.5 * LOG2_E
    cos_s = cos * scale
    sin_s = sin * scale
    q1 = _dot(cq, wq1_ref[...])
    q2 = _dot(cq, wq2_ref[...])
    kn = _dot(ckv, wk_ref[...])
    for h in range(MLA_HEADS):
        s0 = h * HEAD_SLOT
        q_ref[:, s0:s0 + LANES] = (q1[:, s0:s0 + LANES] * scale).astype(BF16)
        q_ref[:, s0 + LANES:s0 + HEAD_SLOT] = (
            q1[:, s0 + LANES:s0 + HEAD_SLOT] * cos_s
            + q2[:, h * LANES:(h + 1) * LANES] * sin_s).astype(BF16)
        k_ref[:, s0:s0 + LANES] = kn[:, h * LANES:(h + 1) * LANES].astype(BF16)
        k_ref[:, s0 + LANES:s0 + HEAD_SLOT] = k_pe
    vt_ref[...] = _dot_nt(wvt_ref[...], ckv).astype(BF16)


def _mla_proj(x, cos, sin, w, tm):
    t = x.shape[0]
    row = lambda n: pl.BlockSpec((tm, n), lambda i: (i, 0))
    full = lambda a: pl.BlockSpec(a.shape, lambda i: (0,) * a.ndim)
    consts = (w["w_in"], w["g_q"], w["g_kv"], w["w_q1"], w["w_q2"], w["w_k"], w["w_vt"])
    return pl.pallas_call(
        _mla_proj_kernel,
        grid=(t // tm,),
        in_specs=[row(D_MODEL), row(LANES), row(LANES)] + [full(a) for a in consts],
        out_specs=[row(MLA_HEADS * HEAD_SLOT), row(MLA_HEADS * HEAD_SLOT),
                   pl.BlockSpec((None, MLA_HEADS * MLA_V, tm), lambda i: (i, 0, 0))],
        out_shape=[jax.ShapeDtypeStruct((t, MLA_HEADS * HEAD_SLOT), BF16),
                   jax.ShapeDtypeStruct((t, MLA_HEADS * HEAD_SLOT), BF16),
                   jax.ShapeDtypeStruct((t // tm, MLA_HEADS * MLA_V, tm), BF16)],
        compiler_params=_params("parallel"),
        name="mla_proj",
    )(x, cos, sin, *consts)


def _attn_kernel(q_ref, k_ref, vt_ref, o_ref, s_ref, m_ref, l_ref, acc_ref, *, tile):
    qi = pl.program_id(2)
    m_ref[...] = jnp.full(m_ref.shape, -jnp.inf, F32)
    l_ref[...] = jnp.zeros(l_ref.shape, F32)
    acc_ref[...] = jnp.zeros(acc_ref.shape, F32)
    heads = range(ATTN_HEADS_PER_STEP)

    def scores(ki, buf):
        start = pl.multiple_of(ki * tile, tile)
        for g in heads:
            slot = slice(g * HEAD_SLOT, (g + 1) * HEAD_SLOT)
            s_ref[buf, g] = _dot_nt(k_ref[pl.ds(start, tile), slot], q_ref[:, slot])

    def absorb(ki, buf, masked):
        for g in heads:
            s = s_ref[buf, g]
            if masked:
                keys = lax.broadcasted_iota(jnp.int32, s.shape, 0)
                queries = lax.broadcasted_iota(jnp.int32, s.shape, 1)
                s = jnp.where(keys <= queries, s, -jnp.inf)
            m_old = m_ref[g]
            m_new = jnp.maximum(m_old, jnp.max(s, axis=0, keepdims=True))
            p = jnp.exp2(s - m_new)
            alpha = jnp.exp2(m_old - m_new)
            l_ref[g] = alpha * l_ref[g] + jnp.sum(p, axis=0, keepdims=True)
            acc_ref[g] = alpha * acc_ref[g] + _dot(vt_ref[ki, g * MLA_V:(g + 1) * MLA_V, :], p.astype(BF16))
            m_ref[g] = m_new

    scores(0, 0)

    def body(j, carry):
        scores(2 * j + 1, 1)
        absorb(2 * j, 0, False)
        scores(2 * j + 2, 0)
        absorb(2 * j + 1, 1, False)
        return carry

    lax.fori_loop(0, qi // 2, body, 0)

    @pl.when(qi % 2 == 1)
    def _():
        scores(qi, 1)
        absorb(qi - 1, 0, False)
        absorb(qi, 1, True)

    @pl.when(qi % 2 == 0)
    def _():
        absorb(qi, 0, True)

    for g in heads:
        o_ref[:, g * MLA_V:(g + 1) * MLA_V] = jnp.transpose(acc_ref[g] / l_ref[g]).astype(o_ref.dtype)


def _attention(q, k, vt, batch, seq, tile):
    t = q.shape[0]
    nq = seq // tile
    hps = ATTN_HEADS_PER_STEP
    return pl.pallas_call(
        functools.partial(_attn_kernel, tile=tile),
        grid=(batch, MLA_HEADS // hps, nq),
        in_specs=[pl.BlockSpec((tile, hps * HEAD_SLOT), lambda b, h, i: (b * nq + i, h)),
                  pl.BlockSpec((seq, hps * HEAD_SLOT), lambda b, h, i: (b, h)),
                  pl.BlockSpec((nq, hps * MLA_V, tile), lambda b, h, i: (b, h, 0))],
        out_specs=pl.BlockSpec((tile, hps * MLA_V), lambda b, h, i: (b * nq + i, h)),
        out_shape=jax.ShapeDtypeStruct((t, MLA_HEADS * MLA_V), BF16),
        scratch_shapes=[pltpu.VMEM((2, hps, tile, tile), F32),
                        pltpu.VMEM((hps, 1, tile), F32), pltpu.VMEM((hps, 1, tile), F32),
                        pltpu.VMEM((hps, MLA_V, tile), F32)],
        compiler_params=_params("parallel", "parallel", "arbitrary"),
        name="mla_attention",
    )(q, k, vt)


def _route_rows(sel):
    best = None
    for g in range(N_GROUPS):
        a = [sel[g * EXPERTS_PER_GROUP + j:g * EXPERTS_PER_GROUP + j + 1, :]
             for j in range(EXPERTS_PER_GROUP)]
        v1 = functools.reduce(jnp.maximum, a)
        i1 = jnp.full(v1.shape, EXPERTS_PER_GROUP - 1, jnp.int32)
        for j in reversed(range(EXPERTS_PER_GROUP - 1)):
            i1 = jnp.where(a[j] == v1, j, i1)
        r = [jnp.where(i1 == j, -jnp.inf, a[j]) for j in range(EXPERTS_PER_GROUP)]
        v2 = functools.reduce(jnp.maximum, r)
        i2 = jnp.full(v1.shape, EXPERTS_PER_GROUP - 1, jnp.int32)
        for j in reversed(range(EXPERTS_PER_GROUP - 1)):
            i2 = jnp.where(r[j] == v2, j, i2)
        total = v1 + v2
        if best is None:
            best = (total, jnp.zeros_like(i1), i1, i2)
        else:
            better = total > best[0]
            best = (jnp.where(better, total, best[0]), jnp.where(better, g, best[1]),
                    jnp.where(better, i1, best[2]), jnp.where(better, i2, best[3]))
    _, grp, i1, i2 = best
    lo = jnp.minimum(i1, i2)
    hi = jnp.maximum(i1, i2)
    pair = jnp.where(lo == 0, hi - 1, jnp.where(lo == 1, hi + 1, N_PAIRS - 1))
    return grp * N_PAIRS + pair


def _post_kernel(a_ref, w_ref, x_ref, g_ref, b_ref, rwt_ref, rb_ref, y_ref, cls_ref):
    y = _dot(a_ref[...], w_ref[...])
    x1 = _layer_norm(DN_ALPHA * x_ref[...] + y, g_ref[...], b_ref[...])
    y_ref[...] = x1
    logits = _dot_nt(rwt_ref[...], x1.astype(BF16))
    cls_ref[...] = _route_rows(_sigmoid(logits) + rb_ref[...])


def _post(a, w, x, g, b, rwt, rb, tm):
    t, k = a.shape
    row = lambda n: pl.BlockSpec((tm, n), lambda i: (i, 0))
    full = lambda arr: pl.BlockSpec(arr.shape, lambda i: (0,) * arr.ndim)
    return pl.pallas_call(
        _post_kernel,
        grid=(t // tm,),
        in_specs=[row(k), full(w), row(D_MODEL), full(g), full(b), full(rwt), full(rb)],
        out_specs=[row(D_MODEL), pl.BlockSpec((1, tm), lambda i: (0, i))],
        out_shape=[jax.ShapeDtypeStruct((t, D_MODEL), F32),
                   jax.ShapeDtypeStruct((1, t), jnp.int32)],
        compiler_params=_params("parallel"),
        name="mixer_out_norm_route",
    )(a, w, x, g, b, rwt, rb)


def _expert_kernel(ea_ref, eb_ref, nblk_ref, x_ref, rw_ref, wga_ref, wua_ref, wda_ref,
                   wgb_ref, wub_ref, wdb_ref, g_ref, b_ref, o_ref):
    i = pl.program_id(0)

    @pl.when(i < nblk_ref[0])
    def _():
        x = x_ref[...]
        xb = x.astype(BF16)
        sig = _sigmoid(_dot(xb, rw_ref[...]))
        lane = lax.broadcasted_iota(jnp.int32, sig.shape, 1)
        sa = jnp.sum(jnp.where(lane == ea_ref[i], sig, 0.0), axis=-1, keepdims=True)
        sb = jnp.sum(jnp.where(lane == eb_ref[i], sig, 0.0), axis=-1, keepdims=True)
        tot = sa + sb
        ha = _silu(_dot(xb, wga_ref[...])) * _dot(xb, wua_ref[...])
        hb = _silu(_dot(xb, wgb_ref[...])) * _dot(xb, wub_ref[...])
        y = (sa / tot) * _dot(ha.astype(BF16), wda_ref[...]) \
            + (sb / tot) * _dot(hb.astype(BF16), wdb_ref[...])
        o_ref[...] = _layer_norm(DN_ALPHA * x + y, g_ref[...], b_ref[...])

    @pl.when(i >= nblk_ref[0])
    def _():
        o_ref[...] = jnp.zeros(o_ref.shape, o_ref.dtype)


def _experts(xs, ea, eb, nblk, rw, wg, wu, wd, g, b, bm):
    p = xs.shape[0]
    sel = lambda which: (lambda i, ea, eb, nb: ((ea if which == 0 else eb)[i], 0, 0))
    up = lambda which: pl.BlockSpec((None, D_MODEL, EXPERT_FF), sel(which))
    down = lambda which: pl.BlockSpec((None, EXPERT_FF, D_MODEL), sel(which))
    full = lambda arr: pl.BlockSpec(arr.shape, lambda i, ea, eb, nb: (0,) * arr.ndim)
    row = pl.BlockSpec((bm, D_MODEL), lambda i, ea, eb, nb: (i, 0))
    return pl.pallas_call(
        _expert_kernel,
        grid_spec=pltpu.PrefetchScalarGridSpec(
            num_scalar_prefetch=3,
            grid=(p // bm,),
            in_specs=[row, full(rw), up(0), up(0), down(0), up(1), up(1), down(1), full(g), full(b)],
            out_specs=row),
        out_shape=jax.ShapeDtypeStruct((p, D_MODEL), F32),
        compiler_params=_params("arbitrary"),
        name="moe_experts",
    )(ea, eb, nblk, xs, rw, wg, wu, wd, wg, wu, wd, g, b)


def _row_gather(src, idx):
    return jnp.take(src, idx, axis=0)


def _moe(x1, cls, rw, wg, wu, wd, g, b, bm):
    t = x1.shape[0]
    p_rows = t + N_CLASSES * bm
    nblk = p_rows // bm
    classes = jnp.arange(N_CLASSES, dtype=jnp.int32)
    counts = jnp.sum((cls[:, None] == classes[None, :]).astype(jnp.int32), axis=0)
    padded = ((counts + bm - 1) // bm) * bm
    off = jnp.cumsum(counts) - counts
    pend = jnp.cumsum(padded)
    poff = pend - padded
    tok = jnp.arange(t, dtype=jnp.int32)
    cls_sorted, order = lax.sort((cls, tok), num_keys=1)
    shift = jnp.sum(jnp.where(cls_sorted[:, None] == classes[None, :], (poff - off)[None, :], 0), axis=1)
    _, dest = lax.sort((order, shift + tok), num_keys=1)
    blk_start = jnp.arange(nblk, dtype=jnp.int32) * bm
    blk_cls = jnp.minimum(jnp.sum((blk_start[:, None] >= pend[None, :]).astype(jnp.int32), axis=1),
                          N_CLASSES - 1)
    n_used = (pend[-1] // bm).astype(jnp.int32).reshape(1)
    j = jnp.arange(bm, dtype=jnp.int32)[None, :] + (blk_start - poff[blk_cls])[:, None]
    rank = off[blk_cls][:, None] + jnp.minimum(j, counts[blk_cls][:, None] - 1)
    src = order[jnp.clip(rank, 0, t - 1).reshape(p_rows)]
    grp = blk_cls // N_PAIRS
    pair = blk_cls % N_PAIRS
    lo = jnp.where(pair < 3, 0, jnp.where(pair < 5, 1, 2))
    hi = jnp.where(pair < 3, pair + 1, jnp.where(pair < 5, pair - 1, 3))
    ea = (grp * EXPERTS_PER_GROUP + lo).astype(jnp.int32)
    eb = (grp * EXPERTS_PER_GROUP + hi).astype(jnp.int32)
    xs = _row_gather(x1, src)
    ys = _experts(xs, ea, eb, n_used, rw, wg, wu, wd, g, b, bm)
    return _row_gather(ys, dest)


def _mlstm_pre_kernel(x_ref, w_in_ref, cw_ref, cb_ref, bdqk_ref, bdv_ref, wgate_ref, bgate_ref,
                      q_ref, k_ref, v_ref, xc_ref, sz_ref, gate_ref, ext_ref, *, tm):
    halo = 8
    xb = x_ref[...].astype(BF16)

    @pl.when(pl.program_id(1) == 0)
    def _():
        ext_ref[0:halo, :] = jnp.zeros((halo, MLSTM_INNER), F32)

    @pl.when(pl.program_id(1) != 0)
    def _():
        ext_ref[0:halo, :] = ext_ref[tm:tm + halo, :]

    xm = _dot(xb, w_in_ref[:, :MLSTM_INNER])
    ext_ref[halo:halo + tm, :] = xm
    conv = cb_ref[...] + cw_ref[MLSTM_CONV - 1:MLSTM_CONV, :] * xm
    for kk in range(MLSTM_CONV - 1):
        lag = MLSTM_CONV - 1 - kk
        conv = conv + cw_ref[kk:kk + 1, :] * ext_ref[halo - lag:halo - lag + tm, :]
    xc = _silu(conv)
    xc_b = xc.astype(BF16)
    xm_b = xm.astype(BF16)
    xc_ref[...] = xc_b
    sz_ref[...] = _silu(_dot(xb, w_in_ref[:, MLSTM_INNER:])).astype(BF16)

    gates = bgate_ref[...]
    k_scale = MLSTM_HD ** -0.5
    for c in range(MLSTM_INNER // MXU_DIM):
        lo, hi = c * MXU_DIM, (c + 1) * MXU_DIM
        qk = _dot(xc_b[:, lo:hi], bdqk_ref[c])
        vv = _dot(xm_b[:, lo:hi], bdv_ref[c])
        qb = qk[:, :MXU_DIM].astype(BF16)
        kb = qk[:, MXU_DIM:].astype(BF16)
        vb = vv.astype(BF16)
        q_ref[:, lo:hi] = qb
        k_ref[:, lo:hi] = (qk[:, MXU_DIM:] * k_scale).astype(BF16)
        v_ref[:, lo:hi] = vb
        gates = gates + _dot(qb, wgate_ref[lo:hi, :]) \
            + _dot(kb, wgate_ref[MLSTM_INNER + lo:MLSTM_INNER + hi, :]) \
            + _dot(vb, wgate_ref[2 * MLSTM_INNER + lo:2 * MLSTM_INNER + hi, :])
    fg = gates[:, LANES:]
    gate_ref[:, :LANES] = gates[:, :LANES]
    gate_ref[:, LANES:] = jnp.minimum(fg, 0.0) - jnp.log(1.0 + jnp.exp(-jnp.abs(fg)))


def _mlstm_pre(x, w, batch, seq, tm):
    t = x.shape[0]
    ns = seq // tm
    row = lambda n: pl.BlockSpec((tm, n), lambda b, s: (b * ns + s, 0))
    full = lambda a: pl.BlockSpec(a.shape, lambda b, s: (0,) * a.ndim)
    consts = (w["w_in"], w["conv_w"], w["conv_b"], w["bd_qk"], w["bd_v"], w["w_gate"], w["b_gate"])
    act = jax.ShapeDtypeStruct((t, MLSTM_INNER), BF16)
    return pl.pallas_call(
        functools.partial(_mlstm_pre_kernel, tm=tm),
        grid=(batch, ns),
        in_specs=[row(D_MODEL)] + [full(a) for a in consts],
        out_specs=[row(MLSTM_INNER)] * 5 + [row(2 * LANES)],
        out_shape=[act] * 5 + [jax.ShapeDtypeStruct((t, 2 * LANES), F32)],
        scratch_shapes=[pltpu.VMEM((tm + 8, MLSTM_INNER), F32)],
        compiler_params=_params("parallel", "arbitrary"),
        name="mlstm_pre",
    )(x, *consts)


def _split3(v):
    a = v.astype(BF16)
    r = v - a.astype(F32)
    b = r.astype(BF16)
    c = (r - b.astype(F32)).astype(BF16)
    return a, b, c


def _mlstm_chunk_kernel(q_ref, k_ref, v_ref, gate_ref, xc_ref, sz_ref, gout_ref, skip_ref, o_ref,
                        c_ref, n_ref, m_ref, *, chunk):
    @pl.when(pl.program_id(1) == 0)
    def _():
        c_ref[...] = jnp.zeros(c_ref.shape, F32)
        n_ref[...] = jnp.zeros(n_ref.shape, F32)
        m_ref[...] = jnp.zeros(m_ref.shape, F32)

    ig_all = gate_ref[:, :LANES]
    lf_all = gate_ref[:, LANES:]
    rows = lax.broadcasted_iota(jnp.int32, (chunk, chunk), 0)
    cols = lax.broadcasted_iota(jnp.int32, (chunk, chunk), 1)
    causal = cols <= rows
    tril = jnp.where(causal, 1.0, 0.0).astype(BF16)
    b_all = functools.reduce(lambda u, w: u + w, [_dot(tril, piece) for piece in _split3(lf_all)])
    row_all = jnp.transpose(ig_all - b_all)

    for h in range(MLSTM_HEADS):
        lo, hi = h * MLSTM_HD, (h + 1) * MLSTM_HD
        q = q_ref[:, lo:hi]
        k = k_ref[:, lo:hi]
        v = v_ref[:, lo:hi]
        b = b_all[:, h:h + 1]
        ig = ig_all[:, h:h + 1]
        m_prev = m_ref[h]
        d = jnp.where(causal, b + row_all[h:h + 1, :], -jnp.inf)
        inter = b + m_prev
        m_i = jnp.maximum(inter, jnp.max(d, axis=-1, keepdims=True))
        w_intra = jnp.exp(d - m_i)
        w_inter = jnp.exp(inter - m_i)
        s = _dot_nt(q, k) * w_intra
        c_old = c_ref[h]
        n_old = n_ref[h]
        num = _dot(s.astype(BF16), v) + w_inter * _dot(q, c_old.astype(BF16))
        qn = jnp.sum(q.astype(F32) * n_old, axis=-1, keepdims=True)
        den = jnp.sum(s, axis=-1, keepdims=True) + w_inter * qn
        hh = num / jnp.maximum(jnp.abs(den), jnp.exp(-m_i))

        b_last = b[chunk - 1:chunk, :]
        gg = b_last - b + ig
        m_new = jnp.maximum(b_last + m_prev, jnp.max(gg, axis=0, keepdims=True))
        kw = k.astype(F32) * jnp.exp(gg - m_new)
        decay = jnp.exp(b_last + m_prev - m_new)
        c_ref[h] = decay * c_old + _dot_tn(kw.astype(BF16), v)
        n_ref[h] = decay * n_old + jnp.sum(kw, axis=0, keepdims=True)
        m_ref[h] = m_new

        mu = jnp.mean(hh, axis=-1, keepdims=True)
        cen = hh - mu
        var = jnp.mean(cen * cen, axis=-1, keepdims=True)
        hn = cen * lax.rsqrt(var + LN_EPS) * gout_ref[:, lo:hi]
        o_ref[:, lo:hi] = ((hn + skip_ref[:, lo:hi] * xc_ref[:, lo:hi].astype(F32))
                           * sz_ref[:, lo:hi].astype(F32)).astype(o_ref.dtype)


def _mlstm_chunks(q, k, v, gates, xc, sz, g_out, skip, batch, seq, chunk):
    t = q.shape[0]
    nc = seq // chunk
    row = lambda n: pl.BlockSpec((chunk, n), lambda b, c: (b * nc + c, 0))
    full = lambda a: pl.BlockSpec(a.shape, lambda b, c: (0,) * a.ndim)
    return pl.pallas_call(
        functools.partial(_mlstm_chunk_kernel, chunk=chunk),
        grid=(batch, nc),
        in_specs=[row(MLSTM_INNER)] * 3 + [row(2 * LANES)] + [row(MLSTM_INNER)] * 2
                 + [full(g_out), full(skip)],
        out_specs=row(MLSTM_INNER),
        out_shape=jax.ShapeDtypeStruct((t, MLSTM_INNER), BF16),
        scratch_shapes=[pltpu.VMEM((MLSTM_HEADS, MLSTM_HD, MLSTM_HD), F32),
                        pltpu.VMEM((MLSTM_HEADS, 1, MLSTM_HD), F32),
                        pltpu.VMEM((MLSTM_HEADS, 1, 1), F32)],
        compiler_params=_params("parallel", "arbitrary"),
        name="mlstm_chunks",
    )(q, k, v, gates, xc, sz, g_out, skip)


def _rot_half(w):
    half = MLA_ROPE // 2
    return jnp.concatenate([-w[..., half:], w[..., :half]], axis=-1)


def _prep_mla(w_in, g_q, w_uq, g_kv, w_ukv):
    pad = jnp.zeros((D_MODEL, LANES - MLA_ROPE), F32)
    w_pe = w_in[:, MLA_Q_RANK + MLA_KV_RANK:]
    w_in_p = jnp.concatenate([w_in[:, :MLA_Q_RANK + MLA_KV_RANK], w_pe, pad, _rot_half(w_pe), pad], axis=1)
    uq = w_uq.reshape(MLA_Q_RANK, MLA_HEADS, MLA_NOPE + MLA_ROPE)
    zq = jnp.zeros((MLA_Q_RANK, MLA_HEADS, LANES - MLA_ROPE), F32)
    w_q1 = jnp.concatenate([uq, zq], axis=-1).reshape(MLA_Q_RANK, MLA_HEADS * HEAD_SLOT)
    w_q2 = jnp.concatenate([_rot_half(uq[..., MLA_NOPE:]), zq], axis=-1).reshape(MLA_Q_RANK, MLA_HEADS * LANES)
    ukv = w_ukv.reshape(MLA_KV_RANK, MLA_HEADS, MLA_NOPE + MLA_V)
    w_k = ukv[..., :MLA_NOPE].reshape(MLA_KV_RANK, -1)
    w_vt = ukv[..., MLA_NOPE:].reshape(MLA_KV_RANK, -1).T
    return dict(w_in=w_in_p.astype(BF16), g_q=g_q.reshape(1, -1), g_kv=g_kv.reshape(1, -1),
                w_q1=w_q1.astype(BF16), w_q2=w_q2.astype(BF16), w_k=w_k.astype(BF16),
                w_vt=w_vt.astype(BF16))


def _block_diag_tiles(w):
    per = MXU_DIM // MLSTM_QKV_BLOCK
    w = w.reshape(MLSTM_INNER // MXU_DIM, per, MLSTM_QKV_BLOCK, MLSTM_QKV_BLOCK)
    eye = jnp.eye(per, dtype=w.dtype)
    return jnp.einsum("cgij,gh->cgihj", w, eye).reshape(MLSTM_INNER // MXU_DIM, MXU_DIM, MXU_DIM)


def _prep_mlstm(w_in, conv_w, conv_b, w_q, w_k, w_v, w_ig, b_ig, w_fg, b_fg):
    bd_qk = jnp.concatenate([_block_diag_tiles(w_q), _block_diag_tiles(w_k)], axis=-1)
    zw = jnp.zeros((3 * MLSTM_INNER, LANES - MLSTM_HEADS), F32)
    zb = jnp.zeros((LANES - MLSTM_HEADS,), F32)
    w_gate = jnp.concatenate([w_ig, zw, w_fg, zw], axis=1)
    b_gate = jnp.concatenate([b_ig, zb, b_fg, zb]).reshape(1, 2 * LANES)
    return dict(w_in=w_in.astype(BF16), conv_w=conv_w, conv_b=conv_b.reshape(1, -1),
                bd_qk=bd_qk.astype(BF16), bd_v=_block_diag_tiles(w_v).astype(BF16),
                w_gate=w_gate.astype(BF16), b_gate=b_gate)


def kernel(x, positions, router_w, router_b, mla_w_in, mla_g_q, mla_w_uq, mla_g_kv, mla_w_ukv, mla_w_o, mlstm_w_in, mlstm_conv_w, mlstm_conv_b, mlstm_w_q, mlstm_w_k, mlstm_w_v, mlstm_w_ig, mlstm_b_ig, mlstm_w_fg, mlstm_b_fg, mlstm_g_out, mlstm_skip, mlstm_w_out, moe_w_gate, moe_w_up, moe_w_down, ln_g, ln_b):
    batch, seq, _ = x.shape
    t = batch * seq
    tiles = _tiles(seq)
    cos, sin = _rope_tables(positions, tiles["rope"])
    rwt = router_w.T.astype(BF16)
    rb = router_b.reshape(N_EXPERTS, 1)
    rw_pad = jnp.concatenate([router_w, jnp.zeros((D_MODEL, LANES - N_EXPERTS), F32)], axis=1).astype(BF16)
    xt = x.reshape(t, D_MODEL)
    for i in range(DEPTH):
        j = i // N_MIXERS
        g0, b0 = ln_g[i, 0].reshape(1, -1), ln_b[i, 0].reshape(1, -1)
        g1, b1 = ln_g[i, 1].reshape(1, -1), ln_b[i, 1].reshape(1, -1)
        if i % N_MIXERS == 0:
            w = _prep_mla(mla_w_in[j], mla_g_q[j], mla_w_uq[j], mla_g_kv[j], mla_w_ukv[j])
            q, k, vt = _mla_proj(xt, cos, sin, w, tiles["attn"])
            a = _attention(q, k, vt, batch, seq, tiles["attn"])
            w_out = mla_w_o[j].astype(BF16)
        else:
            w = _prep_mlstm(mlstm_w_in[j], mlstm_conv_w[j], mlstm_conv_b[j], mlstm_w_q[j], mlstm_w_k[j],
                            mlstm_w_v[j], mlstm_w_ig[j], mlstm_b_ig[j], mlstm_w_fg[j], mlstm_b_fg[j])
            q, k, v, xc, sz, gates = _mlstm_pre(xt, w, batch, seq, tiles["pre"])
            a = _mlstm_chunks(q, k, v, gates, xc, sz, mlstm_g_out[j].reshape(1, -1),
                              mlstm_skip[j].reshape(1, -1), batch, seq, tiles["chunk"])
            w_out = mlstm_w_out[j].astype(BF16)
        x1, cls = _post(a, w_out, xt, g0, b0, rwt, rb, tiles["post"])
        xt = _moe(x1, cls.reshape(t), rw_pad, moe_w_gate[i].astype(BF16), moe_w_up[i].astype(BF16),
                  moe_w_down[i].astype(BF16), g1, b1, tiles["moe"])
    return xt.reshape(batch, seq, D_MODEL)
```

```python
import functools

import jax
import jax.numpy as jnp
from jax import lax
from jax.experimental import pallas as pl
from jax.experimental.pallas import tpu as pltpu
from jax.experimental.pallas import tpu_sc as plsc

D_MODEL = 1024
DEPTH = 4
N_MIXERS = 2
MLA_HEADS = 8
MLA_Q_RANK = 256
MLA_KV_RANK = 256
MLA_NOPE = 128
MLA_ROPE = 64
MLA_V = 128
ROPE_THETA = 10000.0
MLSTM_INNER = 2 * D_MODEL
MLSTM_HEADS = 4
MLSTM_HD = MLSTM_INNER // MLSTM_HEADS
MLSTM_CONV = 4
MLSTM_QKV_BLOCK = 4
N_EXPERTS = 16
N_GROUPS = 4
EXPERTS_PER_GROUP = N_EXPERTS // N_GROUPS
EXPERT_FF = 512
DN_ALPHA = (2.0 * DEPTH) ** 0.25
LN_EPS = 1e-5
LOG2_E = 1.4426950408889634

LANES = 128
MXU_DIM = 256
HEAD_SLOT = 2 * LANES
ATTN_HEADS_PER_STEP = 4
N_PAIRS = EXPERTS_PER_GROUP * (EXPERTS_PER_GROUP - 1) // 2
N_CLASSES = N_GROUPS * N_PAIRS
SC_WORKERS = 32
SC_ROWS = 32
VMEM_LIMIT = 56 * 1024 * 1024

F32 = jnp.float32
BF16 = jnp.bfloat16


def _tiles(seq):
    return dict(
        rope=min(seq, 2048),
        attn=min(seq, 512),
        post=min(seq, 512),
        pre=min(seq, 256),
        chunk=min(seq, 256),
        moe=256,
    )


def _params(*sem):
    return pltpu.CompilerParams(dimension_semantics=sem, vmem_limit_bytes=VMEM_LIMIT)


def _dot(a, b):
    return jnp.dot(a, b, preferred_element_type=F32)


def _dot_nt(a, b):
    return lax.dot_general(a, b, (((1,), (1,)), ((), ())), preferred_element_type=F32)


def _dot_tn(a, b):
    return lax.dot_general(a, b, (((0,), (0,)), ((), ())), preferred_element_type=F32)


def _layer_norm(v, g, b):
    mu = jnp.mean(v, axis=-1, keepdims=True)
    c = v - mu
    var = jnp.mean(c * c, axis=-1, keepdims=True)
    return c * lax.rsqrt(var + LN_EPS) * g + b


def _rms_norm(v, g):
    return v * lax.rsqrt(jnp.mean(v * v, axis=-1, keepdims=True) + LN_EPS) * g


def _sigmoid(v):
    return 1.0 / (1.0 + jnp.exp(-v))


def _silu(v):
    return v * _sigmoid(v)


def _rope_kernel(pos_ref, invf_ref, cos_ref, sin_ref):
    ang = pos_ref[...].astype(F32) * invf_ref[...]
    cos_ref[...] = jnp.cos(ang)
    sin_ref[...] = jnp.sin(ang)


def _rope_tables(positions, tm):
    t = positions.size
    inv_freq = 1.0 / (ROPE_THETA ** (jnp.arange(0, MLA_ROPE, 2, dtype=F32) / MLA_ROPE))
    invf = jnp.tile(inv_freq, LANES // (MLA_ROPE // 2)).reshape(1, LANES)
    out = jax.ShapeDtypeStruct((t, LANES), F32)
    return pl.pallas_call(
        _rope_kernel,
        grid=(t // tm,),
        in_specs=[pl.BlockSpec((tm, 1), lambda i: (i, 0)),
                  pl.BlockSpec((1, LANES), lambda i: (0, 0))],
        out_specs=[pl.BlockSpec((tm, LANES), lambda i: (i, 0))] * 2,
        out_shape=[out, out],
        compiler_params=_params("parallel"),
        name="rope_tables",
    )(positions.reshape(t, 1), invf)


def _mla_proj_kernel(x_ref, cos_ref, sin_ref, w_in_ref, gq_ref, gkv_ref, wq1_ref, wq2_ref,
                     wk_ref, wvt_ref, q_ref, k_ref, vt_ref):
    xb = x_ref[...].astype(BF16)
    c = _dot(xb, w_in_ref[...])
    cq = _rms_norm(c[:, :MLA_Q_RANK], gq_ref[...]).astype(BF16)
    ckv = _rms_norm(c[:, MLA_Q_RANK:MLA_Q_RANK + MLA_KV_RANK], gkv_ref[...]).astype(BF16)
    cos = cos_ref[...]
    sin = sin_ref[...]
    base = MLA_Q_RANK + MLA_KV_RANK
    k_pe = (c[:, base:base + LANES] * cos + c[:, base + LANES:base + 2 * LANES] * sin).astype(BF16)
    scale = (MLA_NOPE + MLA_ROPE) ** -0.5 * LOG2_E
    cos_s = cos * scale
    sin_s = sin * scale
    q1 = _dot(cq, wq1_ref[...])
    q2 = _dot(cq, wq2_ref[...])
    kn = _dot(ckv, wk_ref[...])
    for h in range(MLA_HEADS):
        s0 = h * HEAD_SLOT
        q_ref[:, s0:s0 + LANES] = (q1[:, s0:s0 + LANES] * scale).astype(BF16)
        q_ref[:, s0 + LANES:s0 + HEAD_SLOT] = (
            q1[:, s0 + LANES:s0 + HEAD_SLOT] * cos_s
            + q2[:, h * LANES:(h + 1) * LANES] * sin_s).astype(BF16)
        k_ref[:, s0:s0 + LANES] = kn[:, h * LANES:(h + 1) * LANES].astype(BF16)
        k_ref[:, s0 + LANES:s0 + HEAD_SLOT] = k_pe
    vt_ref[...] = _dot_nt(wvt_ref[...], ckv).astype(BF16)


def _mla_proj(x, cos, sin, w, tm):
    t = x.shape[0]
    row = lambda n: pl.BlockSpec((tm, n), lambda i: (i, 0))
    full = lambda a: pl.BlockSpec(a.shape, lambda i: (0,) * a.ndim)
    consts = (w["w_in"], w["g_q"], w["g_kv"], w["w_q1"], w["w_q2"], w["w_k"], w["w_vt"])
    return pl.pallas_call(
        _mla_proj_kernel,
        grid=(t // tm,),
        in_specs=[row(D_MODEL), row(LANES), row(LANES)] + [full(a) for a in consts],
        out_specs=[row(MLA_HEADS * HEAD_SLOT), row(MLA_HEADS * HEAD_SLOT),
                   pl.BlockSpec((None, MLA_HEADS * MLA_V, tm), lambda i: (i, 0, 0))],
        out_shape=[jax.ShapeDtypeStruct((t, MLA_HEADS * HEAD_SLOT), BF16),
                   jax.ShapeDtypeStruct((t, MLA_HEADS * HEAD_SLOT), BF16),
                   jax.ShapeDtypeStruct((t // tm, MLA_HEADS * MLA_V, tm), BF16)],
        compiler_params=_params("parallel"),
        name="mla_proj",
    )(x, cos, sin, *consts)


def _attn_kernel(q_ref, k_ref, vt_ref, o_ref, s_ref, m_ref, l_ref, acc_ref, *, tile):
    qi = pl.program_id(2)
    m_ref[...] = jnp.full(m_ref.shape, -jnp.inf, F32)
    l_ref[...] = jnp.zeros(l_ref.shape, F32)
    acc_ref[...] = jnp.zeros(acc_ref.shape, F32)
    heads = range(ATTN_HEADS_PER_STEP)

    def scores(ki, buf):
        start = pl.multiple_of(ki * tile, tile)
        for g in heads:
            slot = slice(g * HEAD_SLOT, (g + 1) * HEAD_SLOT)
            s_ref[buf, g] = _dot_nt(k_ref[pl.ds(start, tile), slot], q_ref[:, slot])

    def absorb(ki, buf, masked):
        for g in heads:
            s = s_ref[buf, g]
            if masked:
                keys = lax.broadcasted_iota(jnp.int32, s.shape, 0)
                queries = lax.broadcasted_iota(jnp.int32, s.shape, 1)
                s = jnp.where(keys <= queries, s, -jnp.inf)
            m_old = m_ref[g]
            m_new = jnp.maximum(m_old, jnp.max(s, axis=0, keepdims=True))
            p = jnp.exp2(s - m_new)
            alpha = jnp.exp2(m_old - m_new)
            l_ref[g] = alpha * l_ref[g] + jnp.sum(p, axis=0, keepdims=True)
            acc_ref[g] = alpha * acc_ref[g] + _dot(vt_ref[ki, g * MLA_V:(g + 1) * MLA_V, :], p.astype(BF16))
            m_ref[g] = m_new

    scores(0, 0)

    def body(j, carry):
        scores(2 * j + 1, 1)
        absorb(2 * j, 0, False)
        scores(2 * j + 2, 0)
        absorb(2 * j + 1, 1, False)
        return carry

    lax.fori_loop(0, qi // 2, body, 0)

    @pl.when(qi % 2 == 1)
    def _():
        scores(qi, 1)
        absorb(qi - 1, 0, False)
        absorb(qi, 1, True)

    @pl.when(qi % 2 == 0)
    def _():
        absorb(qi, 0, True)

    for g in heads:
        o_ref[:, g * MLA_V:(g + 1) * MLA_V] = jnp.transpose(acc_ref[g] / l_ref[g]).astype(o_ref.dtype)


def _attention(q, k, vt, batch, seq, tile):
    t = q.shape[0]
    nq = seq // tile
    hps = ATTN_HEADS_PER_STEP
    return pl.pallas_call(
        functools.partial(_attn_kernel, tile=tile),
        grid=(batch, MLA_HEADS // hps, nq),
        in_specs=[pl.BlockSpec((tile, hps * HEAD_SLOT), lambda b, h, i: (b * nq + i, h)),
                  pl.BlockSpec((seq, hps * HEAD_SLOT), lambda b, h, i: (b, h)),
                  pl.BlockSpec((nq, hps * MLA_V, tile), lambda b, h, i: (b, h, 0))],
        out_specs=pl.BlockSpec((tile, hps * MLA_V), lambda b, h, i: (b * nq + i, h)),
        out_shape=jax.ShapeDtypeStruct((t, MLA_HEADS * MLA_V), BF16),
        scratch_shapes=[pltpu.VMEM((2, hps, tile, tile), F32),
                        pltpu.VMEM((hps, 1, tile), F32), pltpu.VMEM((hps, 1, tile), F32),
                        pltpu.VMEM((hps, MLA_V, tile), F32)],
        compiler_params=_params("parallel", "parallel", "arbitrary"),
        name="mla_attention",
    )(q, k, vt)


def _route_rows(sel):
    best = None
    for g in range(N_GROUPS):
        a = [sel[g * EXPERTS_PER_GROUP + j:g * EXPERTS_PER_GROUP + j + 1, :]
             for j in range(EXPERTS_PER_GROUP)]
        v1 = functools.reduce(jnp.maximum, a)
        i1 = jnp.full(v1.shape, EXPERTS_PER_GROUP - 1, jnp.int32)
        for j in reversed(range(EXPERTS_PER_GROUP - 1)):
            i1 = jnp.where(a[j] == v1, j, i1)
        r = [jnp.where(i1 == j, -jnp.inf, a[j]) for j in range(EXPERTS_PER_GROUP)]
        v2 = functools.reduce(jnp.maximum, r)
        i2 = jnp.full(v1.shape, EXPERTS_PER_GROUP - 1, jnp.int32)
        for j in reversed(range(EXPERTS_PER_GROUP - 1)):
            i2 = jnp.where(r[j] == v2, j, i2)
        total = v1 + v2
        if best is None:
            best = (total, jnp.zeros_like(i1), i1, i2)
        else:
            better = total > best[0]
            best = (jnp.where(better, total, best[0]), jnp.where(better, g, best[1]),
                    jnp.where(better, i1, best[2]), jnp.where(better, i2, best[3]))
    _, grp, i1, i2 = best
    lo = jnp.minimum(i1, i2)
    hi = jnp.maximum(i1, i2)
    pair = jnp.where(lo == 0, hi - 1, jnp.where(lo == 1, hi + 1, N_PAIRS - 1))
    return grp * N_PAIRS + pair


def _post_kernel(a_ref, w_ref, x_ref, g_ref, b_ref, rwt_ref, rb_ref, y_ref, cls_ref):
    y = _dot(a_ref[...], w_ref[...])
    x1 = _layer_norm(DN_ALPHA * x_ref[...] + y, g_ref[...], b_ref[...])
    y_ref[...] = x1
    logits = _dot_nt(rwt_ref[...], x1.astype(BF16))
    cls_ref[...] = _route_rows(_sigmoid(logits) + rb_ref[...])


def _post(a, w, x, g, b, rwt, rb, tm):
    t, k = a.shape
    row = lambda n: pl.BlockSpec((tm, n), lambda i: (i, 0))
    full = lambda arr: pl.BlockSpec(arr.shape, lambda i: (0,) * arr.ndim)
    return pl.pallas_call(
        _post_kernel,
        grid=(t // tm,),
        in_specs=[row(k), full(w), row(D_MODEL), full(g), full(b), full(rwt), full(rb)],
        out_specs=[row(D_MODEL), pl.BlockSpec((1, tm), lambda i: (0, i))],
        out_shape=[jax.ShapeDtypeStruct((t, D_MODEL), F32),
                   jax.ShapeDtypeStruct((1, t), jnp.int32)],
        compiler_params=_params("parallel"),
        name="mixer_out_norm_route",
    )(a, w, x, g, b, rwt, rb)


def _expert_kernel(ea_ref, eb_ref, fresh_ref, nblk_ref, x_ref, rw_ref, wga_ref, wua_ref, wda_ref,
                   wgb_ref, wub_ref, wdb_ref, g_ref, b_ref, o_ref, wga, wua, wda, wgb, wub, wdb):
    i = pl.program_id(0)

    @pl.when(fresh_ref[i] == 1)
    def _():
        for src, dst in ((wga_ref, wga), (wua_ref, wua), (wda_ref, wda),
                         (wgb_ref, wgb), (wub_ref, wub), (wdb_ref, wdb)):
            dst[...] = src[...].astype(BF16)

    @pl.when(i < nblk_ref[0])
    def _():
        x = x_ref[...]
        xb = x.astype(BF16)
        sig = _sigmoid(_dot(xb, rw_ref[...]))
        lane = lax.broadcasted_iota(jnp.int32, sig.shape, 1)
        sa = jnp.sum(jnp.where(lane == ea_ref[i], sig, 0.0), axis=-1, keepdims=True)
        sb = jnp.sum(jnp.where(lane == eb_ref[i], sig, 0.0), axis=-1, keepdims=True)
        tot = sa + sb
        ha = _silu(_dot(xb, wga[...])) * _dot(xb, wua[...])
        hb = _silu(_dot(xb, wgb[...])) * _dot(xb, wub[...])
        y = (sa / tot) * _dot(ha.astype(BF16), wda[...]) + (sb / tot) * _dot(hb.astype(BF16), wdb[...])
        o_ref[...] = _layer_norm(DN_ALPHA * x + y, g_ref[...], b_ref[...])

    @pl.when(i >= nblk_ref[0])
    def _():
        o_ref[...] = jnp.zeros(o_ref.shape, o_ref.dtype)


def _experts(xs, ea, eb, fresh, nblk, rw, wg, wu, wd, g, b, bm):
    p = xs.shape[0]
    sel = lambda which: (lambda i, ea, eb, fr, nb: ((ea if which == 0 else eb)[i], 0, 0))
    up = lambda which: pl.BlockSpec((None, D_MODEL, EXPERT_FF), sel(which))
    down = lambda which: pl.BlockSpec((None, EXPERT_FF, D_MODEL), sel(which))
    full = lambda arr: pl.BlockSpec(arr.shape, lambda i, ea, eb, fr, nb: (0,) * arr.ndim)
    row = pl.BlockSpec((bm, D_MODEL), lambda i, ea, eb, fr, nb: (i, 0))
    up_s = pltpu.VMEM((D_MODEL, EXPERT_FF), BF16)
    down_s = pltpu.VMEM((EXPERT_FF, D_MODEL), BF16)
    return pl.pallas_call(
        _expert_kernel,
        grid_spec=pltpu.PrefetchScalarGridSpec(
            num_scalar_prefetch=4,
            grid=(p // bm,),
            in_specs=[row, full(rw), up(0), up(0), down(0), up(1), up(1), down(1), full(g), full(b)],
            out_specs=row,
            scratch_shapes=[up_s, up_s, down_s, up_s, up_s, down_s]),
        out_shape=jax.ShapeDtypeStruct((p, D_MODEL), F32),
        compiler_params=_params("arbitrary"),
        name="moe_experts",
    )(ea, eb, fresh, nblk, xs, rw, wg, wu, wd, wg, wu, wd, g, b)


def _row_gather(src, idx):
    m = idx.shape[0]
    d = src.shape[1]
    per = m // SC_WORKERS
    n = per // SC_ROWS
    assert per * SC_WORKERS == m and n * SC_ROWS == per and n % 2 == 0
    mesh = plsc.VectorSubcoreMesh(core_axis_name="core", subcore_axis_name="subcore")

    def body(src_hbm, idx_hbm, out_hbm, idx_v, buf, gsem, psem):
        worker = lax.axis_index("core") * (SC_WORKERS // 2) + lax.axis_index("subcore")
        base = worker * per
        pltpu.sync_copy(idx_hbm.at[pl.ds(base, per)], idx_v)

        def gather(c, slot):
            return pltpu.make_async_copy(src_hbm.at[idx_v.at[pl.ds(c * SC_ROWS, SC_ROWS)]],
                                         buf.at[slot], gsem.at[slot])

        def put(c, slot):
            return pltpu.make_async_copy(buf.at[slot], out_hbm.at[pl.ds(base + c * SC_ROWS, SC_ROWS)],
                                         psem.at[slot])

        gather(0, 0).start()

        @pl.loop(0, n // 2)
        def _(j):
            for slot in range(2):
                c = 2 * j + slot
                gather(c, slot).wait()
                put(c, slot).start()

                @pl.when(c >= 1)
                def _():
                    put(c - 1, 1 - slot).wait()

                @pl.when(c + 1 < n)
                def _():
                    gather(c + 1, 1 - slot).start()

        put(n - 1, 1).wait()

    return pl.kernel(
        body,
        out_type=jax.ShapeDtypeStruct((m, d), src.dtype),
        mesh=mesh,
        scratch_types=[pltpu.VMEM((per,), jnp.int32), pltpu.VMEM((2, SC_ROWS, d), src.dtype),
                       pltpu.SemaphoreType.DMA((2,)), pltpu.SemaphoreType.DMA((2,))],
    )(src, idx)


def _moe(x1, cls, rw, wg, wu, wd, g, b, bm):
    t = x1.shape[0]
    p_rows = t + N_CLASSES * bm
    nblk = p_rows // bm
    classes = jnp.arange(N_CLASSES, dtype=jnp.int32)
    counts = jnp.sum((cls[:, None] == classes[None, :]).astype(jnp.int32), axis=0)
    padded = ((counts + bm - 1) // bm) * bm
    off = jnp.cumsum(counts) - counts
    pend = jnp.cumsum(padded)
    poff = pend - padded
    tok = jnp.arange(t, dtype=jnp.int32)
    cls_sorted, order = lax.sort((cls, tok), num_keys=1)
    shift = jnp.sum(jnp.where(cls_sorted[:, None] == classes[None, :], (poff - off)[None, :], 0), axis=1)
    _, dest = lax.sort((order, shift + tok), num_keys=1)
    blk_start = jnp.arange(nblk, dtype=jnp.int32) * bm
    blk_cls = jnp.minimum(jnp.sum((blk_start[:, None] >= pend[None, :]).astype(jnp.int32), axis=1),
                          N_CLASSES - 1)
    n_used = (pend[-1] // bm).astype(jnp.int32).reshape(1)
    j = jnp.arange(bm, dtype=jnp.int32)[None, :] + (blk_start - poff[blk_cls])[:, None]
    rank = off[blk_cls][:, None] + jnp.minimum(j, counts[blk_cls][:, None] - 1)
    src = order[jnp.clip(rank, 0, t - 1).reshape(p_rows)]
    grp = blk_cls // N_PAIRS
    pair = blk_cls % N_PAIRS
    lo = jnp.where(pair < 3, 0, jnp.where(pair < 5, 1, 2))
    hi = jnp.where(pair < 3, pair + 1, jnp.where(pair < 5, pair - 1, 3))
    ea = (grp * EXPERTS_PER_GROUP + lo).astype(jnp.int32)
    eb = (grp * EXPERTS_PER_GROUP + hi).astype(jnp.int32)
    fresh = jnp.concatenate([jnp.ones((1,), jnp.int32), (blk_cls[1:] != blk_cls[:-1]).astype(jnp.int32)])
    xs = _row_gather(x1, src)
    ys = _experts(xs, ea, eb, fresh, n_used, rw, wg, wu, wd, g, b, bm)
    return _row_gather(ys, dest)


def _mlstm_pre_kernel(x_ref, w_in_ref, cw_ref, cb_ref, bdqk_ref, bdv_ref, wgate_ref, bgate_ref,
                      q_ref, k_ref, v_ref, xc_ref, sz_ref, gate_ref, ext_ref, *, tm):
    halo = 8
    xb = x_ref[...].astype(BF16)

    @pl.when(pl.program_id(1) == 0)
    def _():
        ext_ref[0:halo, :] = jnp.zeros((halo, MLSTM_INNER), F32)

    @pl.when(pl.program_id(1) != 0)
    def _():
        ext_ref[0:halo, :] = ext_ref[tm:tm + halo, :]

    xm = _dot(xb, w_in_ref[:, :MLSTM_INNER])
    ext_ref[halo:halo + tm, :] = xm
    conv = cb_ref[...] + cw_ref[MLSTM_CONV - 1:MLSTM_CONV, :] * xm
    for kk in range(MLSTM_CONV - 1):
        lag = MLSTM_CONV - 1 - kk
        conv = conv + cw_ref[kk:kk + 1, :] * ext_ref[halo - lag:halo - lag + tm, :]
    xc = _silu(conv)
    xc_b = xc.astype(BF16)
    xm_b = xm.astype(BF16)
    xc_ref[...] = xc_b
    sz_ref[...] = _silu(_dot(xb, w_in_ref[:, MLSTM_INNER:])).astype(BF16)

    gates = bgate_ref[...]
    k_scale = MLSTM_HD ** -0.5
    for c in range(MLSTM_INNER // MXU_DIM):
        lo, hi = c * MXU_DIM, (c + 1) * MXU_DIM
        qk = _dot(xc_b[:, lo:hi], bdqk_ref[c])
        vv = _dot(xm_b[:, lo:hi], bdv_ref[c])
        qb = qk[:, :MXU_DIM].astype(BF16)
        kb = qk[:, MXU_DIM:].astype(BF16)
        vb = vv.astype(BF16)
        q_ref[:, lo:hi] = qb
        k_ref[:, lo:hi] = (qk[:, MXU_DIM:] * k_scale).astype(BF16)
        v_ref[:, lo:hi] = vb
        gates = gates + _dot(qb, wgate_ref[lo:hi, :]) \
            + _dot(kb, wgate_ref[MLSTM_INNER + lo:MLSTM_INNER + hi, :]) \
            + _dot(vb, wgate_ref[2 * MLSTM_INNER + lo:2 * MLSTM_INNER + hi, :])
    fg = gates[:, LANES:]
    gate_ref[:, :LANES] = gates[:, :LANES]
    gate_ref[:, LANES:] = jnp.minimum(fg, 0.0) - jnp.log(1.0 + jnp.exp(-jnp.abs(fg)))


def _mlstm_pre(x, w, batch, seq, tm):
    t = x.shape[0]
    ns = seq // tm
    row = lambda n: pl.BlockSpec((tm, n), lambda b, s: (b * ns + s, 0))
    full = lambda a: pl.BlockSpec(a.shape, lambda b, s: (0,) * a.ndim)
    consts = (w["w_in"], w["conv_w"], w["conv_b"], w["bd_qk"], w["bd_v"], w["w_gate"], w["b_gate"])
    act = jax.ShapeDtypeStruct((t, MLSTM_INNER), BF16)
    return pl.pallas_call(
        functools.partial(_mlstm_pre_kernel, tm=tm),
        grid=(batch, ns),
        in_specs=[row(D_MODEL)] + [full(a) for a in consts],
        out_specs=[row(MLSTM_INNER)] * 5 + [row(2 * LANES)],
        out_shape=[act] * 5 + [jax.ShapeDtypeStruct((t, 2 * LANES), F32)],
        scratch_shapes=[pltpu.VMEM((tm + 8, MLSTM_INNER), F32)],
        compiler_params=_params("parallel", "arbitrary"),
        name="mlstm_pre",
    )(x, *consts)


def _split3(v):
    a = v.astype(BF16)
    r = v - a.astype(F32)
    b = r.astype(BF16)
    c = (r - b.astype(F32)).astype(BF16)
    return a, b, c


def _mlstm_chunk_kernel(q_ref, k_ref, v_ref, gate_ref, xc_ref, sz_ref, gout_ref, skip_ref, o_ref,
                        c_ref, n_ref, m_ref, *, chunk):
    @pl.when(pl.program_id(1) == 0)
    def _():
        c_ref[...] = jnp.zeros(c_ref.shape, F32)
        n_ref[...] = jnp.zeros(n_ref.shape, F32)
        m_ref[...] = jnp.zeros(m_ref.shape, F32)

    ig_all = gate_ref[:, :LANES]
    lf_all = gate_ref[:, LANES:]
    rows = lax.broadcasted_iota(jnp.int32, (chunk, chunk), 0)
    cols = lax.broadcasted_iota(jnp.int32, (chunk, chunk), 1)
    causal = cols <= rows
    tril = jnp.where(causal, 1.0, 0.0).astype(BF16)
    b_all = functools.reduce(lambda u, w: u + w, [_dot(tril, piece) for piece in _split3(lf_all)])
    row_all = jnp.transpose(ig_all - b_all)

    for h in range(MLSTM_HEADS):
        lo, hi = h * MLSTM_HD, (h + 1) * MLSTM_HD
        q = q_ref[:, lo:hi]
        k = k_ref[:, lo:hi]
        v = v_ref[:, lo:hi]
        b = b_all[:, h:h + 1]
        ig = ig_all[:, h:h + 1]
        m_prev = m_ref[h]
        d = jnp.where(causal, b + row_all[h:h + 1, :], -jnp.inf)
        inter = b + m_prev
        m_i = jnp.maximum(inter, jnp.max(d, axis=-1, keepdims=True))
        w_intra = jnp.exp(d - m_i)
        w_inter = jnp.exp(inter - m_i)
        s = _dot_nt(q, k) * w_intra
        c_old = c_ref[h]
        n_old = n_ref[h]
        num = _dot(s.astype(BF16), v) + w_inter * _dot(q, c_old.astype(BF16))
        qn = jnp.sum(q.astype(F32) * n_old, axis=-1, keepdims=True)
        den = jnp.sum(s, axis=-1, keepdims=True) + w_inter * qn
        hh = num / jnp.maximum(jnp.abs(den), jnp.exp(-m_i))

        b_last = b[chunk - 1:chunk, :]
        gg = b_last - b + ig
        m_new = jnp.maximum(b_last + m_prev, jnp.max(gg, axis=0, keepdims=True))
        kw = k.astype(F32) * jnp.exp(gg - m_new)
        decay = jnp.exp(b_last + m_prev - m_new)
        c_ref[h] = decay * c_old + _dot_tn(kw.astype(BF16), v)
        n_ref[h] = decay * n_old + jnp.sum(kw, axis=0, keepdims=True)
        m_ref[h] = m_new

        mu = jnp.mean(hh, axis=-1, keepdims=True)
        cen = hh - mu
        var = jnp.mean(cen * cen, axis=-1, keepdims=True)
        hn = cen * lax.rsqrt(var + LN_EPS) * gout_ref[:, lo:hi]
        o_ref[:, lo:hi] = ((hn + skip_ref[:, lo:hi] * xc_ref[:, lo:hi].astype(F32))
                           * sz_ref[:, lo:hi].astype(F32)).astype(o_ref.dtype)


def _mlstm_chunks(q, k, v, gates, xc, sz, g_out, skip, batch, seq, chunk):
    t = q.shape[0]
    nc = seq // chunk
    row = lambda n: pl.BlockSpec((chunk, n), lambda b, c: (b * nc + c, 0))
    full = lambda a: pl.BlockSpec(a.shape, lambda b, c: (0,) * a.ndim)
    return pl.pallas_call(
        functools.partial(_mlstm_chunk_kernel, chunk=chunk),
        grid=(batch, nc),
        in_specs=[row(MLSTM_INNER)] * 3 + [row(2 * LANES)] + [row(MLSTM_INNER)] * 2
                 + [full(g_out), full(skip)],
        out_specs=row(MLSTM_INNER),
        out_shape=jax.ShapeDtypeStruct((t, MLSTM_INNER), BF16),
        scratch_shapes=[pltpu.VMEM((MLSTM_HEADS, MLSTM_HD, MLSTM_HD), F32),
                        pltpu.VMEM((MLSTM_HEADS, 1, MLSTM_HD), F32),
                        pltpu.VMEM((MLSTM_HEADS, 1, 1), F32)],
        compiler_params=_params("parallel", "arbitrary"),
        name="mlstm_chunks",
    )(q, k, v, gates, xc, sz, g_out, skip)


def _rot_half(w):
    half = MLA_ROPE // 2
    return jnp.concatenate([-w[..., half:], w[..., :half]], axis=-1)


def _prep_mla(w_in, g_q, w_uq, g_kv, w_ukv):
    pad = jnp.zeros((D_MODEL, LANES - MLA_ROPE), F32)
    w_pe = w_in[:, MLA_Q_RANK + MLA_KV_RANK:]
    w_in_p = jnp.concatenate([w_in[:, :MLA_Q_RANK + MLA_KV_RANK], w_pe, pad, _rot_half(w_pe), pad], axis=1)
    uq = w_uq.reshape(MLA_Q_RANK, MLA_HEADS, MLA_NOPE + MLA_ROPE)
    zq = jnp.zeros((MLA_Q_RANK, MLA_HEADS, LANES - MLA_ROPE), F32)
    w_q1 = jnp.concatenate([uq, zq], axis=-1).reshape(MLA_Q_RANK, MLA_HEADS * HEAD_SLOT)
    w_q2 = jnp.concatenate([_rot_half(uq[..., MLA_NOPE:]), zq], axis=-1).reshape(MLA_Q_RANK, MLA_HEADS * LANES)
    ukv = w_ukv.reshape(MLA_KV_RANK, MLA_HEADS, MLA_NOPE + MLA_V)
    w_k = ukv[..., :MLA_NOPE].reshape(MLA_KV_RANK, -1)
    w_vt = ukv[..., MLA_NOPE:].reshape(MLA_KV_RANK, -1).T
    return dict(w_in=w_in_p.astype(BF16), g_q=g_q.reshape(1, -1), g_kv=g_kv.reshape(1, -1),
                w_q1=w_q1.astype(BF16), w_q2=w_q2.astype(BF16), w_k=w_k.astype(BF16),
                w_vt=w_vt.astype(BF16))


def _block_diag_tiles(w):
    per = MXU_DIM // MLSTM_QKV_BLOCK
    w = w.reshape(MLSTM_INNER // MXU_DIM, per, MLSTM_QKV_BLOCK, MLSTM_QKV_BLOCK)
    eye = jnp.eye(per, dtype=w.dtype)
    return jnp.einsum("cgij,gh->cgihj", w, eye).reshape(MLSTM_INNER // MXU_DIM, MXU_DIM, MXU_DIM)


def _prep_mlstm(w_in, conv_w, conv_b, w_q, w_k, w_v, w_ig, b_ig, w_fg, b_fg):
    bd_qk = jnp.concatenate([_block_diag_tiles(w_q), _block_diag_tiles(w_k)], axis=-1)
    zw = jnp.zeros((3 * MLSTM_INNER, LANES - MLSTM_HEADS), F32)
    zb = jnp.zeros((LANES - MLSTM_HEADS,), F32)
    w_gate = jnp.concatenate([w_ig, zw, w_fg, zw], axis=1)
    b_gate = jnp.concatenate([b_ig, zb, b_fg, zb]).reshape(1, 2 * LANES)
    return dict(w_in=w_in.astype(BF16), conv_w=conv_w, conv_b=conv_b.reshape(1, -1),
                bd_qk=bd_qk.astype(BF16), bd_v=_block_diag_tiles(w_v).astype(BF16),
                w_gate=w_gate.astype(BF16), b_gate=b_gate)


def kernel(x, positions, router_w, router_b, mla_w_in, mla_g_q, mla_w_uq, mla_g_kv, mla_w_ukv, mla_w_o, mlstm_w_in, mlstm_conv_w, mlstm_conv_b, mlstm_w_q, mlstm_w_k, mlstm_w_v, mlstm_w_ig, mlstm_b_ig, mlstm_w_fg, mlstm_b_fg, mlstm_g_out, mlstm_skip, mlstm_w_out, moe_w_gate, moe_w_up, moe_w_down, ln_g, ln_b):
    batch, seq, _ = x.shape
    t = batch * seq
    tiles = _tiles(seq)
    cos, sin = _rope_tables(positions, tiles["rope"])
    rwt = router_w.T.astype(BF16)
    rb = router_b.reshape(N_EXPERTS, 1)
    rw_pad = jnp.concatenate([router_w, jnp.zeros((D_MODEL, LANES - N_EXPERTS), F32)], axis=1).astype(BF16)
    xt = x.reshape(t, D_MODEL)
    for i in range(DEPTH):
        j = i // N_MIXERS
        g0, b0 = ln_g[i, 0].reshape(1, -1), ln_b[i, 0].reshape(1, -1)
        g1, b1 = ln_g[i, 1].reshape(1, -1), ln_b[i, 1].reshape(1, -1)
        if i % N_MIXERS == 0:
            w = _prep_mla(mla_w_in[j], mla_g_q[j], mla_w_uq[j], mla_g_kv[j], mla_w_ukv[j])
            q, k, vt = _mla_proj(xt, cos, sin, w, tiles["attn"])
            a = _attention(q, k, vt, batch, seq, tiles["attn"])
            w_out = mla_w_o[j].astype(BF16)
        else:
            w = _prep_mlstm(mlstm_w_in[j], mlstm_conv_w[j], mlstm_conv_b[j], mlstm_w_q[j], mlstm_w_k[j],
                            mlstm_w_v[j], mlstm_w_ig[j], mlstm_b_ig[j], mlstm_w_fg[j], mlstm_b_fg[j])
            q, k, v, xc, sz, gates = _mlstm_pre(xt, w, batch, seq, tiles["pre"])
            a = _mlstm_chunks(q, k, v, gates, xc, sz, mlstm_g_out[j].reshape(1, -1),
                              mlstm_skip[j].reshape(1, -1), batch, seq, tiles["chunk"])
            w_out = mlstm_w_out[j].astype(BF16)
        x1, cls = _post(a, w_out, xt, g0, b0, rwt, rb, tiles["post"])
        xt = _moe(x1, cls.reshape(t), rw_pad, moe_w_gate[i], moe_w_up[i], moe_w_down[i], g1, b1, tiles["moe"])
    return xt.reshape(batch, seq, D_MODEL)
```

```python
import functools

import jax
import jax.numpy as jnp
from jax import lax
from jax.experimental import pallas as pl
from jax.experimental.pallas import tpu as pltpu
from jax.experimental.pallas import tpu_sc as plsc

D_MODEL = 1024
DEPTH = 4
N_MIXERS = 2
MLA_HEADS = 8
MLA_Q_RANK = 256
MLA_KV_RANK = 256
MLA_NOPE = 128
MLA_ROPE = 64
MLA_V = 128
ROPE_THETA = 10000.0
MLSTM_INNER = 2 * D_MODEL
MLSTM_HEADS = 4
MLSTM_HD = MLSTM_INNER // MLSTM_HEADS
MLSTM_CONV = 4
MLSTM_QKV_BLOCK = 4
N_EXPERTS = 16
N_GROUPS = 4
EXPERTS_PER_GROUP = N_EXPERTS // N_GROUPS
EXPERT_FF = 512
DN_ALPHA = (2.0 * DEPTH) ** 0.25
LN_EPS = 1e-5
LOG2_E = 1.4426950408889634

LANES = 128
MXU_DIM = 256
HEAD_SLOT = 2 * LANES
ATTN_HEADS_PER_STEP = 4
N_PAIRS = EXPERTS_PER_GROUP * (EXPERTS_PER_GROUP - 1) // 2
N_CLASSES = N_GROUPS * N_PAIRS
STREAMS = 2
SC_WORKERS = 32
SC_ROWS = 32
VMEM_LIMIT = 56 * 1024 * 1024

F32 = jnp.float32
BF16 = jnp.bfloat16


def _tiles(seq):
    return dict(
        rope=min(seq, 2048),
        attn=min(seq, 512),
        post=min(seq, 512),
        pre=min(seq, 256),
        chunk=min(seq, 256),
        moe=256,
    )


def _params(*sem):
    return pltpu.CompilerParams(dimension_semantics=sem, vmem_limit_bytes=VMEM_LIMIT)


def _dot(a, b):
    return jnp.dot(a, b, preferred_element_type=F32)


def _dot_nt(a, b):
    return lax.dot_general(a, b, (((1,), (1,)), ((), ())), preferred_element_type=F32)


def _dot_tn(a, b):
    return lax.dot_general(a, b, (((0,), (0,)), ((), ())), preferred_element_type=F32)


def _layer_norm(v, g, b):
    mu = jnp.mean(v, axis=-1, keepdims=True)
    c = v - mu
    var = jnp.mean(c * c, axis=-1, keepdims=True)
    return c * lax.rsqrt(var + LN_EPS) * g + b


def _rms_norm(v, g):
    return v * lax.rsqrt(jnp.mean(v * v, axis=-1, keepdims=True) + LN_EPS) * g


def _sigmoid(v):
    return 1.0 / (1.0 + jnp.exp(-v))


def _silu(v):
    return v * _sigmoid(v)


def _rope_kernel(pos_ref, invf_ref, cos_ref, sin_ref):
    ang = pos_ref[...].astype(F32) * invf_ref[...]
    cos_ref[...] = jnp.cos(ang)
    sin_ref[...] = jnp.sin(ang)


def _rope_tables(positions, tm):
    t = positions.size
    inv_freq = 1.0 / (ROPE_THETA ** (jnp.arange(0, MLA_ROPE, 2, dtype=F32) / MLA_ROPE))
    invf = jnp.tile(inv_freq, LANES // (MLA_ROPE // 2)).reshape(1, LANES)
    out = jax.ShapeDtypeStruct((t, LANES), F32)
    return pl.pallas_call(
        _rope_kernel,
        grid=(t // tm,),
        in_specs=[pl.BlockSpec((tm, 1), lambda i: (i, 0)),
                  pl.BlockSpec((1, LANES), lambda i: (0, 0))],
        out_specs=[pl.BlockSpec((tm, LANES), lambda i: (i, 0))] * 2,
        out_shape=[out, out],
        compiler_params=_params("parallel"),
        name="rope_tables",
    )(positions.reshape(t, 1), invf)


def _mla_proj_kernel(x_ref, cos_ref, sin_ref, w_in_ref, gq_ref, gkv_ref, wq1_ref, wq2_ref,
                     wk_ref, wvt_ref, q_ref, k_ref, vt_ref):
    xb = x_ref[...].astype(BF16)
    c = _dot(xb, w_in_ref[...])
    cq = _rms_norm(c[:, :MLA_Q_RANK], gq_ref[...]).astype(BF16)
    ckv = _rms_norm(c[:, MLA_Q_RANK:MLA_Q_RANK + MLA_KV_RANK], gkv_ref[...]).astype(BF16)
    cos = cos_ref[...]
    sin = sin_ref[...]
    base = MLA_Q_RANK + MLA_KV_RANK
    k_pe = (c[:, base:base + LANES] * cos + c[:, base + LANES:base + 2 * LANES] * sin).astype(BF16)
    scale = (MLA_NOPE + MLA_ROPE) ** -0.5 * LOG2_E
    cos_s = cos * scale
    sin_s = sin * scale
    q1 = _dot(cq, wq1_ref[...])
    q2 = _dot(cq, wq2_ref[...])
    kn = _dot(ckv, wk_ref[...])
    for h in range(MLA_HEADS):
        s0 = h * HEAD_SLOT
        q_ref[:, s0:s0 + LANES] = (q1[:, s0:s0 + LANES] * scale).astype(BF16)
        q_ref[:, s0 + LANES:s0 + HEAD_SLOT] = (
            q1[:, s0 + LANES:s0 + HEAD_SLOT] * cos_s
            + q2[:, h * LANES:(h + 1) * LANES] * sin_s).astype(BF16)
        k_ref[:, s0:s0 + LANES] = kn[:, h * LANES:(h + 1) * LANES].astype(BF16)
        k_ref[:, s0 + LANES:s0 + HEAD_SLOT] = k_pe
    vt_ref[...] = _dot_nt(wvt_ref[...], ckv).astype(BF16)


def _mla_proj(x, cos, sin, w, tm):
    t = x.shape[0]
    row = lambda n: pl.BlockSpec((tm, n), lambda i: (i, 0))
    full = lambda a: pl.BlockSpec(a.shape, lambda i: (0,) * a.ndim)
    consts = (w["w_in"], w["g_q"], w["g_kv"], w["w_q1"], w["w_q2"], w["w_k"], w["w_vt"])
    return pl.pallas_call(
        _mla_proj_kernel,
        grid=(t // tm,),
        in_specs=[row(D_MODEL), row(LANES), row(LANES)] + [full(a) for a in consts],
        out_specs=[row(MLA_HEADS * HEAD_SLOT), row(MLA_HEADS * HEAD_SLOT),
                   pl.BlockSpec((None, MLA_HEADS * MLA_V, tm), lambda i: (i, 0, 0))],
        out_shape=[jax.ShapeDtypeStruct((t, MLA_HEADS * HEAD_SLOT), BF16),
                   jax.ShapeDtypeStruct((t, MLA_HEADS * HEAD_SLOT), BF16),
                   jax.ShapeDtypeStruct((t // tm, MLA_HEADS * MLA_V, tm), BF16)],
        compiler_params=_params("parallel"),
        name="mla_proj",
    )(x, cos, sin, *consts)


def _attn_kernel(q_ref, k_ref, vt_ref, o_ref, s_ref, m_ref, l_ref, acc_ref, *, tile):
    qi = pl.program_id(2)
    m_ref[...] = jnp.full(m_ref.shape, -jnp.inf, F32)
    l_ref[...] = jnp.zeros(l_ref.shape, F32)
    acc_ref[...] = jnp.zeros(acc_ref.shape, F32)
    heads = range(ATTN_HEADS_PER_STEP)

    def scores(ki, buf):
        start = pl.multiple_of(ki * tile, tile)
        for g in heads:
            slot = slice(g * HEAD_SLOT, (g + 1) * HEAD_SLOT)
            s_ref[buf, g] = _dot_nt(k_ref[pl.ds(start, tile), slot], q_ref[:, slot])

    def absorb(ki, buf, masked):
        for g in heads:
            s = s_ref[buf, g]
            if masked:
                keys = lax.broadcasted_iota(jnp.int32, s.shape, 0)
                queries = lax.broadcasted_iota(jnp.int32, s.shape, 1)
                s = jnp.where(keys <= queries, s, -jnp.inf)
            m_old = m_ref[g]
            m_new = jnp.maximum(m_old, jnp.max(s, axis=0, keepdims=True))
            p = jnp.exp2(s - m_new)
            alpha = jnp.exp2(m_old - m_new)
            l_ref[g] = alpha * l_ref[g] + jnp.sum(p, axis=0, keepdims=True)
            acc_ref[g] = alpha * acc_ref[g] + _dot(vt_ref[ki, g * MLA_V:(g + 1) * MLA_V, :], p.astype(BF16))
            m_ref[g] = m_new

    scores(0, 0)

    def body(j, carry):
        scores(2 * j + 1, 1)
        absorb(2 * j, 0, False)
        scores(2 * j + 2, 0)
        absorb(2 * j + 1, 1, False)
        return carry

    lax.fori_loop(0, qi // 2, body, 0)

    @pl.when(qi % 2 == 1)
    def _():
        scores(qi, 1)
        absorb(qi - 1, 0, False)
        absorb(qi, 1, True)

    @pl.when(qi % 2 == 0)
    def _():
        absorb(qi, 0, True)

    for g in heads:
        o_ref[:, g * MLA_V:(g + 1) * MLA_V] = jnp.transpose(acc_ref[g] / l_ref[g]).astype(o_ref.dtype)


def _attention(q, k, vt, batch, seq, tile):
    t = q.shape[0]
    nq = seq // tile
    hps = ATTN_HEADS_PER_STEP
    return pl.pallas_call(
        functools.partial(_attn_kernel, tile=tile),
        grid=(batch, MLA_HEADS // hps, nq),
        in_specs=[pl.BlockSpec((tile, hps * HEAD_SLOT), lambda b, h, i: (b * nq + i, h)),
                  pl.BlockSpec((seq, hps * HEAD_SLOT), lambda b, h, i: (b, h)),
                  pl.BlockSpec((nq, hps * MLA_V, tile), lambda b, h, i: (b, h, 0))],
        out_specs=pl.BlockSpec((tile, hps * MLA_V), lambda b, h, i: (b * nq + i, h)),
        out_shape=jax.ShapeDtypeStruct((t, MLA_HEADS * MLA_V), BF16),
        scratch_shapes=[pltpu.VMEM((2, hps, tile, tile), F32),
                        pltpu.VMEM((hps, 1, tile), F32), pltpu.VMEM((hps, 1, tile), F32),
                        pltpu.VMEM((hps, MLA_V, tile), F32)],
        compiler_params=_params("parallel", "parallel", "arbitrary"),
        name="mla_attention",
    )(q, k, vt)


def _route_rows(sel):
    best = None
    for g in range(N_GROUPS):
        a = [sel[g * EXPERTS_PER_GROUP + j:g * EXPERTS_PER_GROUP + j + 1, :]
             for j in range(EXPERTS_PER_GROUP)]
        v1 = functools.reduce(jnp.maximum, a)
        i1 = jnp.full(v1.shape, EXPERTS_PER_GROUP - 1, jnp.int32)
        for j in reversed(range(EXPERTS_PER_GROUP - 1)):
            i1 = jnp.where(a[j] == v1, j, i1)
        r = [jnp.where(i1 == j, -jnp.inf, a[j]) for j in range(EXPERTS_PER_GROUP)]
        v2 = functools.reduce(jnp.maximum, r)
        i2 = jnp.full(v1.shape, EXPERTS_PER_GROUP - 1, jnp.int32)
        for j in reversed(range(EXPERTS_PER_GROUP - 1)):
            i2 = jnp.where(r[j] == v2, j, i2)
        total = v1 + v2
        if best is None:
            best = (total, jnp.zeros_like(i1), i1, i2)
        else:
            better = total > best[0]
            best = (jnp.where(better, total, best[0]), jnp.where(better, g, best[1]),
                    jnp.where(better, i1, best[2]), jnp.where(better, i2, best[3]))
    _, grp, i1, i2 = best
    lo = jnp.minimum(i1, i2)
    hi = jnp.maximum(i1, i2)
    pair = jnp.where(lo == 0, hi - 1, jnp.where(lo == 1, hi + 1, N_PAIRS - 1))
    return grp * N_PAIRS + pair


def _post_kernel(a_ref, w_ref, x_ref, g_ref, b_ref, rwt_ref, rb_ref, y_ref, cls_ref):
    y = _dot(a_ref[...], w_ref[...])
    x1 = _layer_norm(DN_ALPHA * x_ref[...] + y, g_ref[...], b_ref[...])
    y_ref[...] = x1
    logits = _dot_nt(rwt_ref[...], x1.astype(BF16))
    cls_ref[...] = _route_rows(_sigmoid(logits) + rb_ref[...])


def _post(a, w, x, g, b, rwt, rb, tm):
    t, k = a.shape
    row = lambda n: pl.BlockSpec((tm, n), lambda i: (i, 0))
    full = lambda arr: pl.BlockSpec(arr.shape, lambda i: (0,) * arr.ndim)
    return pl.pallas_call(
        _post_kernel,
        grid=(t // tm,),
        in_specs=[row(k), full(w), row(D_MODEL), full(g), full(b), full(rwt), full(rb)],
        out_specs=[row(D_MODEL), pl.BlockSpec((1, tm), lambda i: (0, i))],
        out_shape=[jax.ShapeDtypeStruct((t, D_MODEL), F32),
                   jax.ShapeDtypeStruct((1, t), jnp.int32)],
        compiler_params=_params("parallel"),
        name="mixer_out_norm_route",
    )(a, w, x, g, b, rwt, rb)


def _expert_kernel(ea_ref, eb_ref, fresh_ref, nblk_ref, x_ref, rw_ref, wga_ref, wua_ref, wda_ref,
                   wgb_ref, wub_ref, wdb_ref, g_ref, b_ref, o_ref, wga, wua, wda, wgb, wub, wdb):
    i = pl.program_id(0)

    @pl.when(fresh_ref[i] == 1)
    def _():
        for src, dst in ((wga_ref, wga), (wua_ref, wua), (wda_ref, wda),
                         (wgb_ref, wgb), (wub_ref, wub), (wdb_ref, wdb)):
            dst[...] = src[...].astype(BF16)

    @pl.when(i < nblk_ref[0])
    def _():
        x = x_ref[...]
        xb = x.astype(BF16)
        sig = _sigmoid(_dot(xb, rw_ref[...]))
        lane = lax.broadcasted_iota(jnp.int32, sig.shape, 1)
        sa = jnp.sum(jnp.where(lane == ea_ref[i], sig, 0.0), axis=-1, keepdims=True)
        sb = jnp.sum(jnp.where(lane == eb_ref[i], sig, 0.0), axis=-1, keepdims=True)
        tot = sa + sb
        ha = _silu(_dot(xb, wga[...])) * _dot(xb, wua[...])
        hb = _silu(_dot(xb, wgb[...])) * _dot(xb, wub[...])
        y = (sa / tot) * _dot(ha.astype(BF16), wda[...]) + (sb / tot) * _dot(hb.astype(BF16), wdb[...])
        o_ref[...] = _layer_norm(DN_ALPHA * x + y, g_ref[...], b_ref[...])

    @pl.when(i >= nblk_ref[0])
    def _():
        o_ref[...] = jnp.zeros(o_ref.shape, o_ref.dtype)


def _experts(xs, ea, eb, fresh, nblk, rw, layer, wg, wu, wd, g, b, bm):
    p = xs.shape[0]
    sel = lambda which: (lambda i, ea, eb, fr, nb: (layer, (ea if which == 0 else eb)[i], 0, 0))
    up = lambda which: pl.BlockSpec((None, None, D_MODEL, EXPERT_FF), sel(which))
    down = lambda which: pl.BlockSpec((None, None, EXPERT_FF, D_MODEL), sel(which))
    full = lambda arr: pl.BlockSpec(arr.shape, lambda i, ea, eb, fr, nb: (0,) * arr.ndim)
    row = pl.BlockSpec((bm, D_MODEL), lambda i, ea, eb, fr, nb: (i, 0))
    up_s = pltpu.VMEM((D_MODEL, EXPERT_FF), BF16)
    down_s = pltpu.VMEM((EXPERT_FF, D_MODEL), BF16)
    return pl.pallas_call(
        _expert_kernel,
        grid_spec=pltpu.PrefetchScalarGridSpec(
            num_scalar_prefetch=4,
            grid=(p // bm,),
            in_specs=[row, full(rw), up(0), up(0), down(0), up(1), up(1), down(1), full(g), full(b)],
            out_specs=row,
            scratch_shapes=[up_s, up_s, down_s, up_s, up_s, down_s]),
        out_shape=jax.ShapeDtypeStruct((p, D_MODEL), F32),
        compiler_params=_params("arbitrary"),
        name="moe_experts",
    )(ea, eb, fresh, nblk, xs, rw, wg, wu, wd, wg, wu, wd, g, b)


def _row_gather(src, idx):
    m = idx.shape[0]
    d = src.shape[1]
    per = m // SC_WORKERS
    n = per // SC_ROWS
    assert per * SC_WORKERS == m and n * SC_ROWS == per and n % 2 == 0
    mesh = plsc.VectorSubcoreMesh(core_axis_name="core", subcore_axis_name="subcore")

    def body(src_hbm, idx_hbm, out_hbm, idx_v, buf, gsem, psem):
        worker = lax.axis_index("core") * (SC_WORKERS // 2) + lax.axis_index("subcore")
        base = worker * per
        pltpu.sync_copy(idx_hbm.at[pl.ds(base, per)], idx_v)

        def gather(c, slot):
            return pltpu.make_async_copy(src_hbm.at[idx_v.at[pl.ds(c * SC_ROWS, SC_ROWS)]],
                                         buf.at[slot], gsem.at[slot])

        def put(c, slot):
            return pltpu.make_async_copy(buf.at[slot], out_hbm.at[pl.ds(base + c * SC_ROWS, SC_ROWS)],
                                         psem.at[slot])

        gather(0, 0).start()

        @pl.loop(0, n // 2)
        def _(j):
            for slot in range(2):
                c = 2 * j + slot
                gather(c, slot).wait()
                put(c, slot).start()

                @pl.when(c >= 1)
                def _():
                    put(c - 1, 1 - slot).wait()

                @pl.when(c + 1 < n)
                def _():
                    gather(c + 1, 1 - slot).start()

        put(n - 1, 1).wait()

    return pl.kernel(
        body,
        out_type=jax.ShapeDtypeStruct((m, d), src.dtype),
        mesh=mesh,
        scratch_types=[pltpu.VMEM((per,), jnp.int32), pltpu.VMEM((2, SC_ROWS, d), src.dtype),
                       pltpu.SemaphoreType.DMA((2,)), pltpu.SemaphoreType.DMA((2,))],
    )(src, idx)


def _moe(x1, cls, rw, layer, wg, wu, wd, g, b, bm):
    t = x1.shape[0]
    p_rows = t + N_CLASSES * bm
    nblk = p_rows // bm
    classes = jnp.arange(N_CLASSES, dtype=jnp.int32)
    counts = jnp.sum((cls[:, None] == classes[None, :]).astype(jnp.int32), axis=0)
    padded = ((counts + bm - 1) // bm) * bm
    off = jnp.cumsum(counts) - counts
    pend = jnp.cumsum(padded)
    poff = pend - padded
    tok = jnp.arange(t, dtype=jnp.int32)
    cls_sorted, order = lax.sort((cls, tok), num_keys=1)
    shift = jnp.sum(jnp.where(cls_sorted[:, None] == classes[None, :], (poff - off)[None, :], 0), axis=1)
    _, dest = lax.sort((order, shift + tok), num_keys=1)
    blk_start = jnp.arange(nblk, dtype=jnp.int32) * bm
    blk_cls = jnp.minimum(jnp.sum((blk_start[:, None] >= pend[None, :]).astype(jnp.int32), axis=1),
                          N_CLASSES - 1)
    n_used = (pend[-1] // bm).astype(jnp.int32).reshape(1)
    j = jnp.arange(bm, dtype=jnp.int32)[None, :] + (blk_start - poff[blk_cls])[:, None]
    rank = off[blk_cls][:, None] + jnp.minimum(j, counts[blk_cls][:, None] - 1)
    src = order[jnp.clip(rank, 0, t - 1).reshape(p_rows)]
    grp = blk_cls // N_PAIRS
    pair = blk_cls % N_PAIRS
    lo = jnp.where(pair < 3, 0, jnp.where(pair < 5, 1, 2))
    hi = jnp.where(pair < 3, pair + 1, jnp.where(pair < 5, pair - 1, 3))
    ea = (grp * EXPERTS_PER_GROUP + lo).astype(jnp.int32)
    eb = (grp * EXPERTS_PER_GROUP + hi).astype(jnp.int32)
    fresh = jnp.concatenate([jnp.ones((1,), jnp.int32), (blk_cls[1:] != blk_cls[:-1]).astype(jnp.int32)])
    xs = _row_gather(x1, src)
    ys = _experts(xs, ea, eb, fresh, n_used, rw, layer, wg, wu, wd, g, b, bm)
    return _row_gather(ys, dest)


def _mlstm_pre_kernel(x_ref, w_in_ref, cw_ref, cb_ref, bdqk_ref, bdv_ref, wgate_ref, bgate_ref,
                      q_ref, k_ref, v_ref, xc_ref, sz_ref, gate_ref, ext_ref, *, tm):
    halo = 8
    xb = x_ref[...].astype(BF16)

    @pl.when(pl.program_id(1) == 0)
    def _():
        ext_ref[0:halo, :] = jnp.zeros((halo, MLSTM_INNER), F32)

    @pl.when(pl.program_id(1) != 0)
    def _():
        ext_ref[0:halo, :] = ext_ref[tm:tm + halo, :]

    xm = _dot(xb, w_in_ref[:, :MLSTM_INNER])
    ext_ref[halo:halo + tm, :] = xm
    conv = cb_ref[...] + cw_ref[MLSTM_CONV - 1:MLSTM_CONV, :] * xm
    for kk in range(MLSTM_CONV - 1):
        lag = MLSTM_CONV - 1 - kk
        conv = conv + cw_ref[kk:kk + 1, :] * ext_ref[halo - lag:halo - lag + tm, :]
    xc = _silu(conv)
    xc_b = xc.astype(BF16)
    xm_b = xm.astype(BF16)
    xc_ref[...] = xc_b
    sz_ref[...] = _silu(_dot(xb, w_in_ref[:, MLSTM_INNER:])).astype(BF16)

    gates = bgate_ref[...]
    k_scale = MLSTM_HD ** -0.5
    for c in range(MLSTM_INNER // MXU_DIM):
        lo, hi = c * MXU_DIM, (c + 1) * MXU_DIM
        qk = _dot(xc_b[:, lo:hi], bdqk_ref[c])
        vv = _dot(xm_b[:, lo:hi], bdv_ref[c])
        qb = qk[:, :MXU_DIM].astype(BF16)
        kb = qk[:, MXU_DIM:].astype(BF16)
        vb = vv.astype(BF16)
        q_ref[:, lo:hi] = qb
        k_ref[:, lo:hi] = (qk[:, MXU_DIM:] * k_scale).astype(BF16)
        v_ref[:, lo:hi] = vb
        gates = gates + _dot(qb, wgate_ref[lo:hi, :]) \
            + _dot(kb, wgate_ref[MLSTM_INNER + lo:MLSTM_INNER + hi, :]) \
            + _dot(vb, wgate_ref[2 * MLSTM_INNER + lo:2 * MLSTM_INNER + hi, :])
    fg = gates[:, LANES:]
    gate_ref[:, :LANES] = gates[:, :LANES]
    gate_ref[:, LANES:] = jnp.minimum(fg, 0.0) - jnp.log(1.0 + jnp.exp(-jnp.abs(fg)))


def _mlstm_pre(x, w, batch, seq, tm):
    t = x.shape[0]
    ns = seq // tm
    row = lambda n: pl.BlockSpec((tm, n), lambda b, s: (b * ns + s, 0))
    full = lambda a: pl.BlockSpec(a.shape, lambda b, s: (0,) * a.ndim)
    consts = (w["w_in"], w["conv_w"], w["conv_b"], w["bd_qk"], w["bd_v"], w["w_gate"], w["b_gate"])
    act = jax.ShapeDtypeStruct((t, MLSTM_INNER), BF16)
    return pl.pallas_call(
        functools.partial(_mlstm_pre_kernel, tm=tm),
        grid=(batch, ns),
        in_specs=[row(D_MODEL)] + [full(a) for a in consts],
        out_specs=[row(MLSTM_INNER)] * 5 + [row(2 * LANES)],
        out_shape=[act] * 5 + [jax.ShapeDtypeStruct((t, 2 * LANES), F32)],
        scratch_shapes=[pltpu.VMEM((tm + 8, MLSTM_INNER), F32)],
        compiler_params=_params("parallel", "arbitrary"),
        name="mlstm_pre",
    )(x, *consts)


def _split3(v):
    a = v.astype(BF16)
    r = v - a.astype(F32)
    b = r.astype(BF16)
    c = (r - b.astype(F32)).astype(BF16)
    return a, b, c


def _mlstm_chunk_kernel(q_ref, k_ref, v_ref, gate_ref, xc_ref, sz_ref, gout_ref, skip_ref, o_ref,
                        c_ref, n_ref, m_ref, *, chunk):
    @pl.when(pl.program_id(1) == 0)
    def _():
        c_ref[...] = jnp.zeros(c_ref.shape, F32)
        n_ref[...] = jnp.zeros(n_ref.shape, F32)
        m_ref[...] = jnp.zeros(m_ref.shape, F32)

    ig_all = gate_ref[:, :LANES]
    lf_all = gate_ref[:, LANES:]
    rows = lax.broadcasted_iota(jnp.int32, (chunk, chunk), 0)
    cols = lax.broadcasted_iota(jnp.int32, (chunk, chunk), 1)
    causal = cols <= rows
    tril = jnp.where(causal, 1.0, 0.0).astype(BF16)
    b_all = functools.reduce(lambda u, w: u + w, [_dot(tril, piece) for piece in _split3(lf_all)])
    row_all = jnp.transpose(ig_all - b_all)

    for h in range(MLSTM_HEADS):
        lo, hi = h * MLSTM_HD, (h + 1) * MLSTM_HD
        q = q_ref[:, lo:hi]
        k = k_ref[:, lo:hi]
        v = v_ref[:, lo:hi]
        b = b_all[:, h:h + 1]
        ig = ig_all[:, h:h + 1]
        m_prev = m_ref[h]
        d = jnp.where(causal, b + row_all[h:h + 1, :], -jnp.inf)
        inter = b + m_prev
        m_i = jnp.maximum(inter, jnp.max(d, axis=-1, keepdims=True))
        w_intra = jnp.exp(d - m_i)
        w_inter = jnp.exp(inter - m_i)
        s = _dot_nt(q, k) * w_intra
        c_old = c_ref[h]
        n_old = n_ref[h]
        num = _dot(s.astype(BF16), v) + w_inter * _dot(q, c_old.astype(BF16))
        qn = jnp.sum(q.astype(F32) * n_old, axis=-1, keepdims=True)
        den = jnp.sum(s, axis=-1, keepdims=True) + w_inter * qn
        hh = num / jnp.maximum(jnp.abs(den), jnp.exp(-m_i))

        b_last = b[chunk - 1:chunk, :]
        gg = b_last - b + ig
        m_new = jnp.maximum(b_last + m_prev, jnp.max(gg, axis=0, keepdims=True))
        kw = k.astype(F32) * jnp.exp(gg - m_new)
        decay = jnp.exp(b_last + m_prev - m_new)
        c_ref[h] = decay * c_old + _dot_tn(kw.astype(BF16), v)
        n_ref[h] = decay * n_old + jnp.sum(kw, axis=0, keepdims=True)
        m_ref[h] = m_new

        mu = jnp.mean(hh, axis=-1, keepdims=True)
        cen = hh - mu
        var = jnp.mean(cen * cen, axis=-1, keepdims=True)
        hn = cen * lax.rsqrt(var + LN_EPS) * gout_ref[:, lo:hi]
        o_ref[:, lo:hi] = ((hn + skip_ref[:, lo:hi] * xc_ref[:, lo:hi].astype(F32))
                           * sz_ref[:, lo:hi].astype(F32)).astype(o_ref.dtype)


def _mlstm_chunks(q, k, v, gates, xc, sz, g_out, skip, batch, seq, chunk):
    t = q.shape[0]
    nc = seq // chunk
    row = lambda n: pl.BlockSpec((chunk, n), lambda b, c: (b * nc + c, 0))
    full = lambda a: pl.BlockSpec(a.shape, lambda b, c: (0,) * a.ndim)
    return pl.pallas_call(
        functools.partial(_mlstm_chunk_kernel, chunk=chunk),
        grid=(batch, nc),
        in_specs=[row(MLSTM_INNER)] * 3 + [row(2 * LANES)] + [row(MLSTM_INNER)] * 2
                 + [full(g_out), full(skip)],
        out_specs=row(MLSTM_INNER),
        out_shape=jax.ShapeDtypeStruct((t, MLSTM_INNER), BF16),
        scratch_shapes=[pltpu.VMEM((MLSTM_HEADS, MLSTM_HD, MLSTM_HD), F32),
                        pltpu.VMEM((MLSTM_HEADS, 1, MLSTM_HD), F32),
                        pltpu.VMEM((MLSTM_HEADS, 1, 1), F32)],
        compiler_params=_params("parallel", "arbitrary"),
        name="mlstm_chunks",
    )(q, k, v, gates, xc, sz, g_out, skip)


def _rot_half(w):
    half = MLA_ROPE // 2
    return jnp.concatenate([-w[..., half:], w[..., :half]], axis=-1)


def _prep_mla(w_in, g_q, w_uq, g_kv, w_ukv):
    pad = jnp.zeros((D_MODEL, LANES - MLA_ROPE), F32)
    w_pe = w_in[:, MLA_Q_RANK + MLA_KV_RANK:]
    w_in_p = jnp.concatenate([w_in[:, :MLA_Q_RANK + MLA_KV_RANK], w_pe, pad, _rot_half(w_pe), pad], axis=1)
    uq = w_uq.reshape(MLA_Q_RANK, MLA_HEADS, MLA_NOPE + MLA_ROPE)
    zq = jnp.zeros((MLA_Q_RANK, MLA_HEADS, LANES - MLA_ROPE), F32)
    w_q1 = jnp.concatenate([uq, zq], axis=-1).reshape(MLA_Q_RANK, MLA_HEADS * HEAD_SLOT)
    w_q2 = jnp.concatenate([_rot_half(uq[..., MLA_NOPE:]), zq], axis=-1).reshape(MLA_Q_RANK, MLA_HEADS * LANES)
    ukv = w_ukv.reshape(MLA_KV_RANK, MLA_HEADS, MLA_NOPE + MLA_V)
    w_k = ukv[..., :MLA_NOPE].reshape(MLA_KV_RANK, -1)
    w_vt = ukv[..., MLA_NOPE:].reshape(MLA_KV_RANK, -1).T
    return dict(w_in=w_in_p.astype(BF16), g_q=g_q.reshape(1, -1), g_kv=g_kv.reshape(1, -1),
                w_q1=w_q1.astype(BF16), w_q2=w_q2.astype(BF16), w_k=w_k.astype(BF16),
                w_vt=w_vt.astype(BF16))


def _block_diag_tiles(w):
    per = MXU_DIM // MLSTM_QKV_BLOCK
    w = w.reshape(MLSTM_INNER // MXU_DIM, per, MLSTM_QKV_BLOCK, MLSTM_QKV_BLOCK)
    eye = jnp.eye(per, dtype=w.dtype)
    return jnp.einsum("cgij,gh->cgihj", w, eye).reshape(MLSTM_INNER // MXU_DIM, MXU_DIM, MXU_DIM)


def _prep_mlstm(w_in, conv_w, conv_b, w_q, w_k, w_v, w_ig, b_ig, w_fg, b_fg):
    bd_qk = jnp.concatenate([_block_diag_tiles(w_q), _block_diag_tiles(w_k)], axis=-1)
    zw = jnp.zeros((3 * MLSTM_INNER, LANES - MLSTM_HEADS), F32)
    zb = jnp.zeros((LANES - MLSTM_HEADS,), F32)
    w_gate = jnp.concatenate([w_ig, zw, w_fg, zw], axis=1)
    b_gate = jnp.concatenate([b_ig, zb, b_fg, zb]).reshape(1, 2 * LANES)
    return dict(w_in=w_in.astype(BF16), conv_w=conv_w, conv_b=conv_b.reshape(1, -1),
                bd_qk=bd_qk.astype(BF16), bd_v=_block_diag_tiles(w_v).astype(BF16),
                w_gate=w_gate.astype(BF16), b_gate=b_gate)


def kernel(x, positions, router_w, router_b, mla_w_in, mla_g_q, mla_w_uq, mla_g_kv, mla_w_ukv, mla_w_o, mlstm_w_in, mlstm_conv_w, mlstm_conv_b, mlstm_w_q, mlstm_w_k, mlstm_w_v, mlstm_w_ig, mlstm_b_ig, mlstm_w_fg, mlstm_b_fg, mlstm_g_out, mlstm_skip, mlstm_w_out, moe_w_gate, moe_w_up, moe_w_down, ln_g, ln_b):
    batch, seq, _ = x.shape
    tiles = _tiles(seq)
    rwt = router_w.T.astype(BF16)
    rb = router_b.reshape(N_EXPERTS, 1)
    rw_pad = jnp.concatenate([router_w, jnp.zeros((D_MODEL, LANES - N_EXPERTS), F32)], axis=1).astype(BF16)
    n_streams = STREAMS if batch % STREAMS == 0 else 1
    sb = batch // n_streams
    ts = sb * seq
    xs = [x[s * sb:(s + 1) * sb].reshape(ts, D_MODEL) for s in range(n_streams)]
    rope = [_rope_tables(positions[s * sb:(s + 1) * sb], tiles["rope"]) for s in range(n_streams)]
    for i in range(DEPTH):
        j = i // N_MIXERS
        g0, b0 = ln_g[i, 0].reshape(1, -1), ln_b[i, 0].reshape(1, -1)
        g1, b1 = ln_g[i, 1].reshape(1, -1), ln_b[i, 1].reshape(1, -1)
        if i % N_MIXERS == 0:
            w = _prep_mla(mla_w_in[j], mla_g_q[j], mla_w_uq[j], mla_g_kv[j], mla_w_ukv[j])
            w_out = mla_w_o[j].astype(BF16)
        else:
            w = _prep_mlstm(mlstm_w_in[j], mlstm_conv_w[j], mlstm_conv_b[j], mlstm_w_q[j], mlstm_w_k[j],
                            mlstm_w_v[j], mlstm_w_ig[j], mlstm_b_ig[j], mlstm_w_fg[j], mlstm_b_fg[j])
            w_out = mlstm_w_out[j].astype(BF16)
        for s in range(n_streams):
            xt = xs[s]
            if i % N_MIXERS == 0:
                q, k, vt = _mla_proj(xt, rope[s][0], rope[s][1], w, tiles["attn"])
                a = _attention(q, k, vt, sb, seq, tiles["attn"])
            else:
                q, k, v, xc, sz, gates = _mlstm_pre(xt, w, sb, seq, tiles["pre"])
                a = _mlstm_chunks(q, k, v, gates, xc, sz, mlstm_g_out[j].reshape(1, -1),
                                  mlstm_skip[j].reshape(1, -1), sb, seq, tiles["chunk"])
            x1, cls = _post(a, w_out, xt, g0, b0, rwt, rb, tiles["post"])
            xs[s] = _moe(x1, cls.reshape(ts), rw_pad, i, moe_w_gate, moe_w_up, moe_w_down, g1, b1, tiles["moe"])
    return jnp.concatenate(xs, axis=0).reshape(batch, seq, D_MODEL)
```

```python
import functools

import jax
import jax.numpy as jnp
from jax import lax
from jax.experimental import pallas as pl
from jax.experimental.pallas import tpu as pltpu
from jax.experimental.pallas import tpu_sc as plsc

D_MODEL = 1024
DEPTH = 4
N_MIXERS = 2
MLA_HEADS = 8
MLA_Q_RANK = 256
MLA_KV_RANK = 256
MLA_NOPE = 128
MLA_ROPE = 64
MLA_V = 128
ROPE_THETA = 10000.0
MLSTM_INNER = 2 * D_MODEL
MLSTM_HEADS = 4
MLSTM_HD = MLSTM_INNER // MLSTM_HEADS
MLSTM_CONV = 4
MLSTM_QKV_BLOCK = 4
N_EXPERTS = 16
N_GROUPS = 4
EXPERTS_PER_GROUP = N_EXPERTS // N_GROUPS
EXPERT_FF = 512
DN_ALPHA = (2.0 * DEPTH) ** 0.25
LN_EPS = 1e-5
LOG2_E = 1.4426950408889634

LANES = 128
MXU_DIM = 256
HEAD_SLOT = 2 * LANES
ATTN_HEADS_PER_STEP = 4
N_PAIRS = EXPERTS_PER_GROUP * (EXPERTS_PER_GROUP - 1) // 2
N_CLASSES = N_GROUPS * N_PAIRS
SC_WORKERS = 32
SC_ROWS = 32
POST_SUB = 512
VMEM_LIMIT = 56 * 1024 * 1024

F32 = jnp.float32
BF16 = jnp.bfloat16


def _tiles(seq):
    return dict(
        rope=min(seq, 2048),
        attn=min(seq, 512),
        post=min(seq, 1024),
        pre=min(seq, 256),
        chunk=min(seq, 256),
        moe=256,
    )


def _params(*sem):
    return pltpu.CompilerParams(dimension_semantics=sem, vmem_limit_bytes=VMEM_LIMIT)


def _dot(a, b):
    return jnp.dot(a, b, preferred_element_type=F32)


def _dot_nt(a, b):
    return lax.dot_general(a, b, (((1,), (1,)), ((), ())), preferred_element_type=F32)


def _dot_tn(a, b):
    return lax.dot_general(a, b, (((0,), (0,)), ((), ())), preferred_element_type=F32)


def _layer_norm(v, g, b):
    mu = jnp.mean(v, axis=-1, keepdims=True)
    c = v - mu
    var = jnp.mean(c * c, axis=-1, keepdims=True)
    return c * lax.rsqrt(var + LN_EPS) * g + b


def _rms_norm(v, g):
    return v * lax.rsqrt(jnp.mean(v * v, axis=-1, keepdims=True) + LN_EPS) * g


def _sigmoid(v):
    return 1.0 / (1.0 + jnp.exp(-v))


def _silu(v):
    return v * _sigmoid(v)


def _rope_kernel(pos_ref, invf_ref, cos_ref, sin_ref):
    ang = pos_ref[...].astype(F32) * invf_ref[...]
    cos_ref[...] = jnp.cos(ang)
    sin_ref[...] = jnp.sin(ang)


def _rope_tables(positions, tm):
    t = positions.size
    inv_freq = 1.0 / (ROPE_THETA ** (jnp.arange(0, MLA_ROPE, 2, dtype=F32) / MLA_ROPE))
    invf = jnp.tile(inv_freq, LANES // (MLA_ROPE // 2)).reshape(1, LANES)
    out = jax.ShapeDtypeStruct((t, LANES), F32)
    return pl.pallas_call(
        _rope_kernel,
        grid=(t // tm,),
        in_specs=[pl.BlockSpec((tm, 1), lambda i: (i, 0)),
                  pl.BlockSpec((1, LANES), lambda i: (0, 0))],
        out_specs=[pl.BlockSpec((tm, LANES), lambda i: (i, 0))] * 2,
        out_shape=[out, out],
        compiler_params=_params("parallel"),
        name="rope_tables",
    )(positions.reshape(t, 1), invf)


def _mla_proj_kernel(x_ref, cos_ref, sin_ref, w_in_ref, gq_ref, gkv_ref, wq1_ref, wq2_ref,
                     wk_ref, wvt_ref, q_ref, k_ref, vt_ref):
    xb = x_ref[...].astype(BF16)
    c = _dot(xb, w_in_ref[...])
    cq = _rms_norm(c[:, :MLA_Q_RANK], gq_ref[...]).astype(BF16)
    ckv = _rms_norm(c[:, MLA_Q_RANK:MLA_Q_RANK + MLA_KV_RANK], gkv_ref[...]).astype(BF16)
    cos = cos_ref[...]
    sin = sin_ref[...]
    base = MLA_Q_RANK + MLA_KV_RANK
    k_pe = (c[:, base:base + LANES] * cos + c[:, base + LANES:base + 2 * LANES] * sin).astype(BF16)
    scale = (MLA_NOPE + MLA_ROPE) ** -0.5 * LOG2_E
    cos_s = cos * scale
    sin_s = sin * scale
    q1 = _dot(cq, wq1_ref[...])
    q2 = _dot(cq, wq2_ref[...])
    kn = _dot(ckv, wk_ref[...])
    for h in range(MLA_HEADS):
        s0 = h * HEAD_SLOT
        q_ref[:, s0:s0 + LANES] = (q1[:, s0:s0 + LANES] * scale).astype(BF16)
        q_ref[:, s0 + LANES:s0 + HEAD_SLOT] = (
            q1[:, s0 + LANES:s0 + HEAD_SLOT] * cos_s
            + q2[:, h * LANES:(h + 1) * LANES] * sin_s).astype(BF16)
        k_ref[:, s0:s0 + LANES] = kn[:, h * LANES:(h + 1) * LANES].astype(BF16)
        k_ref[:, s0 + LANES:s0 + HEAD_SLOT] = k_pe
    vt_ref[...] = _dot_nt(wvt_ref[...], ckv).astype(BF16)


def _mla_proj(x, cos, sin, w, tm):
    t = x.shape[0]
    row = lambda n: pl.BlockSpec((tm, n), lambda i: (i, 0))
    full = lambda a: pl.BlockSpec(a.shape, lambda i: (0,) * a.ndim)
    consts = (w["w_in"], w["g_q"], w["g_kv"], w["w_q1"], w["w_q2"], w["w_k"], w["w_vt"])
    return pl.pallas_call(
        _mla_proj_kernel,
        grid=(t // tm,),
        in_specs=[row(D_MODEL), row(LANES), row(LANES)] + [full(a) for a in consts],
        out_specs=[row(MLA_HEADS * HEAD_SLOT), row(MLA_HEADS * HEAD_SLOT),
                   pl.BlockSpec((None, MLA_HEADS * MLA_V, tm), lambda i: (i, 0, 0))],
        out_shape=[jax.ShapeDtypeStruct((t, MLA_HEADS * HEAD_SLOT), BF16),
                   jax.ShapeDtypeStruct((t, MLA_HEADS * HEAD_SLOT), BF16),
                   jax.ShapeDtypeStruct((t // tm, MLA_HEADS * MLA_V, tm), BF16)],
        compiler_params=_params("parallel"),
        name="mla_proj",
    )(x, cos, sin, *consts)


def _attn_kernel(q_ref, k_ref, vt_ref, o_ref, s_ref, m_ref, l_ref, acc_ref, *, tile):
    qi = pl.program_id(2)
    m_ref[...] = jnp.full(m_ref.shape, -jnp.inf, F32)
    l_ref[...] = jnp.zeros(l_ref.shape, F32)
    acc_ref[...] = jnp.zeros(acc_ref.shape, F32)
    heads = range(ATTN_HEADS_PER_STEP)

    def scores(ki, buf):
        start = pl.multiple_of(ki * tile, tile)
        for g in heads:
            slot = slice(g * HEAD_SLOT, (g + 1) * HEAD_SLOT)
            s_ref[buf, g] = _dot_nt(k_ref[pl.ds(start, tile), slot], q_ref[:, slot])

    def absorb(ki, buf, masked):
        for g in heads:
            s = s_ref[buf, g]
            if masked:
                keys = lax.broadcasted_iota(jnp.int32, s.shape, 0)
                queries = lax.broadcasted_iota(jnp.int32, s.shape, 1)
                s = jnp.where(keys <= queries, s, -jnp.inf)
            m_old = m_ref[g]
            m_new = jnp.maximum(m_old, jnp.max(s, axis=0, keepdims=True))
            p = jnp.exp2(s - m_new)
            alpha = jnp.exp2(m_old - m_new)
            l_ref[g] = alpha * l_ref[g] + jnp.sum(p, axis=0, keepdims=True)
            acc_ref[g] = alpha * acc_ref[g] + _dot(vt_ref[ki, g * MLA_V:(g + 1) * MLA_V, :], p.astype(BF16))
            m_ref[g] = m_new

    scores(0, 0)

    def body(j, carry):
        scores(2 * j + 1, 1)
        absorb(2 * j, 0, False)
        scores(2 * j + 2, 0)
        absorb(2 * j + 1, 1, False)
        return carry

    lax.fori_loop(0, qi // 2, body, 0)

    @pl.when(qi % 2 == 1)
    def _():
        scores(qi, 1)
        absorb(qi - 1, 0, False)
        absorb(qi, 1, True)

    @pl.when(qi % 2 == 0)
    def _():
        absorb(qi, 0, True)

    for g in heads:
        o_ref[:, g * MLA_V:(g + 1) * MLA_V] = jnp.transpose(acc_ref[g] / l_ref[g]).astype(o_ref.dtype)


def _attention(q, k, vt, batch, seq, tile):
    t = q.shape[0]
    nq = seq // tile
    hps = ATTN_HEADS_PER_STEP
    return pl.pallas_call(
        functools.partial(_attn_kernel, tile=tile),
        grid=(batch, MLA_HEADS // hps, nq),
        in_specs=[pl.BlockSpec((tile, hps * HEAD_SLOT), lambda b, h, i: (b * nq + i, h)),
                  pl.BlockSpec((seq, hps * HEAD_SLOT), lambda b, h, i: (b, h)),
                  pl.BlockSpec((nq, hps * MLA_V, tile), lambda b, h, i: (b, h, 0))],
        out_specs=pl.BlockSpec((tile, hps * MLA_V), lambda b, h, i: (b * nq + i, h)),
        out_shape=jax.ShapeDtypeStruct((t, MLA_HEADS * MLA_V), BF16),
        scratch_shapes=[pltpu.VMEM((2, hps, tile, tile), F32),
                        pltpu.VMEM((hps, 1, tile), F32), pltpu.VMEM((hps, 1, tile), F32),
                        pltpu.VMEM((hps, MLA_V, tile), F32)],
        compiler_params=_params("parallel", "parallel", "arbitrary"),
        name="mla_attention",
    )(q, k, vt)


def _route_rows(sel):
    best = None
    for g in range(N_GROUPS):
        a = [sel[g * EXPERTS_PER_GROUP + j:g * EXPERTS_PER_GROUP + j + 1, :]
             for j in range(EXPERTS_PER_GROUP)]
        v1 = functools.reduce(jnp.maximum, a)
        i1 = jnp.full(v1.shape, EXPERTS_PER_GROUP - 1, jnp.int32)
        for j in reversed(range(EXPERTS_PER_GROUP - 1)):
            i1 = jnp.where(a[j] == v1, j, i1)
        r = [jnp.where(i1 == j, -jnp.inf, a[j]) for j in range(EXPERTS_PER_GROUP)]
        v2 = functools.reduce(jnp.maximum, r)
        i2 = jnp.full(v1.shape, EXPERTS_PER_GROUP - 1, jnp.int32)
        for j in reversed(range(EXPERTS_PER_GROUP - 1)):
            i2 = jnp.where(r[j] == v2, j, i2)
        total = v1 + v2
        if best is None:
            best = (total, jnp.zeros_like(i1), i1, i2)
        else:
            better = total > best[0]
            best = (jnp.where(better, total, best[0]), jnp.where(better, g, best[1]),
                    jnp.where(better, i1, best[2]), jnp.where(better, i2, best[3]))
    _, grp, i1, i2 = best
    lo = jnp.minimum(i1, i2)
    hi = jnp.maximum(i1, i2)
    pair = jnp.where(lo == 0, hi - 1, jnp.where(lo == 1, hi + 1, N_PAIRS - 1))
    return grp * N_PAIRS + pair


def _post_kernel(a_ref, w_ref, x_ref, g_ref, b_ref, rwt_ref, rb_ref, y_ref, cls_ref):
    subs = [slice(r, r + POST_SUB) for r in range(0, a_ref.shape[0], POST_SUB)]
    y = _dot(a_ref[subs[0], :], w_ref[...])
    for n, rows in enumerate(subs):
        y_next = _dot(a_ref[subs[n + 1], :], w_ref[...]) if n + 1 < len(subs) else None
        x1 = _layer_norm(DN_ALPHA * x_ref[rows, :] + y, g_ref[...], b_ref[...])
        y_ref[rows, :] = x1
        logits = _dot_nt(rwt_ref[...], x1.astype(BF16))
        cls_ref[:, rows] = _route_rows(_sigmoid(logits) + rb_ref[...])
        y = y_next


def _post(a, w, x, g, b, rwt, rb, tm):
    t, k = a.shape
    row = lambda n: pl.BlockSpec((tm, n), lambda i: (i, 0))
    full = lambda arr: pl.BlockSpec(arr.shape, lambda i: (0,) * arr.ndim)
    return pl.pallas_call(
        _post_kernel,
        grid=(t // tm,),
        in_specs=[row(k), full(w), row(D_MODEL), full(g), full(b), full(rwt), full(rb)],
        out_specs=[row(D_MODEL), pl.BlockSpec((1, tm), lambda i: (0, i))],
        out_shape=[jax.ShapeDtypeStruct((t, D_MODEL), F32),
                   jax.ShapeDtypeStruct((1, t), jnp.int32)],
        compiler_params=_params("parallel"),
        name="mixer_out_norm_route",
    )(a, w, x, g, b, rwt, rb)


def _expert_kernel(ea_ref, eb_ref, fresh_ref, nblk_ref, x_ref, rw_ref, wga_ref, wua_ref, wda_ref,
                   wgb_ref, wub_ref, wdb_ref, g_ref, b_ref, o_ref, wga, wua, wda, wgb, wub, wdb):
    i = pl.program_id(0)

    @pl.when(fresh_ref[i] == 1)
    def _():
        for src, dst in ((wga_ref, wga), (wua_ref, wua), (wda_ref, wda),
                         (wgb_ref, wgb), (wub_ref, wub), (wdb_ref, wdb)):
            dst[...] = src[...].astype(BF16)

    @pl.when(i < nblk_ref[0])
    def _():
        x = x_ref[...]
        xb = x.astype(BF16)
        sig = _sigmoid(_dot(xb, rw_ref[...]))
        lane = lax.broadcasted_iota(jnp.int32, sig.shape, 1)
        sa = jnp.sum(jnp.where(lane == ea_ref[i], sig, 0.0), axis=-1, keepdims=True)
        sb = jnp.sum(jnp.where(lane == eb_ref[i], sig, 0.0), axis=-1, keepdims=True)
        tot = sa + sb
        ha = _silu(_dot(xb, wga[...])) * _dot(xb, wua[...])
        hb = _silu(_dot(xb, wgb[...])) * _dot(xb, wub[...])
        y = (sa / tot) * _dot(ha.astype(BF16), wda[...]) + (sb / tot) * _dot(hb.astype(BF16), wdb[...])
        o_ref[...] = _layer_norm(DN_ALPHA * x + y, g_ref[...], b_ref[...])

    @pl.when(i >= nblk_ref[0])
    def _():
        o_ref[...] = jnp.zeros(o_ref.shape, o_ref.dtype)


def _experts(xs, ea, eb, fresh, nblk, rw, layer, wg, wu, wd, g, b, bm):
    p = xs.shape[0]
    sel = lambda which: (lambda i, ea, eb, fr, nb: (layer, (ea if which == 0 else eb)[i], 0, 0))
    up = lambda which: pl.BlockSpec((None, None, D_MODEL, EXPERT_FF), sel(which))
    down = lambda which: pl.BlockSpec((None, None, EXPERT_FF, D_MODEL), sel(which))
    full = lambda arr: pl.BlockSpec(arr.shape, lambda i, ea, eb, fr, nb: (0,) * arr.ndim)
    row = pl.BlockSpec((bm, D_MODEL), lambda i, ea, eb, fr, nb: (i, 0))
    up_s = pltpu.VMEM((D_MODEL, EXPERT_FF), BF16)
    down_s = pltpu.VMEM((EXPERT_FF, D_MODEL), BF16)
    return pl.pallas_call(
        _expert_kernel,
        grid_spec=pltpu.PrefetchScalarGridSpec(
            num_scalar_prefetch=4,
            grid=(p // bm,),
            in_specs=[row, full(rw), up(0), up(0), down(0), up(1), up(1), down(1), full(g), full(b)],
            out_specs=row,
            scratch_shapes=[up_s, up_s, down_s, up_s, up_s, down_s]),
        out_shape=jax.ShapeDtypeStruct((p, D_MODEL), F32),
        compiler_params=_params("arbitrary"),
        name="moe_experts",
    )(ea, eb, fresh, nblk, xs, rw, wg, wu, wd, wg, wu, wd, g, b)


def _row_gather(src, idx):
    m = idx.shape[0]
    d = src.shape[1]
    per = m // SC_WORKERS
    n = per // SC_ROWS
    assert per * SC_WORKERS == m and n * SC_ROWS == per and n % 2 == 0
    mesh = plsc.VectorSubcoreMesh(core_axis_name="core", subcore_axis_name="subcore")

    def body(src_hbm, idx_hbm, out_hbm, idx_v, buf, gsem, psem):
        worker = lax.axis_index("core") * (SC_WORKERS // 2) + lax.axis_index("subcore")
        base = worker * per
        pltpu.sync_copy(idx_hbm.at[pl.ds(base, per)], idx_v)

        def gather(c, slot):
            return pltpu.make_async_copy(src_hbm.at[idx_v.at[pl.ds(c * SC_ROWS, SC_ROWS)]],
                                         buf.at[slot], gsem.at[slot])

        def put(c, slot):
            return pltpu.make_async_copy(buf.at[slot], out_hbm.at[pl.ds(base + c * SC_ROWS, SC_ROWS)],
                                         psem.at[slot])

        gather(0, 0).start()

        @pl.loop(0, n // 2)
        def _(j):
            for slot in range(2):
                c = 2 * j + slot
                gather(c, slot).wait()
                put(c, slot).start()

                @pl.when(c >= 1)
                def _():
                    put(c - 1, 1 - slot).wait()

                @pl.when(c + 1 < n)
                def _():
                    gather(c + 1, 1 - slot).start()

        put(n - 1, 1).wait()

    return pl.kernel(
        body,
        out_type=jax.ShapeDtypeStruct((m, d), src.dtype),
        mesh=mesh,
        scratch_types=[pltpu.VMEM((per,), jnp.int32), pltpu.VMEM((2, SC_ROWS, d), src.dtype),
                       pltpu.SemaphoreType.DMA((2,)), pltpu.SemaphoreType.DMA((2,))],
    )(src, idx)


def _moe(x1, cls, rw, layer, wg, wu, wd, g, b, bm):
    t = x1.shape[0]
    p_rows = t + N_CLASSES * bm
    nblk = p_rows // bm
    classes = jnp.arange(N_CLASSES, dtype=jnp.int32)
    counts = jnp.sum((cls[:, None] == classes[None, :]).astype(jnp.int32), axis=0)
    padded = ((counts + bm - 1) // bm) * bm
    off = jnp.cumsum(counts) - counts
    pend = jnp.cumsum(padded)
    poff = pend - padded
    tok = jnp.arange(t, dtype=jnp.int32)
    cls_sorted, order = lax.sort((cls, tok), num_keys=1)
    shift = jnp.sum(jnp.where(cls_sorted[:, None] == classes[None, :], (poff - off)[None, :], 0), axis=1)
    _, dest = lax.sort((order, shift + tok), num_keys=1)
    blk_start = jnp.arange(nblk, dtype=jnp.int32) * bm
    blk_cls = jnp.minimum(jnp.sum((blk_start[:, None] >= pend[None, :]).astype(jnp.int32), axis=1),
                          N_CLASSES - 1)
    n_used = (pend[-1] // bm).astype(jnp.int32).reshape(1)
    j = jnp.arange(bm, dtype=jnp.int32)[None, :] + (blk_start - poff[blk_cls])[:, None]
    rank = off[blk_cls][:, None] + jnp.minimum(j, counts[blk_cls][:, None] - 1)
    src = order[jnp.clip(rank, 0, t - 1).reshape(p_rows)]
    grp = blk_cls // N_PAIRS
    pair = blk_cls % N_PAIRS
    lo = jnp.where(pair < 3, 0, jnp.where(pair < 5, 1, 2))
    hi = jnp.where(pair < 3, pair + 1, jnp.where(pair < 5, pair - 1, 3))
    ea = (grp * EXPERTS_PER_GROUP + lo).astype(jnp.int32)
    eb = (grp * EXPERTS_PER_GROUP + hi).astype(jnp.int32)
    fresh = jnp.concatenate([jnp.ones((1,), jnp.int32), (blk_cls[1:] != blk_cls[:-1]).astype(jnp.int32)])
    xs = _row_gather(x1, src)
    ys = _experts(xs, ea, eb, fresh, n_used, rw, layer, wg, wu, wd, g, b, bm)
    return _row_gather(ys, dest)


def _mlstm_pre_kernel(x_ref, w_in_ref, cw_ref, cb_ref, bdqk_ref, bdv_ref, wgate_ref, bgate_ref,
                      q_ref, k_ref, v_ref, xc_ref, sz_ref, gate_ref, ext_ref, *, tm):
    halo = 8
    xb = x_ref[...].astype(BF16)

    @pl.when(pl.program_id(1) == 0)
    def _():
        ext_ref[0:halo, :] = jnp.zeros((halo, MLSTM_INNER), F32)

    @pl.when(pl.program_id(1) != 0)
    def _():
        ext_ref[0:halo, :] = ext_ref[tm:tm + halo, :]

    xm = _dot(xb, w_in_ref[:, :MLSTM_INNER])
    ext_ref[halo:halo + tm, :] = xm
    conv = cb_ref[...] + cw_ref[MLSTM_CONV - 1:MLSTM_CONV, :] * xm
    for kk in range(MLSTM_CONV - 1):
        lag = MLSTM_CONV - 1 - kk
        conv = conv + cw_ref[kk:kk + 1, :] * ext_ref[halo - lag:halo - lag + tm, :]
    xc = _silu(conv)
    xc_b = xc.astype(BF16)
    xm_b = xm.astype(BF16)
    xc_ref[...] = xc_b
    sz_ref[...] = _silu(_dot(xb, w_in_ref[:, MLSTM_INNER:])).astype(BF16)

    gates = bgate_ref[...]
    k_scale = MLSTM_HD ** -0.5
    for c in range(MLSTM_INNER // MXU_DIM):
        lo, hi = c * MXU_DIM, (c + 1) * MXU_DIM
        qk = _dot(xc_b[:, lo:hi], bdqk_ref[c])
        vv = _dot(xm_b[:, lo:hi], bdv_ref[c])
        qb = qk[:, :MXU_DIM].astype(BF16)
        kb = qk[:, MXU_DIM:].astype(BF16)
        vb = vv.astype(BF16)
        q_ref[:, lo:hi] = qb
        k_ref[:, lo:hi] = (qk[:, MXU_DIM:] * k_scale).astype(BF16)
        v_ref[:, lo:hi] = vb
        gates = gates + _dot(qb, wgate_ref[lo:hi, :]) \
            + _dot(kb, wgate_ref[MLSTM_INNER + lo:MLSTM_INNER + hi, :]) \
            + _dot(vb, wgate_ref[2 * MLSTM_INNER + lo:2 * MLSTM_INNER + hi, :])
    fg = gates[:, LANES:]
    gate_ref[:, :LANES] = gates[:, :LANES]
    gate_ref[:, LANES:] = jnp.minimum(fg, 0.0) - jnp.log(1.0 + jnp.exp(-jnp.abs(fg)))


def _mlstm_pre(x, w, batch, seq, tm):
    t = x.shape[0]
    ns = seq // tm
    row = lambda n: pl.BlockSpec((tm, n), lambda b, s: (b * ns + s, 0))
    full = lambda a: pl.BlockSpec(a.shape, lambda b, s: (0,) * a.ndim)
    consts = (w["w_in"], w["conv_w"], w["conv_b"], w["bd_qk"], w["bd_v"], w["w_gate"], w["b_gate"])
    act = jax.ShapeDtypeStruct((t, MLSTM_INNER), BF16)
    return pl.pallas_call(
        functools.partial(_mlstm_pre_kernel, tm=tm),
        grid=(batch, ns),
        in_specs=[row(D_MODEL)] + [full(a) for a in consts],
        out_specs=[row(MLSTM_INNER)] * 5 + [row(2 * LANES)],
        out_shape=[act] * 5 + [jax.ShapeDtypeStruct((t, 2 * LANES), F32)],
        scratch_shapes=[pltpu.VMEM((tm + 8, MLSTM_INNER), F32)],
        compiler_params=_params("parallel", "arbitrary"),
        name="mlstm_pre",
    )(x, *consts)


def _split3(v):
    a = v.astype(BF16)
    r = v - a.astype(F32)
    b = r.astype(BF16)
    c = (r - b.astype(F32)).astype(BF16)
    return a, b, c


def _mlstm_chunk_kernel(q_ref, k_ref, v_ref, gate_ref, xc_ref, sz_ref, gout_ref, skip_ref, o_ref,
                        c_ref, n_ref, m_ref, *, chunk):
    @pl.when(pl.program_id(1) == 0)
    def _():
        c_ref[...] = jnp.zeros(c_ref.shape, F32)
        n_ref[...] = jnp.zeros(n_ref.shape, F32)
        m_ref[...] = jnp.zeros(m_ref.shape, F32)

    ig_all = gate_ref[:, :LANES]
    lf_all = gate_ref[:, LANES:]
    rows = lax.broadcasted_iota(jnp.int32, (chunk, chunk), 0)
    cols = lax.broadcasted_iota(jnp.int32, (chunk, chunk), 1)
    causal = cols <= rows
    tril = jnp.where(causal, 1.0, 0.0).astype(BF16)
    b_all = functools.reduce(lambda u, w: u + w, [_dot(tril, piece) for piece in _split3(lf_all)])
    row_all = jnp.transpose(ig_all - b_all)

    for h in range(MLSTM_HEADS):
        lo, hi = h * MLSTM_HD, (h + 1) * MLSTM_HD
        q = q_ref[:, lo:hi]
        k = k_ref[:, lo:hi]
        v = v_ref[:, lo:hi]
        b = b_all[:, h:h + 1]
        ig = ig_all[:, h:h + 1]
        m_prev = m_ref[h]
        d = jnp.where(causal, b + row_all[h:h + 1, :], -jnp.inf)
        inter = b + m_prev
        m_i = jnp.maximum(inter, jnp.max(d, axis=-1, keepdims=True))
        w_intra = jnp.exp(d - m_i)
        w_inter = jnp.exp(inter - m_i)
        s = _dot_nt(q, k) * w_intra
        c_old = c_ref[h]
        n_old = n_ref[h]
        num = _dot(s.astype(BF16), v) + w_inter * _dot(q, c_old.astype(BF16))
        qn = jnp.sum(q.astype(F32) * n_old, axis=-1, keepdims=True)
        den = jnp.sum(s, axis=-1, keepdims=True) + w_inter * qn
        hh = num / jnp.maximum(jnp.abs(den), jnp.exp(-m_i))

        b_last = b[chunk - 1:chunk, :]
        gg = b_last - b + ig
        m_new = jnp.maximum(b_last + m_prev, jnp.max(gg, axis=0, keepdims=True))
        kw = k.astype(F32) * jnp.exp(gg - m_new)
        decay = jnp.exp(b_last + m_prev - m_new)
        c_ref[h] = decay * c_old + _dot_tn(kw.astype(BF16), v)
        n_ref[h] = decay * n_old + jnp.sum(kw, axis=0, keepdims=True)
        m_ref[h] = m_new

        mu = jnp.mean(hh, axis=-1, keepdims=True)
        cen = hh - mu
        var = jnp.mean(cen * cen, axis=-1, keepdims=True)
        hn = cen * lax.rsqrt(var + LN_EPS) * gout_ref[:, lo:hi]
        o_ref[:, lo:hi] = ((hn + skip_ref[:, lo:hi] * xc_ref[:, lo:hi].astype(F32))
                           * sz_ref[:, lo:hi].astype(F32)).astype(o_ref.dtype)


def _mlstm_chunks(q, k, v, gates, xc, sz, g_out, skip, batch, seq, chunk):
    t = q.shape[0]
    nc = seq // chunk
    row = lambda n: pl.BlockSpec((chunk, n), lambda b, c: (b * nc + c, 0))
    full = lambda a: pl.BlockSpec(a.shape, lambda b, c: (0,) * a.ndim)
    return pl.pallas_call(
        functools.partial(_mlstm_chunk_kernel, chunk=chunk),
        grid=(batch, nc),
        in_specs=[row(MLSTM_INNER)] * 3 + [row(2 * LANES)] + [row(MLSTM_INNER)] * 2
                 + [full(g_out), full(skip)],
        out_specs=row(MLSTM_INNER),
        out_shape=jax.ShapeDtypeStruct((t, MLSTM_INNER), BF16),
        scratch_shapes=[pltpu.VMEM((MLSTM_HEADS, MLSTM_HD, MLSTM_HD), F32),
                        pltpu.VMEM((MLSTM_HEADS, 1, MLSTM_HD), F32),
                        pltpu.VMEM((MLSTM_HEADS, 1, 1), F32)],
        compiler_params=_params("parallel", "arbitrary"),
        name="mlstm_chunks",
    )(q, k, v, gates, xc, sz, g_out, skip)


def _rot_half(w):
    half = MLA_ROPE // 2
    return jnp.concatenate([-w[..., half:], w[..., :half]], axis=-1)


def _prep_mla(w_in, g_q, w_uq, g_kv, w_ukv):
    pad = jnp.zeros((D_MODEL, LANES - MLA_ROPE), F32)
    w_pe = w_in[:, MLA_Q_RANK + MLA_KV_RANK:]
    w_in_p = jnp.concatenate([w_in[:, :MLA_Q_RANK + MLA_KV_RANK], w_pe, pad, _rot_half(w_pe), pad], axis=1)
    uq = w_uq.reshape(MLA_Q_RANK, MLA_HEADS, MLA_NOPE + MLA_ROPE)
    zq = jnp.zeros((MLA_Q_RANK, MLA_HEADS, LANES - MLA_ROPE), F32)
    w_q1 = jnp.concatenate([uq, zq], axis=-1).reshape(MLA_Q_RANK, MLA_HEADS * HEAD_SLOT)
    w_q2 = jnp.concatenate([_rot_half(uq[..., MLA_NOPE:]), zq], axis=-1).reshape(MLA_Q_RANK, MLA_HEADS * LANES)
    ukv = w_ukv.reshape(MLA_KV_RANK, MLA_HEADS, MLA_NOPE + MLA_V)
    w_k = ukv[..., :MLA_NOPE].reshape(MLA_KV_RANK, -1)
    w_vt = ukv[..., MLA_NOPE:].reshape(MLA_KV_RANK, -1).T
    return dict(w_in=w_in_p.astype(BF16), g_q=g_q.reshape(1, -1), g_kv=g_kv.reshape(1, -1),
                w_q1=w_q1.astype(BF16), w_q2=w_q2.astype(BF16), w_k=w_k.astype(BF16),
                w_vt=w_vt.astype(BF16))


def _block_diag_tiles(w):
    per = MXU_DIM // MLSTM_QKV_BLOCK
    w = w.reshape(MLSTM_INNER // MXU_DIM, per, MLSTM_QKV_BLOCK, MLSTM_QKV_BLOCK)
    eye = jnp.eye(per, dtype=w.dtype)
    return jnp.einsum("cgij,gh->cgihj", w, eye).reshape(MLSTM_INNER // MXU_DIM, MXU_DIM, MXU_DIM)


def _prep_mlstm(w_in, conv_w, conv_b, w_q, w_k, w_v, w_ig, b_ig, w_fg, b_fg):
    bd_qk = jnp.concatenate([_block_diag_tiles(w_q), _block_diag_tiles(w_k)], axis=-1)
    zw = jnp.zeros((3 * MLSTM_INNER, LANES - MLSTM_HEADS), F32)
    zb = jnp.zeros((LANES - MLSTM_HEADS,), F32)
    w_gate = jnp.concatenate([w_ig, zw, w_fg, zw], axis=1)
    b_gate = jnp.concatenate([b_ig, zb, b_fg, zb]).reshape(1, 2 * LANES)
    return dict(w_in=w_in.astype(BF16), conv_w=conv_w, conv_b=conv_b.reshape(1, -1),
                bd_qk=bd_qk.astype(BF16), bd_v=_block_diag_tiles(w_v).astype(BF16),
                w_gate=w_gate.astype(BF16), b_gate=b_gate)


def kernel(x, positions, router_w, router_b, mla_w_in, mla_g_q, mla_w_uq, mla_g_kv, mla_w_ukv, mla_w_o, mlstm_w_in, mlstm_conv_w, mlstm_conv_b, mlstm_w_q, mlstm_w_k, mlstm_w_v, mlstm_w_ig, mlstm_b_ig, mlstm_w_fg, mlstm_b_fg, mlstm_g_out, mlstm_skip, mlstm_w_out, moe_w_gate, moe_w_up, moe_w_down, ln_g, ln_b):
    batch, seq, _ = x.shape
    tiles = _tiles(seq)
    rwt = router_w.T.astype(BF16)
    rb = router_b.reshape(N_EXPERTS, 1)
    rw_pad = jnp.concatenate([router_w, jnp.zeros((D_MODEL, LANES - N_EXPERTS), F32)], axis=1).astype(BF16)
    t = batch * seq
    cos, sin = _rope_tables(positions, tiles["rope"])
    xt = x.reshape(t, D_MODEL)
    for i in range(DEPTH):
        j = i // N_MIXERS
        g0, b0 = ln_g[i, 0].reshape(1, -1), ln_b[i, 0].reshape(1, -1)
        g1, b1 = ln_g[i, 1].reshape(1, -1), ln_b[i, 1].reshape(1, -1)
        if i % N_MIXERS == 0:
            w = _prep_mla(mla_w_in[j], mla_g_q[j], mla_w_uq[j], mla_g_kv[j], mla_w_ukv[j])
            w_out = mla_w_o[j].astype(BF16)
        else:
            w = _prep_mlstm(mlstm_w_in[j], mlstm_conv_w[j], mlstm_conv_b[j], mlstm_w_q[j], mlstm_w_k[j],
                            mlstm_w_v[j], mlstm_w_ig[j], mlstm_b_ig[j], mlstm_w_fg[j], mlstm_b_fg[j])
            w_out = mlstm_w_out[j].astype(BF16)
        if i % N_MIXERS == 0:
            q, k, vt = _mla_proj(xt, cos, sin, w, tiles["attn"])
            a = _attention(q, k, vt, batch, seq, tiles["attn"])
        else:
            q, k, v, xc, sz, gates = _mlstm_pre(xt, w, batch, seq, tiles["pre"])
            a = _mlstm_chunks(q, k, v, gates, xc, sz, mlstm_g_out[j].reshape(1, -1),
                              mlstm_skip[j].reshape(1, -1), batch, seq, tiles["chunk"])
        x1, cls = _post(a, w_out, xt, g0, b0, rwt, rb, tiles["post"])
        xt = _moe(x1, cls.reshape(t), rw_pad, i, moe_w_gate, moe_w_up, moe_w_down, g1, b1, tiles["moe"])
    return xt.reshape(batch, seq, D_MODEL)
```

```python
import functools

import jax
import jax.numpy as jnp
from jax import lax
from jax.experimental import pallas as pl
from jax.experimental.pallas import tpu as pltpu
from jax.experimental.pallas import tpu_sc as plsc

D_MODEL = 1024
DEPTH = 4
N_MIXERS = 2
MLA_HEADS = 8
MLA_Q_RANK = 256
MLA_KV_RANK = 256
MLA_NOPE = 128
MLA_ROPE = 64
MLA_V = 128
ROPE_THETA = 10000.0
MLSTM_INNER = 2 * D_MODEL
MLSTM_HEADS = 4
MLSTM_HD = MLSTM_INNER // MLSTM_HEADS
MLSTM_CONV = 4
MLSTM_QKV_BLOCK = 4
N_EXPERTS = 16
N_GROUPS = 4
EXPERTS_PER_GROUP = N_EXPERTS // N_GROUPS
EXPERT_FF = 512
DN_ALPHA = (2.0 * DEPTH) ** 0.25
LN_EPS = 1e-5
LOG2_E = 1.4426950408889634

LANES = 128
MXU_DIM = 256
HEAD_SLOT = 2 * LANES
ATTN_HEADS_PER_STEP = 4
N_PAIRS = EXPERTS_PER_GROUP * (EXPERTS_PER_GROUP - 1) // 2
N_CLASSES = N_GROUPS * N_PAIRS
SC_WORKERS = 32
SC_MAX_ROWS = 48
MOE_SPLIT = 4
POST_SUB = 512
VMEM_LIMIT = 56 * 1024 * 1024

F32 = jnp.float32
BF16 = jnp.bfloat16


def _tiles(seq):
    return dict(
        rope=min(seq, 2048),
        attn=min(seq, 512),
        post=min(seq, 1024),
        pre=min(seq, 256),
        chunk=min(seq, 256),
        moe=256,
    )


def _params(*sem):
    return pltpu.CompilerParams(dimension_semantics=sem, vmem_limit_bytes=VMEM_LIMIT)


def _dot(a, b):
    return jnp.dot(a, b, preferred_element_type=F32)


def _dot_nt(a, b):
    return lax.dot_general(a, b, (((1,), (1,)), ((), ())), preferred_element_type=F32)


def _dot_tn(a, b):
    return lax.dot_general(a, b, (((0,), (0,)), ((), ())), preferred_element_type=F32)


def _layer_norm(v, g, b):
    mu = jnp.mean(v, axis=-1, keepdims=True)
    c = v - mu
    var = jnp.mean(c * c, axis=-1, keepdims=True)
    return c * lax.rsqrt(var + LN_EPS) * g + b


def _rms_norm(v, g):
    return v * lax.rsqrt(jnp.mean(v * v, axis=-1, keepdims=True) + LN_EPS) * g


def _sigmoid(v):
    return 1.0 / (1.0 + jnp.exp(-v))


def _silu(v):
    return v * _sigmoid(v)


def _rope_kernel(pos_ref, invf_ref, cos_ref, sin_ref):
    ang = pos_ref[...].astype(F32) * invf_ref[...]
    cos_ref[...] = jnp.cos(ang)
    sin_ref[...] = jnp.sin(ang)


def _rope_tables(positions, tm):
    t = positions.size
    inv_freq = 1.0 / (ROPE_THETA ** (jnp.arange(0, MLA_ROPE, 2, dtype=F32) / MLA_ROPE))
    invf = jnp.tile(inv_freq, LANES // (MLA_ROPE // 2)).reshape(1, LANES)
    out = jax.ShapeDtypeStruct((t, LANES), F32)
    return pl.pallas_call(
        _rope_kernel,
        grid=(t // tm,),
        in_specs=[pl.BlockSpec((tm, 1), lambda i: (i, 0)),
                  pl.BlockSpec((1, LANES), lambda i: (0, 0))],
        out_specs=[pl.BlockSpec((tm, LANES), lambda i: (i, 0))] * 2,
        out_shape=[out, out],
        compiler_params=_params("parallel"),
        name="rope_tables",
    )(positions.reshape(t, 1), invf)


def _mla_proj_kernel(x_ref, cos_ref, sin_ref, w_in_ref, gq_ref, gkv_ref, wq1_ref, wq2_ref,
                     wk_ref, wvt_ref, q_ref, k_ref, vt_ref):
    xb = x_ref[...].astype(BF16)
    c = _dot(xb, w_in_ref[...])
    cq = _rms_norm(c[:, :MLA_Q_RANK], gq_ref[...]).astype(BF16)
    ckv = _rms_norm(c[:, MLA_Q_RANK:MLA_Q_RANK + MLA_KV_RANK], gkv_ref[...]).astype(BF16)
    cos = cos_ref[...]
    sin = sin_ref[...]
    base = MLA_Q_RANK + MLA_KV_RANK
    k_pe = (c[:, base:base + LANES] * cos + c[:, base + LANES:base + 2 * LANES] * sin).astype(BF16)
    scale = (MLA_NOPE + MLA_ROPE) ** -0.5 * LOG2_E
    cos_s = cos * scale
    sin_s = sin * scale
    q1 = _dot(cq, wq1_ref[...])
    q2 = _dot(cq, wq2_ref[...])
    kn = _dot(ckv, wk_ref[...])
    for h in range(MLA_HEADS):
        s0 = h * HEAD_SLOT
        q_ref[:, s0:s0 + LANES] = (q1[:, s0:s0 + LANES] * scale).astype(BF16)
        q_ref[:, s0 + LANES:s0 + HEAD_SLOT] = (
            q1[:, s0 + LANES:s0 + HEAD_SLOT] * cos_s
            + q2[:, h * LANES:(h + 1) * LANES] * sin_s).astype(BF16)
        k_ref[:, s0:s0 + LANES] = kn[:, h * LANES:(h + 1) * LANES].astype(BF16)
        k_ref[:, s0 + LANES:s0 + HEAD_SLOT] = k_pe
    vt_ref[...] = _dot_nt(wvt_ref[...], ckv).astype(BF16)


def _mla_proj(x, cos, sin, w, tm):
    t = x.shape[0]
    row = lambda n: pl.BlockSpec((tm, n), lambda i: (i, 0))
    full = lambda a: pl.BlockSpec(a.shape, lambda i: (0,) * a.ndim)
    consts = (w["w_in"], w["g_q"], w["g_kv"], w["w_q1"], w["w_q2"], w["w_k"], w["w_vt"])
    return pl.pallas_call(
        _mla_proj_kernel,
        grid=(t // tm,),
        in_specs=[row(D_MODEL), row(LANES), row(LANES)] + [full(a) for a in consts],
        out_specs=[row(MLA_HEADS * HEAD_SLOT), row(MLA_HEADS * HEAD_SLOT),
                   pl.BlockSpec((None, MLA_HEADS * MLA_V, tm), lambda i: (i, 0, 0))],
        out_shape=[jax.ShapeDtypeStruct((t, MLA_HEADS * HEAD_SLOT), BF16),
                   jax.ShapeDtypeStruct((t, MLA_HEADS * HEAD_SLOT), BF16),
                   jax.ShapeDtypeStruct((t // tm, MLA_HEADS * MLA_V, tm), BF16)],
        compiler_params=_params("parallel"),
        name="mla_proj",
    )(x, cos, sin, *consts)


def _attn_kernel(q_ref, k_ref, vt_ref, o_ref, s_ref, m_ref, l_ref, acc_ref, *, tile):
    qi = pl.program_id(2)
    m_ref[...] = jnp.full(m_ref.shape, -jnp.inf, F32)
    l_ref[...] = jnp.zeros(l_ref.shape, F32)
    acc_ref[...] = jnp.zeros(acc_ref.shape, F32)
    heads = range(ATTN_HEADS_PER_STEP)

    def scores(ki, buf):
        start = pl.multiple_of(ki * tile, tile)
        for g in heads:
            slot = slice(g * HEAD_SLOT, (g + 1) * HEAD_SLOT)
            s_ref[buf, g] = _dot_nt(k_ref[pl.ds(start, tile), slot], q_ref[:, slot])

    def absorb(ki, buf, masked):
        for g in heads:
            s = s_ref[buf, g]
            if masked:
                keys = lax.broadcasted_iota(jnp.int32, s.shape, 0)
                queries = lax.broadcasted_iota(jnp.int32, s.shape, 1)
                s = jnp.where(keys <= queries, s, -jnp.inf)
            m_old = m_ref[g]
            m_new = jnp.maximum(m_old, jnp.max(s, axis=0, keepdims=True))
            p = jnp.exp2(s - m_new)
            alpha = jnp.exp2(m_old - m_new)
            l_ref[g] = alpha * l_ref[g] + jnp.sum(p, axis=0, keepdims=True)
            acc_ref[g] = alpha * acc_ref[g] + _dot(vt_ref[ki, g * MLA_V:(g + 1) * MLA_V, :], p.astype(BF16))
            m_ref[g] = m_new

    scores(0, 0)

    def body(j, carry):
        scores(2 * j + 1, 1)
        absorb(2 * j, 0, False)
        scores(2 * j + 2, 0)
        absorb(2 * j + 1, 1, False)
        return carry

    lax.fori_loop(0, qi // 2, body, 0)

    @pl.when(qi % 2 == 1)
    def _():
        scores(qi, 1)
        absorb(qi - 1, 0, False)
        absorb(qi, 1, True)

    @pl.when(qi % 2 == 0)
    def _():
        absorb(qi, 0, True)

    for g in heads:
        o_ref[:, g * MLA_V:(g + 1) * MLA_V] = jnp.transpose(acc_ref[g] / l_ref[g]).astype(o_ref.dtype)


def _attention(q, k, vt, batch, seq, tile):
    t = q.shape[0]
    nq = seq // tile
    hps = ATTN_HEADS_PER_STEP
    return pl.pallas_call(
        functools.partial(_attn_kernel, tile=tile),
        grid=(batch, MLA_HEADS // hps, nq),
        in_specs=[pl.BlockSpec((tile, hps * HEAD_SLOT), lambda b, h, i: (b * nq + i, h)),
                  pl.BlockSpec((seq, hps * HEAD_SLOT), lambda b, h, i: (b, h)),
                  pl.BlockSpec((nq, hps * MLA_V, tile), lambda b, h, i: (b, h, 0))],
        out_specs=pl.BlockSpec((tile, hps * MLA_V), lambda b, h, i: (b * nq + i, h)),
        out_shape=jax.ShapeDtypeStruct((t, MLA_HEADS * MLA_V), BF16),
        scratch_shapes=[pltpu.VMEM((2, hps, tile, tile), F32),
                        pltpu.VMEM((hps, 1, tile), F32), pltpu.VMEM((hps, 1, tile), F32),
                        pltpu.VMEM((hps, MLA_V, tile), F32)],
        compiler_params=_params("parallel", "parallel", "arbitrary"),
        name="mla_attention",
    )(q, k, vt)


def _route_rows(sel):
    best = None
    for g in range(N_GROUPS):
        a = [sel[g * EXPERTS_PER_GROUP + j:g * EXPERTS_PER_GROUP + j + 1, :]
             for j in range(EXPERTS_PER_GROUP)]
        v1 = functools.reduce(jnp.maximum, a)
        i1 = jnp.full(v1.shape, EXPERTS_PER_GROUP - 1, jnp.int32)
        for j in reversed(range(EXPERTS_PER_GROUP - 1)):
            i1 = jnp.where(a[j] == v1, j, i1)
        r = [jnp.where(i1 == j, -jnp.inf, a[j]) for j in range(EXPERTS_PER_GROUP)]
        v2 = functools.reduce(jnp.maximum, r)
        i2 = jnp.full(v1.shape, EXPERTS_PER_GROUP - 1, jnp.int32)
        for j in reversed(range(EXPERTS_PER_GROUP - 1)):
            i2 = jnp.where(r[j] == v2, j, i2)
        total = v1 + v2
        if best is None:
            best = (total, jnp.zeros_like(i1), i1, i2)
        else:
            better = total > best[0]
            best = (jnp.where(better, total, best[0]), jnp.where(better, g, best[1]),
                    jnp.where(better, i1, best[2]), jnp.where(better, i2, best[3]))
    _, grp, i1, i2 = best
    lo = jnp.minimum(i1, i2)
    hi = jnp.maximum(i1, i2)
    pair = jnp.where(lo == 0, hi - 1, jnp.where(lo == 1, hi + 1, N_PAIRS - 1))
    return grp * N_PAIRS + pair


def _post_kernel(a_ref, w_ref, x_ref, g_ref, b_ref, rwt_ref, rb_ref, y_ref, cls_ref):
    subs = [slice(r, r + POST_SUB) for r in range(0, a_ref.shape[0], POST_SUB)]
    y = _dot(a_ref[subs[0], :], w_ref[...])
    for n, rows in enumerate(subs):
        y_next = _dot(a_ref[subs[n + 1], :], w_ref[...]) if n + 1 < len(subs) else None
        x1 = _layer_norm(DN_ALPHA * x_ref[rows, :] + y, g_ref[...], b_ref[...])
        y_ref[rows, :] = x1
        logits = _dot_nt(rwt_ref[...], x1.astype(BF16))
        cls_ref[:, rows] = _route_rows(_sigmoid(logits) + rb_ref[...])
        y = y_next


def _post(a, w, x, g, b, rwt, rb, tm):
    t, k = a.shape
    row = lambda n: pl.BlockSpec((tm, n), lambda i: (i, 0))
    full = lambda arr: pl.BlockSpec(arr.shape, lambda i: (0,) * arr.ndim)
    return pl.pallas_call(
        _post_kernel,
        grid=(t // tm,),
        in_specs=[row(k), full(w), row(D_MODEL), full(g), full(b), full(rwt), full(rb)],
        out_specs=[row(D_MODEL), pl.BlockSpec((1, tm), lambda i: (0, i))],
        out_shape=[jax.ShapeDtypeStruct((t, D_MODEL), F32),
                   jax.ShapeDtypeStruct((1, t), jnp.int32)],
        compiler_params=_params("parallel"),
        name="mixer_out_norm_route",
    )(a, w, x, g, b, rwt, rb)


def _expert_kernel(ea_ref, eb_ref, fresh_ref, nblk_ref, x_ref, rw_ref, wga_ref, wua_ref, wda_ref,
                   wgb_ref, wub_ref, wdb_ref, g_ref, b_ref, prev_ref, o_ref, wga, wua, wda, wgb, wub, wdb):
    del prev_ref
    i = pl.program_id(0)

    @pl.when(fresh_ref[i] == 1)
    def _():
        for src, dst in ((wga_ref, wga), (wua_ref, wua), (wda_ref, wda),
                         (wgb_ref, wgb), (wub_ref, wub), (wdb_ref, wdb)):
            dst[...] = src[...].astype(BF16)

    @pl.when(i < nblk_ref[0])
    def _():
        x = x_ref[...]
        xb = x.astype(BF16)
        sig = _sigmoid(_dot(xb, rw_ref[...]))
        lane = lax.broadcasted_iota(jnp.int32, sig.shape, 1)
        sa = jnp.sum(jnp.where(lane == ea_ref[i], sig, 0.0), axis=-1, keepdims=True)
        sb = jnp.sum(jnp.where(lane == eb_ref[i], sig, 0.0), axis=-1, keepdims=True)
        tot = sa + sb
        ha = _silu(_dot(xb, wga[...])) * _dot(xb, wua[...])
        hb = _silu(_dot(xb, wgb[...])) * _dot(xb, wub[...])
        y = (sa / tot) * _dot(ha.astype(BF16), wda[...]) + (sb / tot) * _dot(hb.astype(BF16), wdb[...])
        o_ref[...] = _layer_norm(DN_ALPHA * x + y, g_ref[...], b_ref[...])

    @pl.when(i >= nblk_ref[0])
    def _():
        o_ref[...] = jnp.zeros(o_ref.shape, o_ref.dtype)


def _experts(xs, ea, eb, fresh, nblk, rw, layer, wg, wu, wd, g, b, bm, blk0, total_rows, prev):
    p = xs.shape[0]
    sel = lambda which: (lambda i, ea, eb, fr, nb: (layer, (ea if which == 0 else eb)[i], 0, 0))
    up = lambda which: pl.BlockSpec((None, None, D_MODEL, EXPERT_FF), sel(which))
    down = lambda which: pl.BlockSpec((None, None, EXPERT_FF, D_MODEL), sel(which))
    full = lambda arr: pl.BlockSpec(arr.shape, lambda i, ea, eb, fr, nb: (0,) * arr.ndim)
    row = pl.BlockSpec((bm, D_MODEL), lambda i, ea, eb, fr, nb: (i, 0))
    out_row = pl.BlockSpec((bm, D_MODEL), lambda i, ea, eb, fr, nb: (i + blk0, 0))
    up_s = pltpu.VMEM((D_MODEL, EXPERT_FF), BF16)
    down_s = pltpu.VMEM((EXPERT_FF, D_MODEL), BF16)
    n_prefetch = 4
    if prev is None:
        prev = jnp.zeros((8, LANES), F32)
        aliases = {}
    else:
        aliases = {n_prefetch + 10: 0}
    return pl.pallas_call(
        _expert_kernel,
        grid_spec=pltpu.PrefetchScalarGridSpec(
            num_scalar_prefetch=n_prefetch,
            grid=(p // bm,),
            in_specs=[row, full(rw), up(0), up(0), down(0), up(1), up(1), down(1), full(g), full(b),
                      pl.BlockSpec(memory_space=pl.ANY)],
            out_specs=out_row,
            scratch_shapes=[up_s, up_s, down_s, up_s, up_s, down_s]),
        out_shape=jax.ShapeDtypeStruct((total_rows, D_MODEL), F32),
        input_output_aliases=aliases,
        compiler_params=_params("arbitrary"),
        name="moe_experts",
    )(ea, eb, fresh, nblk, xs, rw, wg, wu, wd, wg, wu, wd, g, b, prev)


def _row_gather(src, idx):
    m = idx.shape[0]
    d = src.shape[1]
    per = m // SC_WORKERS
    assert per * SC_WORKERS == m
    sc_rows = next(r for r in range(SC_MAX_ROWS, 0, -8) if per % r == 0 and (per // r) % 2 == 0)
    n = per // sc_rows
    mesh = plsc.VectorSubcoreMesh(core_axis_name="core", subcore_axis_name="subcore")

    def body(src_hbm, idx_hbm, out_hbm, idx_v, buf, gsem, psem):
        worker = lax.axis_index("core") * (SC_WORKERS // 2) + lax.axis_index("subcore")
        base = worker * per
        pltpu.sync_copy(idx_hbm.at[pl.ds(base, per)], idx_v)

        def gather(c, slot):
            return pltpu.make_async_copy(src_hbm.at[idx_v.at[pl.ds(c * sc_rows, sc_rows)]],
                                         buf.at[slot], gsem.at[slot])

        def put(c, slot):
            return pltpu.make_async_copy(buf.at[slot], out_hbm.at[pl.ds(base + c * sc_rows, sc_rows)],
                                         psem.at[slot])

        gather(0, 0).start()

        @pl.loop(0, n // 2)
        def _(j):
            for slot in range(2):
                c = 2 * j + slot
                gather(c, slot).wait()
                put(c, slot).start()

                @pl.when(c >= 1)
                def _():
                    put(c - 1, 1 - slot).wait()

                @pl.when(c + 1 < n)
                def _():
                    gather(c + 1, 1 - slot).start()

        put(n - 1, 1).wait()

    return pl.kernel(
        body,
        out_type=jax.ShapeDtypeStruct((m, d), src.dtype),
        mesh=mesh,
        scratch_types=[pltpu.VMEM((per,), jnp.int32), pltpu.VMEM((2, sc_rows, d), src.dtype),
                       pltpu.SemaphoreType.DMA((2,)), pltpu.SemaphoreType.DMA((2,))],
    )(src, idx)


def _moe(x1, cls, rw, layer, wg, wu, wd, g, b, bm):
    t = x1.shape[0]
    p_rows = t + N_CLASSES * bm
    nblk = p_rows // bm
    classes = jnp.arange(N_CLASSES, dtype=jnp.int32)
    counts = jnp.sum((cls[:, None] == classes[None, :]).astype(jnp.int32), axis=0)
    padded = ((counts + bm - 1) // bm) * bm
    off = jnp.cumsum(counts) - counts
    pend = jnp.cumsum(padded)
    poff = pend - padded
    tok = jnp.arange(t, dtype=jnp.int32)
    cls_sorted, order = lax.sort((cls, tok), num_keys=1)
    shift = jnp.sum(jnp.where(cls_sorted[:, None] == classes[None, :], (poff - off)[None, :], 0), axis=1)
    _, dest = lax.sort((order, shift + tok), num_keys=1)
    blk_start = jnp.arange(nblk, dtype=jnp.int32) * bm
    blk_cls = jnp.minimum(jnp.sum((blk_start[:, None] >= pend[None, :]).astype(jnp.int32), axis=1),
                          N_CLASSES - 1)
    n_used = (pend[-1] // bm).astype(jnp.int32).reshape(1)
    j = jnp.arange(bm, dtype=jnp.int32)[None, :] + (blk_start - poff[blk_cls])[:, None]
    rank = off[blk_cls][:, None] + jnp.minimum(j, counts[blk_cls][:, None] - 1)
    src = order[jnp.clip(rank, 0, t - 1).reshape(p_rows)]
    grp = blk_cls // N_PAIRS
    pair = blk_cls % N_PAIRS
    lo = jnp.where(pair < 3, 0, jnp.where(pair < 5, 1, 2))
    hi = jnp.where(pair < 3, pair + 1, jnp.where(pair < 5, pair - 1, 3))
    ea = (grp * EXPERTS_PER_GROUP + lo).astype(jnp.int32)
    eb = (grp * EXPERTS_PER_GROUP + hi).astype(jnp.int32)
    fresh = jnp.concatenate([jnp.ones((1,), jnp.int32), (blk_cls[1:] != blk_cls[:-1]).astype(jnp.int32)])
    nb = nblk // MOE_SPLIT
    xs = [_row_gather(x1, src[k * nb * bm:(k + 1) * nb * bm]) for k in range(MOE_SPLIT)]
    ys = None
    for k in range(MOE_SPLIT):
        part = slice(k * nb, (k + 1) * nb)
        fresh_k = jnp.concatenate([jnp.ones((1,), jnp.int32), fresh[part][1:]])
        ys = _experts(xs[k], ea[part], eb[part], fresh_k, n_used - k * nb, rw, layer, wg, wu, wd, g, b, bm,
                      k * nb, p_rows, ys)
    return _row_gather(ys, dest)


def _mlstm_pre_kernel(x_ref, w_in_ref, cw_ref, cb_ref, bdqk_ref, bdv_ref, wgate_ref, bgate_ref,
                      q_ref, k_ref, v_ref, xc_ref, sz_ref, gate_ref, ext_ref, *, tm):
    halo = 8
    xb = x_ref[...].astype(BF16)

    @pl.when(pl.program_id(1) == 0)
    def _():
        ext_ref[0:halo, :] = jnp.zeros((halo, MLSTM_INNER), F32)

    @pl.when(pl.program_id(1) != 0)
    def _():
        ext_ref[0:halo, :] = ext_ref[tm:tm + halo, :]

    xm = _dot(xb, w_in_ref[:, :MLSTM_INNER])
    ext_ref[halo:halo + tm, :] = xm
    conv = cb_ref[...] + cw_ref[MLSTM_CONV - 1:MLSTM_CONV, :] * xm
    for kk in range(MLSTM_CONV - 1):
        lag = MLSTM_CONV - 1 - kk
        conv = conv + cw_ref[kk:kk + 1, :] * ext_ref[halo - lag:halo - lag + tm, :]
    xc = _silu(conv)
    xc_b = xc.astype(BF16)
    xm_b = xm.astype(BF16)
    xc_ref[...] = xc_b
    sz_ref[...] = _silu(_dot(xb, w_in_ref[:, MLSTM_INNER:])).astype(BF16)

    gates = bgate_ref[...]
    k_scale = MLSTM_HD ** -0.5
    for c in range(MLSTM_INNER // MXU_DIM):
        lo, hi = c * MXU_DIM, (c + 1) * MXU_DIM
        qk = _dot(xc_b[:, lo:hi], bdqk_ref[c])
        vv = _dot(xm_b[:, lo:hi], bdv_ref[c])
        qb = qk[:, :MXU_DIM].astype(BF16)
        kb = qk[:, MXU_DIM:].astype(BF16)
        vb = vv.astype(BF16)
        q_ref[:, lo:hi] = qb
        k_ref[:, lo:hi] = (qk[:, MXU_DIM:] * k_scale).astype(BF16)
        v_ref[:, lo:hi] = vb
        gates = gates + _dot(qb, wgate_ref[lo:hi, :]) \
            + _dot(kb, wgate_ref[MLSTM_INNER + lo:MLSTM_INNER + hi, :]) \
            + _dot(vb, wgate_ref[2 * MLSTM_INNER + lo:2 * MLSTM_INNER + hi, :])
    fg = gates[:, LANES:]
    gate_ref[:, :LANES] = gates[:, :LANES]
    gate_ref[:, LANES:] = jnp.minimum(fg, 0.0) - jnp.log(1.0 + jnp.exp(-jnp.abs(fg)))


def _mlstm_pre(x, w, batch, seq, tm):
    t = x.shape[0]
    ns = seq // tm
    row = lambda n: pl.BlockSpec((tm, n), lambda b, s: (b * ns + s, 0))
    full = lambda a: pl.BlockSpec(a.shape, lambda b, s: (0,) * a.ndim)
    consts = (w["w_in"], w["conv_w"], w["conv_b"], w["bd_qk"], w["bd_v"], w["w_gate"], w["b_gate"])
    act = jax.ShapeDtypeStruct((t, MLSTM_INNER), BF16)
    return pl.pallas_call(
        functools.partial(_mlstm_pre_kernel, tm=tm),
        grid=(batch, ns),
        in_specs=[row(D_MODEL)] + [full(a) for a in consts],
        out_specs=[row(MLSTM_INNER)] * 5 + [row(2 * LANES)],
        out_shape=[act] * 5 + [jax.ShapeDtypeStruct((t, 2 * LANES), F32)],
        scratch_shapes=[pltpu.VMEM((tm + 8, MLSTM_INNER), F32)],
        compiler_params=_params("parallel", "arbitrary"),
        name="mlstm_pre",
    )(x, *consts)


def _split3(v):
    a = v.astype(BF16)
    r = v - a.astype(F32)
    b = r.astype(BF16)
    c = (r - b.astype(F32)).astype(BF16)
    return a, b, c


def _mlstm_chunk_kernel(q_ref, k_ref, v_ref, gate_ref, xc_ref, sz_ref, gout_ref, skip_ref, o_ref,
                        c_ref, n_ref, m_ref, *, chunk):
    @pl.when(pl.program_id(1) == 0)
    def _():
        c_ref[...] = jnp.zeros(c_ref.shape, F32)
        n_ref[...] = jnp.zeros(n_ref.shape, F32)
        m_ref[...] = jnp.zeros(m_ref.shape, F32)

    ig_all = gate_ref[:, :LANES]
    lf_all = gate_ref[:, LANES:]
    rows = lax.broadcasted_iota(jnp.int32, (chunk, chunk), 0)
    cols = lax.broadcasted_iota(jnp.int32, (chunk, chunk), 1)
    causal = cols <= rows
    tril = jnp.where(causal, 1.0, 0.0).astype(BF16)
    b_all = functools.reduce(lambda u, w: u + w, [_dot(tril, piece) for piece in _split3(lf_all)])
    row_all = jnp.transpose(ig_all - b_all)

    for h in range(MLSTM_HEADS):
        lo, hi = h * MLSTM_HD, (h + 1) * MLSTM_HD
        q = q_ref[:, lo:hi]
        k = k_ref[:, lo:hi]
        v = v_ref[:, lo:hi]
        b = b_all[:, h:h + 1]
        ig = ig_all[:, h:h + 1]
        m_prev = m_ref[h]
        d = jnp.where(causal, b + row_all[h:h + 1, :], -jnp.inf)
        inter = b + m_prev
        m_i = jnp.maximum(inter, jnp.max(d, axis=-1, keepdims=True))
        w_intra = jnp.exp(d - m_i)
        w_inter = jnp.exp(inter - m_i)
        s = _dot_nt(q, k) * w_intra
        c_old = c_ref[h]
        n_old = n_ref[h]
        num = _dot(s.astype(BF16), v) + w_inter * _dot(q, c_old.astype(BF16))
        qn = jnp.sum(q.astype(F32) * n_old, axis=-1, keepdims=True)
        den = jnp.sum(s, axis=-1, keepdims=True) + w_inter * qn
        hh = num / jnp.maximum(jnp.abs(den), jnp.exp(-m_i))

        b_last = b[chunk - 1:chunk, :]
        gg = b_last - b + ig
        m_new = jnp.maximum(b_last + m_prev, jnp.max(gg, axis=0, keepdims=True))
        kw = k.astype(F32) * jnp.exp(gg - m_new)
        decay = jnp.exp(b_last + m_prev - m_new)
        c_ref[h] = decay * c_old + _dot_tn(kw.astype(BF16), v)
        n_ref[h] = decay * n_old + jnp.sum(kw, axis=0, keepdims=True)
        m_ref[h] = m_new

        mu = jnp.mean(hh, axis=-1, keepdims=True)
        cen = hh - mu
        var = jnp.mean(cen * cen, axis=-1, keepdims=True)
        hn = cen * lax.rsqrt(var + LN_EPS) * gout_ref[:, lo:hi]
        o_ref[:, lo:hi] = ((hn + skip_ref[:, lo:hi] * xc_ref[:, lo:hi].astype(F32))
                           * sz_ref[:, lo:hi].astype(F32)).astype(o_ref.dtype)


def _mlstm_chunks(q, k, v, gates, xc, sz, g_out, skip, batch, seq, chunk):
    t = q.shape[0]
    nc = seq // chunk
    row = lambda n: pl.BlockSpec((chunk, n), lambda b, c: (b * nc + c, 0))
    full = lambda a: pl.BlockSpec(a.shape, lambda b, c: (0,) * a.ndim)
    return pl.pallas_call(
        functools.partial(_mlstm_chunk_kernel, chunk=chunk),
        grid=(batch, nc),
        in_specs=[row(MLSTM_INNER)] * 3 + [row(2 * LANES)] + [row(MLSTM_INNER)] * 2
                 + [full(g_out), full(skip)],
        out_specs=row(MLSTM_INNER),
        out_shape=jax.ShapeDtypeStruct((t, MLSTM_INNER), BF16),
        scratch_shapes=[pltpu.VMEM((MLSTM_HEADS, MLSTM_HD, MLSTM_HD), F32),
                        pltpu.VMEM((MLSTM_HEADS, 1, MLSTM_HD), F32),
                        pltpu.VMEM((MLSTM_HEADS, 1, 1), F32)],
        compiler_params=_params("parallel", "arbitrary"),
        name="mlstm_chunks",
    )(q, k, v, gates, xc, sz, g_out, skip)


def _rot_half(w):
    half = MLA_ROPE // 2
    return jnp.concatenate([-w[..., half:], w[..., :half]], axis=-1)


def _prep_mla(w_in, g_q, w_uq, g_kv, w_ukv):
    pad = jnp.zeros((D_MODEL, LANES - MLA_ROPE), F32)
    w_pe = w_in[:, MLA_Q_RANK + MLA_KV_RANK:]
    w_in_p = jnp.concatenate([w_in[:, :MLA_Q_RANK + MLA_KV_RANK], w_pe, pad, _rot_half(w_pe), pad], axis=1)
    uq = w_uq.reshape(MLA_Q_RANK, MLA_HEADS, MLA_NOPE + MLA_ROPE)
    zq = jnp.zeros((MLA_Q_RANK, MLA_HEADS, LANES - MLA_ROPE), F32)
    w_q1 = jnp.concatenate([uq, zq], axis=-1).reshape(MLA_Q_RANK, MLA_HEADS * HEAD_SLOT)
    w_q2 = jnp.concatenate([_rot_half(uq[..., MLA_NOPE:]), zq], axis=-1).reshape(MLA_Q_RANK, MLA_HEADS * LANES)
    ukv = w_ukv.reshape(MLA_KV_RANK, MLA_HEADS, MLA_NOPE + MLA_V)
    w_k = ukv[..., :MLA_NOPE].reshape(MLA_KV_RANK, -1)
    w_vt = ukv[..., MLA_NOPE:].reshape(MLA_KV_RANK, -1).T
    return dict(w_in=w_in_p.astype(BF16), g_q=g_q.reshape(1, -1), g_kv=g_kv.reshape(1, -1),
                w_q1=w_q1.astype(BF16), w_q2=w_q2.astype(BF16), w_k=w_k.astype(BF16),
                w_vt=w_vt.astype(BF16))


def _block_diag_tiles(w):
    per = MXU_DIM // MLSTM_QKV_BLOCK
    w = w.reshape(MLSTM_INNER // MXU_DIM, per, MLSTM_QKV_BLOCK, MLSTM_QKV_BLOCK)
    eye = jnp.eye(per, dtype=w.dtype)
    return jnp.einsum("cgij,gh->cgihj", w, eye).reshape(MLSTM_INNER // MXU_DIM, MXU_DIM, MXU_DIM)


def _prep_mlstm(w_in, conv_w, conv_b, w_q, w_k, w_v, w_ig, b_ig, w_fg, b_fg):
    bd_qk = jnp.concatenate([_block_diag_tiles(w_q), _block_diag_tiles(w_k)], axis=-1)
    zw = jnp.zeros((3 * MLSTM_INNER, LANES - MLSTM_HEADS), F32)
    zb = jnp.zeros((LANES - MLSTM_HEADS,), F32)
    w_gate = jnp.concatenate([w_ig, zw, w_fg, zw], axis=1)
    b_gate = jnp.concatenate([b_ig, zb, b_fg, zb]).reshape(1, 2 * LANES)
    return dict(w_in=w_in.astype(BF16), conv_w=conv_w, conv_b=conv_b.reshape(1, -1),
                bd_qk=bd_qk.astype(BF16), bd_v=_block_diag_tiles(w_v).astype(BF16),
                w_gate=w_gate.astype(BF16), b_gate=b_gate)


def kernel(x, positions, router_w, router_b, mla_w_in, mla_g_q, mla_w_uq, mla_g_kv, mla_w_ukv, mla_w_o, mlstm_w_in, mlstm_conv_w, mlstm_conv_b, mlstm_w_q, mlstm_w_k, mlstm_w_v, mlstm_w_ig, mlstm_b_ig, mlstm_w_fg, mlstm_b_fg, mlstm_g_out, mlstm_skip, mlstm_w_out, moe_w_gate, moe_w_up, moe_w_down, ln_g, ln_b):
    batch, seq, _ = x.shape
    tiles = _tiles(seq)
    rwt = router_w.T.astype(BF16)
    rb = router_b.reshape(N_EXPERTS, 1)
    rw_pad = jnp.concatenate([router_w, jnp.zeros((D_MODEL, LANES - N_EXPERTS), F32)], axis=1).astype(BF16)
    t = batch * seq
    cos, sin = _rope_tables(positions, tiles["rope"])
    xt = x.reshape(t, D_MODEL)
    for i in range(DEPTH):
        j = i // N_MIXERS
        g0, b0 = ln_g[i, 0].reshape(1, -1), ln_b[i, 0].reshape(1, -1)
        g1, b1 = ln_g[i, 1].reshape(1, -1), ln_b[i, 1].reshape(1, -1)
        if i % N_MIXERS == 0:
            w = _prep_mla(mla_w_in[j], mla_g_q[j], mla_w_uq[j], mla_g_kv[j], mla_w_ukv[j])
            w_out = mla_w_o[j].astype(BF16)
        else:
            w = _prep_mlstm(mlstm_w_in[j], mlstm_conv_w[j], mlstm_conv_b[j], mlstm_w_q[j], mlstm_w_k[j],
                            mlstm_w_v[j], mlstm_w_ig[j], mlstm_b_ig[j], mlstm_w_fg[j], mlstm_b_fg[j])
            w_out = mlstm_w_out[j].astype(BF16)
        if i % N_MIXERS == 0:
            q, k, vt = _mla_proj(xt, cos, sin, w, tiles["attn"])
            a = _attention(q, k, vt, batch, seq, tiles["attn"])
        else:
            q, k, v, xc, sz, gates = _mlstm_pre(xt, w, batch, seq, tiles["pre"])
            a = _mlstm_chunks(q, k, v, gates, xc, sz, mlstm_g_out[j].reshape(1, -1),
                              mlstm_skip[j].reshape(1, -1), batch, seq, tiles["chunk"])
        x1, cls = _post(a, w_out, xt, g0, b0, rwt, rb, tiles["post"])
        xt = _moe(x1, cls.reshape(t), rw_pad, i, moe_w_gate, moe_w_up, moe_w_down, g1, b1, tiles["moe"])
    return xt.reshape(batch, seq, D_MODEL)
```

```python
import functools

import jax
import jax.numpy as jnp
from jax import lax
from jax.experimental import pallas as pl
from jax.experimental.pallas import tpu as pltpu
from jax.experimental.pallas import tpu_sc as plsc

D_MODEL = 1024
DEPTH = 4
N_MIXERS = 2
MLA_HEADS = 8
MLA_Q_RANK = 256
MLA_KV_RANK = 256
MLA_NOPE = 128
MLA_ROPE = 64
MLA_V = 128
ROPE_THETA = 10000.0
MLSTM_INNER = 2 * D_MODEL
MLSTM_HEADS = 4
MLSTM_HD = MLSTM_INNER // MLSTM_HEADS
MLSTM_CONV = 4
MLSTM_QKV_BLOCK = 4
N_EXPERTS = 16
N_GROUPS = 4
EXPERTS_PER_GROUP = N_EXPERTS // N_GROUPS
EXPERT_FF = 512
DN_ALPHA = (2.0 * DEPTH) ** 0.25
LN_EPS = 1e-5
LOG2_E = 1.4426950408889634

LANES = 128
MXU_DIM = 256
HEAD_SLOT = 2 * LANES
ATTN_HEADS_PER_STEP = 4
N_PAIRS = EXPERTS_PER_GROUP * (EXPERTS_PER_GROUP - 1) // 2
N_CLASSES = N_GROUPS * N_PAIRS
SC_WORKERS = 32
SC_MAX_ROWS = 48
MOE_SPLIT = 4
POST_SUB = 512
VMEM_LIMIT = 56 * 1024 * 1024

F32 = jnp.float32
BF16 = jnp.bfloat16


def _tiles(seq):
    return dict(
        rope=min(seq, 2048),
        attn=min(seq, 512),
        post=min(seq, 1024),
        pre=min(seq, 256),
        chunk=min(seq, 256),
        moe=256,
    )


def _params(*sem):
    return pltpu.CompilerParams(dimension_semantics=sem, vmem_limit_bytes=VMEM_LIMIT)


def _dot(a, b):
    return jnp.dot(a, b, preferred_element_type=F32)


def _dot_nt(a, b):
    return lax.dot_general(a, b, (((1,), (1,)), ((), ())), preferred_element_type=F32)


def _dot_tn(a, b):
    return lax.dot_general(a, b, (((0,), (0,)), ((), ())), preferred_element_type=F32)


def _layer_norm(v, g, b):
    mu = jnp.mean(v, axis=-1, keepdims=True)
    c = v - mu
    var = jnp.mean(c * c, axis=-1, keepdims=True)
    return c * lax.rsqrt(var + LN_EPS) * g + b


def _rms_norm(v, g):
    return v * lax.rsqrt(jnp.mean(v * v, axis=-1, keepdims=True) + LN_EPS) * g


def _sigmoid(v):
    return 1.0 / (1.0 + jnp.exp(-v))


def _silu(v):
    return v * _sigmoid(v)


def _rope_kernel(pos_ref, invf_ref, cos_ref, sin_ref):
    ang = pos_ref[...].astype(F32) * invf_ref[...]
    cos_ref[...] = jnp.cos(ang)
    sin_ref[...] = jnp.sin(ang)


def _rope_tables(positions, tm):
    t = positions.size
    inv_freq = 1.0 / (ROPE_THETA ** (jnp.arange(0, MLA_ROPE, 2, dtype=F32) / MLA_ROPE))
    invf = jnp.tile(inv_freq, LANES // (MLA_ROPE // 2)).reshape(1, LANES)
    out = jax.ShapeDtypeStruct((t, LANES), F32)
    return pl.pallas_call(
        _rope_kernel,
        grid=(t // tm,),
        in_specs=[pl.BlockSpec((tm, 1), lambda i: (i, 0)),
                  pl.BlockSpec((1, LANES), lambda i: (0, 0))],
        out_specs=[pl.BlockSpec((tm, LANES), lambda i: (i, 0))] * 2,
        out_shape=[out, out],
        compiler_params=_params("parallel"),
        name="rope_tables",
    )(positions.reshape(t, 1), invf)


def _mla_proj_kernel(x_ref, cos_ref, sin_ref, w_in_ref, gq_ref, gkv_ref, wq1_ref, wq2_ref,
                     wk_ref, wvt_ref, q_ref, k_ref, vt_ref):
    xb = x_ref[...].astype(BF16)
    c = _dot(xb, w_in_ref[...])
    cq = _rms_norm(c[:, :MLA_Q_RANK], gq_ref[...]).astype(BF16)
    ckv = _rms_norm(c[:, MLA_Q_RANK:MLA_Q_RANK + MLA_KV_RANK], gkv_ref[...]).astype(BF16)
    cos = cos_ref[...]
    sin = sin_ref[...]
    base = MLA_Q_RANK + MLA_KV_RANK
    k_pe = (c[:, base:base + LANES] * cos + c[:, base + LANES:base + 2 * LANES] * sin).astype(BF16)
    scale = (MLA_NOPE + MLA_ROPE) ** -0.5 * LOG2_E
    cos_s = cos * scale
    sin_s = sin * scale
    q1 = _dot(cq, wq1_ref[...])
    q2 = _dot(cq, wq2_ref[...])
    kn = _dot(ckv, wk_ref[...])
    for h in range(MLA_HEADS):
        s0 = h * HEAD_SLOT
        q_ref[:, s0:s0 + LANES] = (q1[:, s0:s0 + LANES] * scale).astype(BF16)
        q_ref[:, s0 + LANES:s0 + HEAD_SLOT] = (
            q1[:, s0 + LANES:s0 + HEAD_SLOT] * cos_s
            + q2[:, h * LANES:(h + 1) * LANES] * sin_s).astype(BF16)
        k_ref[:, s0:s0 + LANES] = kn[:, h * LANES:(h + 1) * LANES].astype(BF16)
        k_ref[:, s0 + LANES:s0 + HEAD_SLOT] = k_pe
    vt_ref[...] = _dot_nt(wvt_ref[...], ckv).astype(BF16)


def _mla_proj(x, cos, sin, w, tm):
    t = x.shape[0]
    row = lambda n: pl.BlockSpec((tm, n), lambda i: (i, 0))
    full = lambda a: pl.BlockSpec(a.shape, lambda i: (0,) * a.ndim)
    consts = (w["w_in"], w["g_q"], w["g_kv"], w["w_q1"], w["w_q2"], w["w_k"], w["w_vt"])
    return pl.pallas_call(
        _mla_proj_kernel,
        grid=(t // tm,),
        in_specs=[row(D_MODEL), row(LANES), row(LANES)] + [full(a) for a in consts],
        out_specs=[row(MLA_HEADS * HEAD_SLOT), row(MLA_HEADS * HEAD_SLOT),
                   pl.BlockSpec((None, MLA_HEADS * MLA_V, tm), lambda i: (i, 0, 0))],
        out_shape=[jax.ShapeDtypeStruct((t, MLA_HEADS * HEAD_SLOT), BF16),
                   jax.ShapeDtypeStruct((t, MLA_HEADS * HEAD_SLOT), BF16),
                   jax.ShapeDtypeStruct((t // tm, MLA_HEADS * MLA_V, tm), BF16)],
        compiler_params=_params("parallel"),
        name="mla_proj",
    )(x, cos, sin, *consts)


def _attn_kernel(q_ref, k_ref, vt_ref, o_ref, s_ref, m_ref, l_ref, acc_ref, *, tile):
    qi = pl.program_id(2)
    m_ref[...] = jnp.full(m_ref.shape, -jnp.inf, F32)
    l_ref[...] = jnp.zeros(l_ref.shape, F32)
    acc_ref[...] = jnp.zeros(acc_ref.shape, F32)
    heads = range(ATTN_HEADS_PER_STEP)

    def scores(ki, buf):
        start = pl.multiple_of(ki * tile, tile)
        for g in heads:
            slot = slice(g * HEAD_SLOT, (g + 1) * HEAD_SLOT)
            s_ref[buf, g] = _dot_nt(k_ref[pl.ds(start, tile), slot], q_ref[:, slot])

    def absorb(ki, buf, masked):
        for g in heads:
            s = s_ref[buf, g]
            if masked:
                keys = lax.broadcasted_iota(jnp.int32, s.shape, 0)
                queries = lax.broadcasted_iota(jnp.int32, s.shape, 1)
                s = jnp.where(keys <= queries, s, -jnp.inf)
            m_old = m_ref[g]
            m_new = jnp.maximum(m_old, jnp.max(s, axis=0, keepdims=True))
            p = jnp.exp2(s - m_new)
            alpha = jnp.exp2(m_old - m_new)
            l_ref[g] = alpha * l_ref[g] + jnp.sum(p, axis=0, keepdims=True)
            acc_ref[g] = alpha * acc_ref[g] + _dot(vt_ref[ki, g * MLA_V:(g + 1) * MLA_V, :], p.astype(BF16))
            m_ref[g] = m_new

    scores(0, 0)

    def body(j, carry):
        scores(2 * j + 1, 1)
        absorb(2 * j, 0, False)
        scores(2 * j + 2, 0)
        absorb(2 * j + 1, 1, False)
        return carry

    lax.fori_loop(0, qi // 2, body, 0)

    @pl.when(qi % 2 == 1)
    def _():
        scores(qi, 1)
        absorb(qi - 1, 0, False)
        absorb(qi, 1, True)

    @pl.when(qi % 2 == 0)
    def _():
        absorb(qi, 0, True)

    for g in heads:
        o_ref[:, g * MLA_V:(g + 1) * MLA_V] = jnp.transpose(acc_ref[g] / l_ref[g]).astype(o_ref.dtype)


def _attention(q, k, vt, batch, seq, tile):
    t = q.shape[0]
    nq = seq // tile
    hps = ATTN_HEADS_PER_STEP
    return pl.pallas_call(
        functools.partial(_attn_kernel, tile=tile),
        grid=(batch, MLA_HEADS // hps, nq),
        in_specs=[pl.BlockSpec((tile, hps * HEAD_SLOT), lambda b, h, i: (b * nq + i, h)),
                  pl.BlockSpec((seq, hps * HEAD_SLOT), lambda b, h, i: (b, h)),
                  pl.BlockSpec((nq, hps * MLA_V, tile), lambda b, h, i: (b, h, 0))],
        out_specs=pl.BlockSpec((tile, hps * MLA_V), lambda b, h, i: (b * nq + i, h)),
        out_shape=jax.ShapeDtypeStruct((t, MLA_HEADS * MLA_V), BF16),
        scratch_shapes=[pltpu.VMEM((2, hps, tile, tile), F32),
                        pltpu.VMEM((hps, 1, tile), F32), pltpu.VMEM((hps, 1, tile), F32),
                        pltpu.VMEM((hps, MLA_V, tile), F32)],
        compiler_params=_params("parallel", "parallel", "arbitrary"),
        name="mla_attention",
    )(q, k, vt)


def _route_rows(sel):
    best = None
    for g in range(N_GROUPS):
        a = [sel[g * EXPERTS_PER_GROUP + j:g * EXPERTS_PER_GROUP + j + 1, :]
             for j in range(EXPERTS_PER_GROUP)]
        v1 = functools.reduce(jnp.maximum, a)
        i1 = jnp.full(v1.shape, EXPERTS_PER_GROUP - 1, jnp.int32)
        for j in reversed(range(EXPERTS_PER_GROUP - 1)):
            i1 = jnp.where(a[j] == v1, j, i1)
        r = [jnp.where(i1 == j, -jnp.inf, a[j]) for j in range(EXPERTS_PER_GROUP)]
        v2 = functools.reduce(jnp.maximum, r)
        i2 = jnp.full(v1.shape, EXPERTS_PER_GROUP - 1, jnp.int32)
        for j in reversed(range(EXPERTS_PER_GROUP - 1)):
            i2 = jnp.where(r[j] == v2, j, i2)
        total = v1 + v2
        if best is None:
            best = (total, jnp.zeros_like(i1), i1, i2)
        else:
            better = total > best[0]
            best = (jnp.where(better, total, best[0]), jnp.where(better, g, best[1]),
                    jnp.where(better, i1, best[2]), jnp.where(better, i2, best[3]))
    _, grp, i1, i2 = best
    lo = jnp.minimum(i1, i2)
    hi = jnp.maximum(i1, i2)
    pair = jnp.where(lo == 0, hi - 1, jnp.where(lo == 1, hi + 1, N_PAIRS - 1))
    return grp * N_PAIRS + pair


def _post_kernel(a_ref, w_ref, x_ref, g_ref, b_ref, rwt_ref, rb_ref, y_ref, cls_ref):
    subs = [slice(r, r + POST_SUB) for r in range(0, a_ref.shape[0], POST_SUB)]
    y = _dot(a_ref[subs[0], :], w_ref[...])
    for n, rows in enumerate(subs):
        y_next = _dot(a_ref[subs[n + 1], :], w_ref[...]) if n + 1 < len(subs) else None
        x1 = _layer_norm(DN_ALPHA * x_ref[rows, :] + y, g_ref[...], b_ref[...])
        y_ref[rows, :] = x1
        logits = _dot_nt(rwt_ref[...], x1.astype(BF16))
        cls_ref[:, rows] = _route_rows(_sigmoid(logits) + rb_ref[...])
        y = y_next


def _post(a, w, x, g, b, rwt, rb, tm):
    t, k = a.shape
    row = lambda n: pl.BlockSpec((tm, n), lambda i: (i, 0))
    full = lambda arr: pl.BlockSpec(arr.shape, lambda i: (0,) * arr.ndim)
    return pl.pallas_call(
        _post_kernel,
        grid=(t // tm,),
        in_specs=[row(k), full(w), row(D_MODEL), full(g), full(b), full(rwt), full(rb)],
        out_specs=[row(D_MODEL), pl.BlockSpec((1, tm), lambda i: (0, i))],
        out_shape=[jax.ShapeDtypeStruct((t, D_MODEL), F32),
                   jax.ShapeDtypeStruct((1, t), jnp.int32)],
        compiler_params=_params("parallel"),
        name="mixer_out_norm_route",
    )(a, w, x, g, b, rwt, rb)


def _expert_kernel(ea_ref, eb_ref, fresh_ref, nblk_ref, x_ref, rw_ref, wga_ref, wua_ref, wda_ref,
                   wgb_ref, wub_ref, wdb_ref, g_ref, b_ref, prev_ref, o_ref, wga, wua, wda, wgb, wub, wdb):
    del prev_ref
    i = pl.program_id(0)

    @pl.when(fresh_ref[i] == 1)
    def _():
        for src, dst in ((wga_ref, wga), (wua_ref, wua), (wda_ref, wda),
                         (wgb_ref, wgb), (wub_ref, wub), (wdb_ref, wdb)):
            dst[...] = src[...].astype(BF16)

    @pl.when(i < nblk_ref[0])
    def _():
        x = x_ref[...]
        xb = x.astype(BF16)
        sig = _sigmoid(_dot(xb, rw_ref[...]))
        lane = lax.broadcasted_iota(jnp.int32, sig.shape, 1)
        sa = jnp.sum(jnp.where(lane == ea_ref[i], sig, 0.0), axis=-1, keepdims=True)
        sb = jnp.sum(jnp.where(lane == eb_ref[i], sig, 0.0), axis=-1, keepdims=True)
        tot = sa + sb
        ha = _silu(_dot(xb, wga[...])) * _dot(xb, wua[...])
        hb = _silu(_dot(xb, wgb[...])) * _dot(xb, wub[...])
        y = (sa / tot) * _dot(ha.astype(BF16), wda[...]) + (sb / tot) * _dot(hb.astype(BF16), wdb[...])
        o_ref[...] = _layer_norm(DN_ALPHA * x + y, g_ref[...], b_ref[...])

    @pl.when(i >= nblk_ref[0])
    def _():
        o_ref[...] = jnp.zeros(o_ref.shape, o_ref.dtype)


def _experts(xs, ea, eb, fresh, nblk, rw, layer, wg, wu, wd, g, b, bm, blk0, total_rows, prev):
    p = xs.shape[0]
    sel = lambda which: (lambda i, ea, eb, fr, nb: (layer, (ea if which == 0 else eb)[i], 0, 0))
    up = lambda which: pl.BlockSpec((None, None, D_MODEL, EXPERT_FF), sel(which))
    down = lambda which: pl.BlockSpec((None, None, EXPERT_FF, D_MODEL), sel(which))
    full = lambda arr: pl.BlockSpec(arr.shape, lambda i, ea, eb, fr, nb: (0,) * arr.ndim)
    row = pl.BlockSpec((bm, D_MODEL), lambda i, ea, eb, fr, nb: (i, 0))
    out_row = pl.BlockSpec((bm, D_MODEL), lambda i, ea, eb, fr, nb: (i + blk0, 0))
    up_s = pltpu.VMEM((D_MODEL, EXPERT_FF), BF16)
    down_s = pltpu.VMEM((EXPERT_FF, D_MODEL), BF16)
    n_prefetch = 4
    if prev is None:
        prev = jnp.zeros((8, LANES), F32)
        aliases = {}
    else:
        aliases = {n_prefetch + 10: 0}
    return pl.pallas_call(
        _expert_kernel,
        grid_spec=pltpu.PrefetchScalarGridSpec(
            num_scalar_prefetch=n_prefetch,
            grid=(p // bm,),
            in_specs=[row, full(rw), up(0), up(0), down(0), up(1), up(1), down(1), full(g), full(b),
                      pl.BlockSpec(memory_space=pl.ANY)],
            out_specs=out_row,
            scratch_shapes=[up_s, up_s, down_s, up_s, up_s, down_s]),
        out_shape=jax.ShapeDtypeStruct((total_rows, D_MODEL), F32),
        input_output_aliases=aliases,
        compiler_params=_params("arbitrary"),
        name="moe_experts",
    )(ea, eb, fresh, nblk, xs, rw, wg, wu, wd, wg, wu, wd, g, b, prev)


def _row_gather(src, idx):
    m = idx.shape[0]
    d = src.shape[1]
    per = m // SC_WORKERS
    assert per * SC_WORKERS == m
    sc_rows = next(r for r in range(SC_MAX_ROWS, 0, -8) if per % r == 0 and (per // r) % 2 == 0)
    n = per // sc_rows
    mesh = plsc.VectorSubcoreMesh(core_axis_name="core", subcore_axis_name="subcore")

    def body(src_hbm, idx_hbm, out_hbm, idx_v, buf, gsem, psem):
        worker = lax.axis_index("core") * (SC_WORKERS // 2) + lax.axis_index("subcore")
        base = worker * per
        pltpu.sync_copy(idx_hbm.at[pl.ds(base, per)], idx_v)

        def gather(c, slot):
            return pltpu.make_async_copy(src_hbm.at[idx_v.at[pl.ds(c * sc_rows, sc_rows)]],
                                         buf.at[slot], gsem.at[slot])

        def put(c, slot):
            return pltpu.make_async_copy(buf.at[slot], out_hbm.at[pl.ds(base + c * sc_rows, sc_rows)],
                                         psem.at[slot])

        gather(0, 0).start()

        @pl.loop(0, n // 2)
        def _(j):
            for slot in range(2):
                c = 2 * j + slot
                gather(c, slot).wait()
                put(c, slot).start()

                @pl.when(c >= 1)
                def _():
                    put(c - 1, 1 - slot).wait()

                @pl.when(c + 1 < n)
                def _():
                    gather(c + 1, 1 - slot).start()

        put(n - 1, 1).wait()

    return pl.kernel(
        body,
        out_type=jax.ShapeDtypeStruct((m, d), src.dtype),
        mesh=mesh,
        scratch_types=[pltpu.VMEM((per,), jnp.int32), pltpu.VMEM((2, sc_rows, d), src.dtype),
                       pltpu.SemaphoreType.DMA((2,)), pltpu.SemaphoreType.DMA((2,))],
    )(src, idx)


def _moe(x1, cls, rw, layer, wg, wu, wd, g, b, bm):
    t = x1.shape[0]
    p_rows = t + N_CLASSES * bm
    nblk = p_rows // bm
    classes = jnp.arange(N_CLASSES, dtype=jnp.int32)
    counts = jnp.sum((cls[:, None] == classes[None, :]).astype(jnp.int32), axis=0)
    padded = ((counts + bm - 1) // bm) * bm
    off = jnp.cumsum(counts) - counts
    pend = jnp.cumsum(padded)
    poff = pend - padded
    tok = jnp.arange(t, dtype=jnp.int32)
    cls_sorted, order = lax.sort((cls, tok), num_keys=1)
    shift = jnp.sum(jnp.where(cls_sorted[:, None] == classes[None, :], (poff - off)[None, :], 0), axis=1)
    _, dest = lax.sort((order, shift + tok), num_keys=1)
    blk_start = jnp.arange(nblk, dtype=jnp.int32) * bm
    blk_cls = jnp.minimum(jnp.sum((blk_start[:, None] >= pend[None, :]).astype(jnp.int32), axis=1),
                          N_CLASSES - 1)
    n_used = (pend[-1] // bm).astype(jnp.int32).reshape(1)
    j = jnp.arange(bm, dtype=jnp.int32)[None, :] + (blk_start - poff[blk_cls])[:, None]
    rank = off[blk_cls][:, None] + jnp.minimum(j, counts[blk_cls][:, None] - 1)
    src = order[jnp.clip(rank, 0, t - 1).reshape(p_rows)]
    grp = blk_cls // N_PAIRS
    pair = blk_cls % N_PAIRS
    lo = jnp.where(pair < 3, 0, jnp.where(pair < 5, 1, 2))
    hi = jnp.where(pair < 3, pair + 1, jnp.where(pair < 5, pair - 1, 3))
    ea = (grp * EXPERTS_PER_GROUP + lo).astype(jnp.int32)
    eb = (grp * EXPERTS_PER_GROUP + hi).astype(jnp.int32)
    fresh = jnp.concatenate([jnp.ones((1,), jnp.int32), (blk_cls[1:] != blk_cls[:-1]).astype(jnp.int32)])
    nb = nblk // MOE_SPLIT
    xs = [_row_gather(x1, src[k * nb * bm:(k + 1) * nb * bm]) for k in range(MOE_SPLIT)]
    ys = None
    for k in range(MOE_SPLIT):
        part = slice(k * nb, (k + 1) * nb)
        fresh_k = jnp.concatenate([jnp.ones((1,), jnp.int32), fresh[part][1:]])
        ys = _experts(xs[k], ea[part], eb[part], fresh_k, n_used - k * nb, rw, layer, wg, wu, wd, g, b, bm,
                      k * nb, p_rows, ys)
    return _row_gather(ys, dest)


def _mlstm_pre_kernel(x_ref, w_in_ref, cw_ref, cb_ref, bdqk_ref, bdv_ref, wgate_ref, bgate_ref,
                      q_ref, k_ref, v_ref, xc_ref, sz_ref, gate_ref, ext_ref, *, tm):
    halo = 8
    xb = x_ref[...].astype(BF16)

    @pl.when(pl.program_id(1) == 0)
    def _():
        ext_ref[0:halo, :] = jnp.zeros((halo, MLSTM_INNER), F32)

    @pl.when(pl.program_id(1) != 0)
    def _():
        ext_ref[0:halo, :] = ext_ref[tm:tm + halo, :]

    xm = _dot(xb, w_in_ref[:, :MLSTM_INNER])
    ext_ref[halo:halo + tm, :] = xm
    conv = cb_ref[...] + cw_ref[MLSTM_CONV - 1:MLSTM_CONV, :] * xm
    for kk in range(MLSTM_CONV - 1):
        lag = MLSTM_CONV - 1 - kk
        conv = conv + cw_ref[kk:kk + 1, :] * ext_ref[halo - lag:halo - lag + tm, :]
    xc = _silu(conv)
    xc_b = xc.astype(BF16)
    xm_b = xm.astype(BF16)
    xc_ref[...] = xc_b
    sz_ref[...] = _silu(_dot(xb, w_in_ref[:, MLSTM_INNER:2 * MLSTM_INNER])).astype(BF16)

    gates = bgate_ref[...] + _dot(xb, w_in_ref[:, 2 * MLSTM_INNER:]) + _dot(xc_b, wgate_ref[...])
    k_scale = MLSTM_HD ** -0.5
    for c in range(MLSTM_INNER // MXU_DIM):
        lo, hi = c * MXU_DIM, (c + 1) * MXU_DIM
        qk = _dot(xc_b[:, lo:hi], bdqk_ref[c])
        q_ref[:, lo:hi] = qk[:, :MXU_DIM].astype(BF16)
        k_ref[:, lo:hi] = (qk[:, MXU_DIM:] * k_scale).astype(BF16)
        v_ref[:, lo:hi] = _dot(xm_b[:, lo:hi], bdv_ref[c]).astype(BF16)
    fg = gates[:, LANES:]
    gate_ref[:, :LANES] = gates[:, :LANES]
    gate_ref[:, LANES:] = jnp.minimum(fg, 0.0) - jnp.log(1.0 + jnp.exp(-jnp.abs(fg)))


def _mlstm_pre(x, w, batch, seq, tm):
    t = x.shape[0]
    ns = seq // tm
    row = lambda n: pl.BlockSpec((tm, n), lambda b, s: (b * ns + s, 0))
    full = lambda a: pl.BlockSpec(a.shape, lambda b, s: (0,) * a.ndim)
    consts = (w["w_in"], w["conv_w"], w["conv_b"], w["bd_qk"], w["bd_v"], w["w_gate"], w["b_gate"])
    act = jax.ShapeDtypeStruct((t, MLSTM_INNER), BF16)
    return pl.pallas_call(
        functools.partial(_mlstm_pre_kernel, tm=tm),
        grid=(batch, ns),
        in_specs=[row(D_MODEL)] + [full(a) for a in consts],
        out_specs=[row(MLSTM_INNER)] * 5 + [row(2 * LANES)],
        out_shape=[act] * 5 + [jax.ShapeDtypeStruct((t, 2 * LANES), F32)],
        scratch_shapes=[pltpu.VMEM((tm + 8, MLSTM_INNER), F32)],
        compiler_params=_params("parallel", "arbitrary"),
        name="mlstm_pre",
    )(x, *consts)


def _split3(v):
    a = v.astype(BF16)
    r = v - a.astype(F32)
    b = r.astype(BF16)
    c = (r - b.astype(F32)).astype(BF16)
    return a, b, c


def _mlstm_chunk_kernel(q_ref, k_ref, v_ref, gate_ref, xc_ref, sz_ref, gout_ref, skip_ref, o_ref,
                        c_ref, n_ref, m_ref, *, chunk):
    @pl.when(pl.program_id(1) == 0)
    def _():
        c_ref[...] = jnp.zeros(c_ref.shape, F32)
        n_ref[...] = jnp.zeros(n_ref.shape, F32)
        m_ref[...] = jnp.zeros(m_ref.shape, F32)

    ig_all = gate_ref[:, :LANES]
    lf_all = gate_ref[:, LANES:]
    rows = lax.broadcasted_iota(jnp.int32, (chunk, chunk), 0)
    cols = lax.broadcasted_iota(jnp.int32, (chunk, chunk), 1)
    causal = cols <= rows
    tril = jnp.where(causal, 1.0, 0.0).astype(BF16)
    b_all = functools.reduce(lambda u, w: u + w, [_dot(tril, piece) for piece in _split3(lf_all)])
    row_all = jnp.transpose(ig_all - b_all)

    for h in range(MLSTM_HEADS):
        lo, hi = h * MLSTM_HD, (h + 1) * MLSTM_HD
        q = q_ref[:, lo:hi]
        k = k_ref[:, lo:hi]
        v = v_ref[:, lo:hi]
        b = b_all[:, h:h + 1]
        ig = ig_all[:, h:h + 1]
        m_prev = m_ref[h]
        d = jnp.where(causal, b + row_all[h:h + 1, :], -jnp.inf)
        inter = b + m_prev
        m_i = jnp.maximum(inter, jnp.max(d, axis=-1, keepdims=True))
        w_intra = jnp.exp(d - m_i)
        w_inter = jnp.exp(inter - m_i)
        s = _dot_nt(q, k) * w_intra
        c_old = c_ref[h]
        n_old = n_ref[h]
        num = _dot(s.astype(BF16), v) + w_inter * _dot(q, c_old.astype(BF16))
        qn = jnp.sum(q.astype(F32) * n_old, axis=-1, keepdims=True)
        den = jnp.sum(s, axis=-1, keepdims=True) + w_inter * qn
        hh = num / jnp.maximum(jnp.abs(den), jnp.exp(-m_i))

        b_last = b[chunk - 1:chunk, :]
        gg = b_last - b + ig
        m_new = jnp.maximum(b_last + m_prev, jnp.max(gg, axis=0, keepdims=True))
        kw = k.astype(F32) * jnp.exp(gg - m_new)
        decay = jnp.exp(b_last + m_prev - m_new)
        c_ref[h] = decay * c_old + _dot_tn(kw.astype(BF16), v)
        n_ref[h] = decay * n_old + jnp.sum(kw, axis=0, keepdims=True)
        m_ref[h] = m_new

        mu = jnp.mean(hh, axis=-1, keepdims=True)
        cen = hh - mu
        var = jnp.mean(cen * cen, axis=-1, keepdims=True)
        hn = cen * lax.rsqrt(var + LN_EPS) * gout_ref[:, lo:hi]
        o_ref[:, lo:hi] = ((hn + skip_ref[:, lo:hi] * xc_ref[:, lo:hi].astype(F32))
                           * sz_ref[:, lo:hi].astype(F32)).astype(o_ref.dtype)


def _mlstm_chunks(q, k, v, gates, xc, sz, g_out, skip, batch, seq, chunk):
    t = q.shape[0]
    nc = seq // chunk
    row = lambda n: pl.BlockSpec((chunk, n), lambda b, c: (b * nc + c, 0))
    full = lambda a: pl.BlockSpec(a.shape, lambda b, c: (0,) * a.ndim)
    return pl.pallas_call(
        functools.partial(_mlstm_chunk_kernel, chunk=chunk),
        grid=(batch, nc),
        in_specs=[row(MLSTM_INNER)] * 3 + [row(2 * LANES)] + [row(MLSTM_INNER)] * 2
                 + [full(g_out), full(skip)],
        out_specs=row(MLSTM_INNER),
        out_shape=jax.ShapeDtypeStruct((t, MLSTM_INNER), BF16),
        scratch_shapes=[pltpu.VMEM((MLSTM_HEADS, MLSTM_HD, MLSTM_HD), F32),
                        pltpu.VMEM((MLSTM_HEADS, 1, MLSTM_HD), F32),
                        pltpu.VMEM((MLSTM_HEADS, 1, 1), F32)],
        compiler_params=_params("parallel", "arbitrary"),
        name="mlstm_chunks",
    )(q, k, v, gates, xc, sz, g_out, skip)


def _rot_half(w):
    half = MLA_ROPE // 2
    return jnp.concatenate([-w[..., half:], w[..., :half]], axis=-1)


def _prep_mla(w_in, g_q, w_uq, g_kv, w_ukv):
    pad = jnp.zeros((D_MODEL, LANES - MLA_ROPE), F32)
    w_pe = w_in[:, MLA_Q_RANK + MLA_KV_RANK:]
    w_in_p = jnp.concatenate([w_in[:, :MLA_Q_RANK + MLA_KV_RANK], w_pe, pad, _rot_half(w_pe), pad], axis=1)
    uq = w_uq.reshape(MLA_Q_RANK, MLA_HEADS, MLA_NOPE + MLA_ROPE)
    zq = jnp.zeros((MLA_Q_RANK, MLA_HEADS, LANES - MLA_ROPE), F32)
    w_q1 = jnp.concatenate([uq, zq], axis=-1).reshape(MLA_Q_RANK, MLA_HEADS * HEAD_SLOT)
    w_q2 = jnp.concatenate([_rot_half(uq[..., MLA_NOPE:]), zq], axis=-1).reshape(MLA_Q_RANK, MLA_HEADS * LANES)
    ukv = w_ukv.reshape(MLA_KV_RANK, MLA_HEADS, MLA_NOPE + MLA_V)
    w_k = ukv[..., :MLA_NOPE].reshape(MLA_KV_RANK, -1)
    w_vt = ukv[..., MLA_NOPE:].reshape(MLA_KV_RANK, -1).T
    return dict(w_in=w_in_p.astype(BF16), g_q=g_q.reshape(1, -1), g_kv=g_kv.reshape(1, -1),
                w_q1=w_q1.astype(BF16), w_q2=w_q2.astype(BF16), w_k=w_k.astype(BF16),
                w_vt=w_vt.astype(BF16))


def _block_diag_tiles(w):
    per = MXU_DIM // MLSTM_QKV_BLOCK
    w = w.reshape(MLSTM_INNER // MXU_DIM, per, MLSTM_QKV_BLOCK, MLSTM_QKV_BLOCK)
    eye = jnp.eye(per, dtype=w.dtype)
    return jnp.einsum("cgij,gh->cgihj", w, eye).reshape(MLSTM_INNER // MXU_DIM, MXU_DIM, MXU_DIM)


def _prep_mlstm(w_in, conv_w, conv_b, w_q, w_k, w_v, w_ig, b_ig, w_fg, b_fg):
    bd_qk = jnp.concatenate([_block_diag_tiles(w_q), _block_diag_tiles(w_k)], axis=-1)
    zw = jnp.zeros((3 * MLSTM_INNER, LANES - MLSTM_HEADS), F32)
    zb = jnp.zeros((LANES - MLSTM_HEADS,), F32)
    w_gate = jnp.concatenate([w_ig, zw, w_fg, zw], axis=1)
    b_gate = jnp.concatenate([b_ig, zb, b_fg, zb]).reshape(1, 2 * LANES)
    blocks = lambda a: a.reshape(MLSTM_INNER // MLSTM_QKV_BLOCK, MLSTM_QKV_BLOCK, 2 * LANES)
    fold = lambda bd, a: jnp.einsum("gij,gjn->gin", bd, blocks(a), precision=lax.Precision.HIGHEST)
    gate_q, gate_k, gate_v = (w_gate[n * MLSTM_INNER:(n + 1) * MLSTM_INNER] for n in range(3))
    w_gate_xc = (fold(w_q, gate_q) + fold(w_k, gate_k)).reshape(MLSTM_INNER, 2 * LANES)
    w_gate_x = jnp.dot(w_in[:, :MLSTM_INNER], fold(w_v, gate_v).reshape(MLSTM_INNER, 2 * LANES),
                       precision=lax.Precision.HIGHEST)
    return dict(w_in=jnp.concatenate([w_in, w_gate_x], axis=1).astype(BF16), conv_w=conv_w,
                conv_b=conv_b.reshape(1, -1), bd_qk=bd_qk.astype(BF16),
                bd_v=_block_diag_tiles(w_v).astype(BF16), w_gate=w_gate_xc.astype(BF16), b_gate=b_gate)


def kernel(x, positions, router_w, router_b, mla_w_in, mla_g_q, mla_w_uq, mla_g_kv, mla_w_ukv, mla_w_o, mlstm_w_in, mlstm_conv_w, mlstm_conv_b, mlstm_w_q, mlstm_w_k, mlstm_w_v, mlstm_w_ig, mlstm_b_ig, mlstm_w_fg, mlstm_b_fg, mlstm_g_out, mlstm_skip, mlstm_w_out, moe_w_gate, moe_w_up, moe_w_down, ln_g, ln_b):
    batch, seq, _ = x.shape
    tiles = _tiles(seq)
    rwt = router_w.T.astype(BF16)
    rb = router_b.reshape(N_EXPERTS, 1)
    rw_pad = jnp.concatenate([router_w, jnp.zeros((D_MODEL, LANES - N_EXPERTS), F32)], axis=1).astype(BF16)
    t = batch * seq
    cos, sin = _rope_tables(positions, tiles["rope"])
    xt = x.reshape(t, D_MODEL)
    for i in range(DEPTH):
        j = i // N_MIXERS
        g0, b0 = ln_g[i, 0].reshape(1, -1), ln_b[i, 0].reshape(1, -1)
        g1, b1 = ln_g[i, 1].reshape(1, -1), ln_b[i, 1].reshape(1, -1)
        if i % N_MIXERS == 0:
            w = _prep_mla(mla_w_in[j], mla_g_q[j], mla_w_uq[j], mla_g_kv[j], mla_w_ukv[j])
            w_out = mla_w_o[j].astype(BF16)
        else:
            w = _prep_mlstm(mlstm_w_in[j], mlstm_conv_w[j], mlstm_conv_b[j], mlstm_w_q[j], mlstm_w_k[j],
                            mlstm_w_v[j], mlstm_w_ig[j], mlstm_b_ig[j], mlstm_w_fg[j], mlstm_b_fg[j])
            w_out = mlstm_w_out[j].astype(BF16)
        if i % N_MIXERS == 0:
            q, k, vt = _mla_proj(xt, cos, sin, w, tiles["attn"])
            a = _attention(q, k, vt, batch, seq, tiles["attn"])
        else:
            q, k, v, xc, sz, gates = _mlstm_pre(xt, w, batch, seq, tiles["pre"])
            a = _mlstm_chunks(q, k, v, gates, xc, sz, mlstm_g_out[j].reshape(1, -1),
                              mlstm_skip[j].reshape(1, -1), batch, seq, tiles["chunk"])
        x1, cls = _post(a, w_out, xt, g0, b0, rwt, rb, tiles["post"])
        xt = _moe(x1, cls.reshape(t), rw_pad, i, moe_w_gate, moe_w_up, moe_w_down, g1, b1, tiles["moe"])
    return xt.reshape(batch, seq, D_MODEL)
```

```python
import functools

import jax
import jax.numpy as jnp
from jax import lax
from jax.experimental import pallas as pl
from jax.experimental.pallas import tpu as pltpu
from jax.experimental.pallas import tpu_sc as plsc

D_MODEL = 1024
DEPTH = 4
N_MIXERS = 2
MLA_HEADS = 8
MLA_Q_RANK = 256
MLA_KV_RANK = 256
MLA_NOPE = 128
MLA_ROPE = 64
MLA_V = 128
ROPE_THETA = 10000.0
MLSTM_INNER = 2 * D_MODEL
MLSTM_HEADS = 4
MLSTM_HD = MLSTM_INNER // MLSTM_HEADS
MLSTM_CONV = 4
MLSTM_QKV_BLOCK = 4
N_EXPERTS = 16
N_GROUPS = 4
EXPERTS_PER_GROUP = N_EXPERTS // N_GROUPS
EXPERT_FF = 512
DN_ALPHA = (2.0 * DEPTH) ** 0.25
LN_EPS = 1e-5
LOG2_E = 1.4426950408889634

LANES = 128
MXU_DIM = 256
HEAD_SLOT = 2 * LANES
ATTN_HEADS_PER_STEP = 4
N_PAIRS = EXPERTS_PER_GROUP * (EXPERTS_PER_GROUP - 1) // 2
N_CLASSES = N_GROUPS * N_PAIRS
SC_WORKERS = 32
SC_MAX_ROWS = 40
SC_SLOTS = 3
MOE_SPLIT = 4
POST_SUB = 512
VMEM_LIMIT = 56 * 1024 * 1024

F32 = jnp.float32
BF16 = jnp.bfloat16


def _tiles(seq):
    return dict(
        rope=min(seq, 2048),
        attn=min(seq, 512),
        post=min(seq, 1024),
        pre=min(seq, 256),
        chunk=min(seq, 256),
        moe=256,
    )


def _params(*sem):
    return pltpu.CompilerParams(dimension_semantics=sem, vmem_limit_bytes=VMEM_LIMIT)


def _dot(a, b):
    return jnp.dot(a, b, preferred_element_type=F32)


def _dot_nt(a, b):
    return lax.dot_general(a, b, (((1,), (1,)), ((), ())), preferred_element_type=F32)


def _dot_tn(a, b):
    return lax.dot_general(a, b, (((0,), (0,)), ((), ())), preferred_element_type=F32)


def _layer_norm(v, g, b):
    mu = jnp.mean(v, axis=-1, keepdims=True)
    c = v - mu
    var = jnp.mean(c * c, axis=-1, keepdims=True)
    return c * lax.rsqrt(var + LN_EPS) * g + b


def _rms_norm(v, g):
    return v * lax.rsqrt(jnp.mean(v * v, axis=-1, keepdims=True) + LN_EPS) * g


def _sigmoid(v):
    return 1.0 / (1.0 + jnp.exp(-v))


def _silu(v):
    return v * _sigmoid(v)


def _rope_kernel(pos_ref, invf_ref, cos_ref, sin_ref):
    ang = pos_ref[...].astype(F32) * invf_ref[...]
    cos_ref[...] = jnp.cos(ang)
    sin_ref[...] = jnp.sin(ang)


def _rope_tables(positions, tm):
    t = positions.size
    inv_freq = 1.0 / (ROPE_THETA ** (jnp.arange(0, MLA_ROPE, 2, dtype=F32) / MLA_ROPE))
    invf = jnp.tile(inv_freq, LANES // (MLA_ROPE // 2)).reshape(1, LANES)
    out = jax.ShapeDtypeStruct((t, LANES), F32)
    return pl.pallas_call(
        _rope_kernel,
        grid=(t // tm,),
        in_specs=[pl.BlockSpec((tm, 1), lambda i: (i, 0)),
                  pl.BlockSpec((1, LANES), lambda i: (0, 0))],
        out_specs=[pl.BlockSpec((tm, LANES), lambda i: (i, 0))] * 2,
        out_shape=[out, out],
        compiler_params=_params("parallel"),
        name="rope_tables",
    )(positions.reshape(t, 1), invf)


def _mla_proj_kernel(x_ref, cos_ref, sin_ref, w_in_ref, gq_ref, gkv_ref, wq1_ref, wq2_ref,
                     wk_ref, wvt_ref, q_ref, k_ref, vt_ref):
    xb = x_ref[...].astype(BF16)
    c = _dot(xb, w_in_ref[...])
    cq = _rms_norm(c[:, :MLA_Q_RANK], gq_ref[...]).astype(BF16)
    ckv = _rms_norm(c[:, MLA_Q_RANK:MLA_Q_RANK + MLA_KV_RANK], gkv_ref[...]).astype(BF16)
    cos = cos_ref[...]
    sin = sin_ref[...]
    base = MLA_Q_RANK + MLA_KV_RANK
    k_pe = (c[:, base:base + LANES] * cos + c[:, base + LANES:base + 2 * LANES] * sin).astype(BF16)
    scale = (MLA_NOPE + MLA_ROPE) ** -0.5 * LOG2_E
    cos_s = cos * scale
    sin_s = sin * scale
    q1 = _dot(cq, wq1_ref[...])
    q2 = _dot(cq, wq2_ref[...])
    kn = _dot(ckv, wk_ref[...])
    for h in range(MLA_HEADS):
        s0 = h * HEAD_SLOT
        q_ref[:, s0:s0 + LANES] = (q1[:, s0:s0 + LANES] * scale).astype(BF16)
        q_ref[:, s0 + LANES:s0 + HEAD_SLOT] = (
            q1[:, s0 + LANES:s0 + HEAD_SLOT] * cos_s
            + q2[:, h * LANES:(h + 1) * LANES] * sin_s).astype(BF16)
        k_ref[:, s0:s0 + LANES] = kn[:, h * LANES:(h + 1) * LANES].astype(BF16)
        k_ref[:, s0 + LANES:s0 + HEAD_SLOT] = k_pe
    vt_ref[...] = _dot_nt(wvt_ref[...], ckv).astype(BF16)


def _mla_proj(x, cos, sin, w, tm):
    t = x.shape[0]
    row = lambda n: pl.BlockSpec((tm, n), lambda i: (i, 0))
    full = lambda a: pl.BlockSpec(a.shape, lambda i: (0,) * a.ndim)
    consts = (w["w_in"], w["g_q"], w["g_kv"], w["w_q1"], w["w_q2"], w["w_k"], w["w_vt"])
    return pl.pallas_call(
        _mla_proj_kernel,
        grid=(t // tm,),
        in_specs=[row(D_MODEL), row(LANES), row(LANES)] + [full(a) for a in consts],
        out_specs=[row(MLA_HEADS * HEAD_SLOT), row(MLA_HEADS * HEAD_SLOT),
                   pl.BlockSpec((None, MLA_HEADS * MLA_V, tm), lambda i: (i, 0, 0))],
        out_shape=[jax.ShapeDtypeStruct((t, MLA_HEADS * HEAD_SLOT), BF16),
                   jax.ShapeDtypeStruct((t, MLA_HEADS * HEAD_SLOT), BF16),
                   jax.ShapeDtypeStruct((t // tm, MLA_HEADS * MLA_V, tm), BF16)],
        compiler_params=_params("parallel"),
        name="mla_proj",
    )(x, cos, sin, *consts)


def _attn_kernel(q_ref, k_ref, vt_ref, o_ref, s_ref, m_ref, l_ref, acc_ref, *, tile):
    qi = pl.program_id(2)
    m_ref[...] = jnp.full(m_ref.shape, -jnp.inf, F32)
    l_ref[...] = jnp.zeros(l_ref.shape, F32)
    acc_ref[...] = jnp.zeros(acc_ref.shape, F32)
    heads = range(ATTN_HEADS_PER_STEP)

    def scores(ki, buf):
        start = pl.multiple_of(ki * tile, tile)
        for g in heads:
            slot = slice(g * HEAD_SLOT, (g + 1) * HEAD_SLOT)
            s_ref[buf, g] = _dot_nt(k_ref[pl.ds(start, tile), slot], q_ref[:, slot])

    def absorb(ki, buf, masked):
        for g in heads:
            s = s_ref[buf, g]
            if masked:
                keys = lax.broadcasted_iota(jnp.int32, s.shape, 0)
                queries = lax.broadcasted_iota(jnp.int32, s.shape, 1)
                s = jnp.where(keys <= queries, s, -jnp.inf)
            m_old = m_ref[g]
            m_new = jnp.maximum(m_old, jnp.max(s, axis=0, keepdims=True))
            p = jnp.exp2(s - m_new)
            alpha = jnp.exp2(m_old - m_new)
            l_ref[g] = alpha * l_ref[g] + jnp.sum(p, axis=0, keepdims=True)
            acc_ref[g] = alpha * acc_ref[g] + _dot(vt_ref[ki, g * MLA_V:(g + 1) * MLA_V, :], p.astype(BF16))
            m_ref[g] = m_new

    scores(0, 0)

    def body(j, carry):
        scores(2 * j + 1, 1)
        absorb(2 * j, 0, False)
        scores(2 * j + 2, 0)
        absorb(2 * j + 1, 1, False)
        return carry

    lax.fori_loop(0, qi // 2, body, 0)

    @pl.when(qi % 2 == 1)
    def _():
        scores(qi, 1)
        absorb(qi - 1, 0, False)
        absorb(qi, 1, True)

    @pl.when(qi % 2 == 0)
    def _():
        absorb(qi, 0, True)

    for g in heads:
        o_ref[:, g * MLA_V:(g + 1) * MLA_V] = jnp.transpose(acc_ref[g] / l_ref[g]).astype(o_ref.dtype)


def _attention(q, k, vt, batch, seq, tile):
    t = q.shape[0]
    nq = seq // tile
    hps = ATTN_HEADS_PER_STEP
    return pl.pallas_call(
        functools.partial(_attn_kernel, tile=tile),
        grid=(batch, MLA_HEADS // hps, nq),
        in_specs=[pl.BlockSpec((tile, hps * HEAD_SLOT), lambda b, h, i: (b * nq + i, h)),
                  pl.BlockSpec((seq, hps * HEAD_SLOT), lambda b, h, i: (b, h)),
                  pl.BlockSpec((nq, hps * MLA_V, tile), lambda b, h, i: (b, h, 0))],
        out_specs=pl.BlockSpec((tile, hps * MLA_V), lambda b, h, i: (b * nq + i, h)),
        out_shape=jax.ShapeDtypeStruct((t, MLA_HEADS * MLA_V), BF16),
        scratch_shapes=[pltpu.VMEM((2, hps, tile, tile), F32),
                        pltpu.VMEM((hps, 1, tile), F32), pltpu.VMEM((hps, 1, tile), F32),
                        pltpu.VMEM((hps, MLA_V, tile), F32)],
        compiler_params=_params("parallel", "parallel", "arbitrary"),
        name="mla_attention",
    )(q, k, vt)


def _route_rows(sel):
    best = None
    for g in range(N_GROUPS):
        a = [sel[g * EXPERTS_PER_GROUP + j:g * EXPERTS_PER_GROUP + j + 1, :]
             for j in range(EXPERTS_PER_GROUP)]
        v1 = functools.reduce(jnp.maximum, a)
        i1 = jnp.full(v1.shape, EXPERTS_PER_GROUP - 1, jnp.int32)
        for j in reversed(range(EXPERTS_PER_GROUP - 1)):
            i1 = jnp.where(a[j] == v1, j, i1)
        r = [jnp.where(i1 == j, -jnp.inf, a[j]) for j in range(EXPERTS_PER_GROUP)]
        v2 = functools.reduce(jnp.maximum, r)
        i2 = jnp.full(v1.shape, EXPERTS_PER_GROUP - 1, jnp.int32)
        for j in reversed(range(EXPERTS_PER_GROUP - 1)):
            i2 = jnp.where(r[j] == v2, j, i2)
        total = v1 + v2
        if best is None:
            best = (total, jnp.zeros_like(i1), i1, i2)
        else:
            better = total > best[0]
            best = (jnp.where(better, total, best[0]), jnp.where(better, g, best[1]),
                    jnp.where(better, i1, best[2]), jnp.where(better, i2, best[3]))
    _, grp, i1, i2 = best
    lo = jnp.minimum(i1, i2)
    hi = jnp.maximum(i1, i2)
    pair = jnp.where(lo == 0, hi - 1, jnp.where(lo == 1, hi + 1, N_PAIRS - 1))
    return grp * N_PAIRS + pair


def _post_kernel(a_ref, w_ref, x_ref, g_ref, b_ref, rwt_ref, rb_ref, y_ref, cls_ref):
    subs = [slice(r, r + POST_SUB) for r in range(0, a_ref.shape[0], POST_SUB)]
    y = _dot(a_ref[subs[0], :], w_ref[...])
    for n, rows in enumerate(subs):
        y_next = _dot(a_ref[subs[n + 1], :], w_ref[...]) if n + 1 < len(subs) else None
        x1 = _layer_norm(DN_ALPHA * x_ref[rows, :] + y, g_ref[...], b_ref[...])
        y_ref[rows, :] = x1
        logits = _dot_nt(rwt_ref[...], x1.astype(BF16))
        cls_ref[:, rows] = _route_rows(_sigmoid(logits) + rb_ref[...])
        y = y_next


def _post(a, w, x, g, b, rwt, rb, tm):
    t, k = a.shape
    row = lambda n: pl.BlockSpec((tm, n), lambda i: (i, 0))
    full = lambda arr: pl.BlockSpec(arr.shape, lambda i: (0,) * arr.ndim)
    return pl.pallas_call(
        _post_kernel,
        grid=(t // tm,),
        in_specs=[row(k), full(w), row(D_MODEL), full(g), full(b), full(rwt), full(rb)],
        out_specs=[row(D_MODEL), pl.BlockSpec((1, tm), lambda i: (0, i))],
        out_shape=[jax.ShapeDtypeStruct((t, D_MODEL), F32),
                   jax.ShapeDtypeStruct((1, t), jnp.int32)],
        compiler_params=_params("parallel"),
        name="mixer_out_norm_route",
    )(a, w, x, g, b, rwt, rb)


def _expert_kernel(ea_ref, eb_ref, fresh_ref, nblk_ref, x_ref, rw_ref, wga_ref, wua_ref, wda_ref,
                   wgb_ref, wub_ref, wdb_ref, g_ref, b_ref, prev_ref, o_ref, wga, wua, wda, wgb, wub, wdb):
    del prev_ref
    i = pl.program_id(0)

    @pl.when(fresh_ref[i] == 1)
    def _():
        for src, dst in ((wga_ref, wga), (wua_ref, wua), (wda_ref, wda),
                         (wgb_ref, wgb), (wub_ref, wub), (wdb_ref, wdb)):
            dst[...] = src[...].astype(BF16)

    @pl.when(i < nblk_ref[0])
    def _():
        x = x_ref[...]
        xb = x.astype(BF16)
        sig = _sigmoid(_dot(xb, rw_ref[...]))
        lane = lax.broadcasted_iota(jnp.int32, sig.shape, 1)
        sa = jnp.sum(jnp.where(lane == ea_ref[i], sig, 0.0), axis=-1, keepdims=True)
        sb = jnp.sum(jnp.where(lane == eb_ref[i], sig, 0.0), axis=-1, keepdims=True)
        tot = sa + sb
        ha = _silu(_dot(xb, wga[...])) * _dot(xb, wua[...])
        hb = _silu(_dot(xb, wgb[...])) * _dot(xb, wub[...])
        y = (sa / tot) * _dot(ha.astype(BF16), wda[...]) + (sb / tot) * _dot(hb.astype(BF16), wdb[...])
        o_ref[...] = _layer_norm(DN_ALPHA * x + y, g_ref[...], b_ref[...])

    @pl.when(i >= nblk_ref[0])
    def _():
        o_ref[...] = jnp.zeros(o_ref.shape, o_ref.dtype)


def _experts(xs, ea, eb, fresh, nblk, rw, layer, wg, wu, wd, g, b, bm, blk0, total_rows, prev):
    p = xs.shape[0]
    sel = lambda which: (lambda i, ea, eb, fr, nb: (layer, (ea if which == 0 else eb)[i], 0, 0))
    up = lambda which: pl.BlockSpec((None, None, D_MODEL, EXPERT_FF), sel(which))
    down = lambda which: pl.BlockSpec((None, None, EXPERT_FF, D_MODEL), sel(which))
    full = lambda arr: pl.BlockSpec(arr.shape, lambda i, ea, eb, fr, nb: (0,) * arr.ndim)
    row = pl.BlockSpec((bm, D_MODEL), lambda i, ea, eb, fr, nb: (i, 0))
    out_row = pl.BlockSpec((bm, D_MODEL), lambda i, ea, eb, fr, nb: (i + blk0, 0))
    up_s = pltpu.VMEM((D_MODEL, EXPERT_FF), BF16)
    down_s = pltpu.VMEM((EXPERT_FF, D_MODEL), BF16)
    n_prefetch = 4
    if prev is None:
        prev = jnp.zeros((8, LANES), F32)
        aliases = {}
    else:
        aliases = {n_prefetch + 10: 0}
    return pl.pallas_call(
        _expert_kernel,
        grid_spec=pltpu.PrefetchScalarGridSpec(
            num_scalar_prefetch=n_prefetch,
            grid=(p // bm,),
            in_specs=[row, full(rw), up(0), up(0), down(0), up(1), up(1), down(1), full(g), full(b),
                      pl.BlockSpec(memory_space=pl.ANY)],
            out_specs=out_row,
            scratch_shapes=[up_s, up_s, down_s, up_s, up_s, down_s]),
        out_shape=jax.ShapeDtypeStruct((total_rows, D_MODEL), F32),
        input_output_aliases=aliases,
        compiler_params=_params("arbitrary"),
        name="moe_experts",
    )(ea, eb, fresh, nblk, xs, rw, wg, wu, wd, wg, wu, wd, g, b, prev)


def _row_gather(src, idx):
    m = idx.shape[0]
    d = src.shape[1]
    per = m // SC_WORKERS
    assert per * SC_WORKERS == m
    sc_rows = next(r for r in range(SC_MAX_ROWS, 0, -8) if per % r == 0)
    n = per // sc_rows
    mesh = plsc.VectorSubcoreMesh(core_axis_name="core", subcore_axis_name="subcore")

    def body(src_hbm, idx_hbm, out_hbm, idx_v, buf, gsem, psem):
        worker = lax.axis_index("core") * (SC_WORKERS // 2) + lax.axis_index("subcore")
        base = worker * per
        pltpu.sync_copy(idx_hbm.at[pl.ds(base, per)], idx_v)

        def gather(c):
            slot = c % SC_SLOTS
            return pltpu.make_async_copy(src_hbm.at[idx_v.at[pl.ds(c * sc_rows, sc_rows)]],
                                         buf.at[slot], gsem.at[slot])

        def put(c):
            slot = c % SC_SLOTS
            return pltpu.make_async_copy(buf.at[slot], out_hbm.at[pl.ds(base + c * sc_rows, sc_rows)],
                                         psem.at[slot])

        for c in range(min(SC_SLOTS - 1, n)):
            gather(c).start()

        @pl.loop(0, n)
        def _(c):
            gather(c).wait()
            put(c).start()

            @pl.when(c >= 1)
            def _():
                put(c - 1).wait()

            @pl.when(c + SC_SLOTS - 1 < n)
            def _():
                gather(c + SC_SLOTS - 1).start()

        put(n - 1).wait()

    return pl.kernel(
        body,
        out_type=jax.ShapeDtypeStruct((m, d), src.dtype),
        mesh=mesh,
        scratch_types=[pltpu.VMEM((per,), jnp.int32), pltpu.VMEM((SC_SLOTS, sc_rows, d), src.dtype),
                       pltpu.SemaphoreType.DMA((SC_SLOTS,)), pltpu.SemaphoreType.DMA((SC_SLOTS,))],
    )(src, idx)


def _moe(x1, cls, rw, layer, wg, wu, wd, g, b, bm):
    t = x1.shape[0]
    p_rows = t + N_CLASSES * bm
    nblk = p_rows // bm
    classes = jnp.arange(N_CLASSES, dtype=jnp.int32)
    counts = jnp.sum((cls[:, None] == classes[None, :]).astype(jnp.int32), axis=0)
    padded = ((counts + bm - 1) // bm) * bm
    off = jnp.cumsum(counts) - counts
    pend = jnp.cumsum(padded)
    poff = pend - padded
    tok = jnp.arange(t, dtype=jnp.int32)
    cls_sorted, order = lax.sort((cls, tok), num_keys=1)
    shift = jnp.sum(jnp.where(cls_sorted[:, None] == classes[None, :], (poff - off)[None, :], 0), axis=1)
    _, dest = lax.sort((order, shift + tok), num_keys=1)
    blk_start = jnp.arange(nblk, dtype=jnp.int32) * bm
    blk_cls = jnp.minimum(jnp.sum((blk_start[:, None] >= pend[None, :]).astype(jnp.int32), axis=1),
                          N_CLASSES - 1)
    n_used = (pend[-1] // bm).astype(jnp.int32).reshape(1)
    j = jnp.arange(bm, dtype=jnp.int32)[None, :] + (blk_start - poff[blk_cls])[:, None]
    rank = off[blk_cls][:, None] + jnp.minimum(j, counts[blk_cls][:, None] - 1)
    src = order[jnp.clip(rank, 0, t - 1).reshape(p_rows)]
    grp = blk_cls // N_PAIRS
    pair = blk_cls % N_PAIRS
    lo = jnp.where(pair < 3, 0, jnp.where(pair < 5, 1, 2))
    hi = jnp.where(pair < 3, pair + 1, jnp.where(pair < 5, pair - 1, 3))
    ea = (grp * EXPERTS_PER_GROUP + lo).astype(jnp.int32)
    eb = (grp * EXPERTS_PER_GROUP + hi).astype(jnp.int32)
    fresh = jnp.concatenate([jnp.ones((1,), jnp.int32), (blk_cls[1:] != blk_cls[:-1]).astype(jnp.int32)])
    nb = nblk // MOE_SPLIT
    xs = [_row_gather(x1, src[k * nb * bm:(k + 1) * nb * bm]) for k in range(MOE_SPLIT)]
    ys = None
    for k in range(MOE_SPLIT):
        part = slice(k * nb, (k + 1) * nb)
        fresh_k = jnp.concatenate([jnp.ones((1,), jnp.int32), fresh[part][1:]])
        ys = _experts(xs[k], ea[part], eb[part], fresh_k, n_used - k * nb, rw, layer, wg, wu, wd, g, b, bm,
                      k * nb, p_rows, ys)
    return _row_gather(ys, dest)


def _mlstm_pre_kernel(x_ref, w_in_ref, cw_ref, cb_ref, bdqk_ref, bdv_ref, wgate_ref, bgate_ref,
                      q_ref, k_ref, v_ref, xc_ref, sz_ref, gate_ref, ext_ref, *, tm):
    halo = 8
    xb = x_ref[...].astype(BF16)

    @pl.when(pl.program_id(1) == 0)
    def _():
        ext_ref[0:halo, :] = jnp.zeros((halo, MLSTM_INNER), F32)

    @pl.when(pl.program_id(1) != 0)
    def _():
        ext_ref[0:halo, :] = ext_ref[tm:tm + halo, :]

    xm = _dot(xb, w_in_ref[:, :MLSTM_INNER])
    ext_ref[halo:halo + tm, :] = xm
    conv = cb_ref[...] + cw_ref[MLSTM_CONV - 1:MLSTM_CONV, :] * xm
    for kk in range(MLSTM_CONV - 1):
        lag = MLSTM_CONV - 1 - kk
        conv = conv + cw_ref[kk:kk + 1, :] * ext_ref[halo - lag:halo - lag + tm, :]
    xc = _silu(conv)
    xc_b = xc.astype(BF16)
    xm_b = xm.astype(BF16)
    xc_ref[...] = xc_b
    sz_ref[...] = _silu(_dot(xb, w_in_ref[:, MLSTM_INNER:2 * MLSTM_INNER])).astype(BF16)

    gates = bgate_ref[...] + _dot(xb, w_in_ref[:, 2 * MLSTM_INNER:]) + _dot(xc_b, wgate_ref[...])
    k_scale = MLSTM_HD ** -0.5
    for c in range(MLSTM_INNER // MXU_DIM):
        lo, hi = c * MXU_DIM, (c + 1) * MXU_DIM
        qk = _dot(xc_b[:, lo:hi], bdqk_ref[c])
        q_ref[:, lo:hi] = qk[:, :MXU_DIM].astype(BF16)
        k_ref[:, lo:hi] = (qk[:, MXU_DIM:] * k_scale).astype(BF16)
        v_ref[:, lo:hi] = _dot(xm_b[:, lo:hi], bdv_ref[c]).astype(BF16)
    fg = gates[:, LANES:]
    gate_ref[:, :LANES] = gates[:, :LANES]
    gate_ref[:, LANES:] = jnp.minimum(fg, 0.0) - jnp.log(1.0 + jnp.exp(-jnp.abs(fg)))


def _mlstm_pre(x, w, batch, seq, tm):
    t = x.shape[0]
    ns = seq // tm
    row = lambda n: pl.BlockSpec((tm, n), lambda b, s: (b * ns + s, 0))
    full = lambda a: pl.BlockSpec(a.shape, lambda b, s: (0,) * a.ndim)
    consts = (w["w_in"], w["conv_w"], w["conv_b"], w["bd_qk"], w["bd_v"], w["w_gate"], w["b_gate"])
    act = jax.ShapeDtypeStruct((t, MLSTM_INNER), BF16)
    return pl.pallas_call(
        functools.partial(_mlstm_pre_kernel, tm=tm),
        grid=(batch, ns),
        in_specs=[row(D_MODEL)] + [full(a) for a in consts],
        out_specs=[row(MLSTM_INNER)] * 5 + [row(2 * LANES)],
        out_shape=[act] * 5 + [jax.ShapeDtypeStruct((t, 2 * LANES), F32)],
        scratch_shapes=[pltpu.VMEM((tm + 8, MLSTM_INNER), F32)],
        compiler_params=_params("parallel", "arbitrary"),
        name="mlstm_pre",
    )(x, *consts)


def _split3(v):
    a = v.astype(BF16)
    r = v - a.astype(F32)
    b = r.astype(BF16)
    c = (r - b.astype(F32)).astype(BF16)
    return a, b, c


def _mlstm_chunk_kernel(q_ref, k_ref, v_ref, gate_ref, xc_ref, sz_ref, gout_ref, skip_ref, o_ref,
                        c_ref, n_ref, m_ref, *, chunk):
    @pl.when(pl.program_id(1) == 0)
    def _():
        c_ref[...] = jnp.zeros(c_ref.shape, F32)
        n_ref[...] = jnp.zeros(n_ref.shape, F32)
        m_ref[...] = jnp.zeros(m_ref.shape, F32)

    ig_all = gate_ref[:, :LANES]
    lf_all = gate_ref[:, LANES:]
    rows = lax.broadcasted_iota(jnp.int32, (chunk, chunk), 0)
    cols = lax.broadcasted_iota(jnp.int32, (chunk, chunk), 1)
    causal = cols <= rows
    tril = jnp.where(causal, 1.0, 0.0).astype(BF16)
    b_all = functools.reduce(lambda u, w: u + w, [_dot(tril, piece) for piece in _split3(lf_all)])
    row_all = jnp.transpose(ig_all - b_all)

    for h in range(MLSTM_HEADS):
        lo, hi = h * MLSTM_HD, (h + 1) * MLSTM_HD
        q = q_ref[:, lo:hi]
        k = k_ref[:, lo:hi]
        v = v_ref[:, lo:hi]
        b = b_all[:, h:h + 1]
        ig = ig_all[:, h:h + 1]
        m_prev = m_ref[h]
        d = jnp.where(causal, b + row_all[h:h + 1, :], -jnp.inf)
        inter = b + m_prev
        m_i = jnp.maximum(inter, jnp.max(d, axis=-1, keepdims=True))
        w_intra = jnp.exp(d - m_i)
        w_inter = jnp.exp(inter - m_i)
        s = _dot_nt(q, k) * w_intra
        c_old = c_ref[h]
        n_old = n_ref[h]
        num = _dot(s.astype(BF16), v) + w_inter * _dot(q, c_old.astype(BF16))
        qn = jnp.sum(q.astype(F32) * n_old, axis=-1, keepdims=True)
        den = jnp.sum(s, axis=-1, keepdims=True) + w_inter * qn
        hh = num / jnp.maximum(jnp.abs(den), jnp.exp(-m_i))

        b_last = b[chunk - 1:chunk, :]
        gg = b_last - b + ig
        m_new = jnp.maximum(b_last + m_prev, jnp.max(gg, axis=0, keepdims=True))
        kw = k.astype(F32) * jnp.exp(gg - m_new)
        decay = jnp.exp(b_last + m_prev - m_new)
        c_ref[h] = decay * c_old + _dot_tn(kw.astype(BF16), v)
        n_ref[h] = decay * n_old + jnp.sum(kw, axis=0, keepdims=True)
        m_ref[h] = m_new

        mu = jnp.mean(hh, axis=-1, keepdims=True)
        cen = hh - mu
        var = jnp.mean(cen * cen, axis=-1, keepdims=True)
        hn = cen * lax.rsqrt(var + LN_EPS) * gout_ref[:, lo:hi]
        o_ref[:, lo:hi] = ((hn + skip_ref[:, lo:hi] * xc_ref[:, lo:hi].astype(F32))
                           * sz_ref[:, lo:hi].astype(F32)).astype(o_ref.dtype)


def _mlstm_chunks(q, k, v, gates, xc, sz, g_out, skip, batch, seq, chunk):
    t = q.shape[0]
    nc = seq // chunk
    row = lambda n: pl.BlockSpec((chunk, n), lambda b, c: (b * nc + c, 0))
    full = lambda a: pl.BlockSpec(a.shape, lambda b, c: (0,) * a.ndim)
    return pl.pallas_call(
        functools.partial(_mlstm_chunk_kernel, chunk=chunk),
        grid=(batch, nc),
        in_specs=[row(MLSTM_INNER)] * 3 + [row(2 * LANES)] + [row(MLSTM_INNER)] * 2
                 + [full(g_out), full(skip)],
        out_specs=row(MLSTM_INNER),
        out_shape=jax.ShapeDtypeStruct((t, MLSTM_INNER), BF16),
        scratch_shapes=[pltpu.VMEM((MLSTM_HEADS, MLSTM_HD, MLSTM_HD), F32),
                        pltpu.VMEM((MLSTM_HEADS, 1, MLSTM_HD), F32),
                        pltpu.VMEM((MLSTM_HEADS, 1, 1), F32)],
        compiler_params=_params("parallel", "arbitrary"),
        name="mlstm_chunks",
    )(q, k, v, gates, xc, sz, g_out, skip)


def _rot_half(w):
    half = MLA_ROPE // 2
    return jnp.concatenate([-w[..., half:], w[..., :half]], axis=-1)


def _prep_mla(w_in, g_q, w_uq, g_kv, w_ukv):
    pad = jnp.zeros((D_MODEL, LANES - MLA_ROPE), F32)
    w_pe = w_in[:, MLA_Q_RANK + MLA_KV_RANK:]
    w_in_p = jnp.concatenate([w_in[:, :MLA_Q_RANK + MLA_KV_RANK], w_pe, pad, _rot_half(w_pe), pad], axis=1)
    uq = w_uq.reshape(MLA_Q_RANK, MLA_HEADS, MLA_NOPE + MLA_ROPE)
    zq = jnp.zeros((MLA_Q_RANK, MLA_HEADS, LANES - MLA_ROPE), F32)
    w_q1 = jnp.concatenate([uq, zq], axis=-1).reshape(MLA_Q_RANK, MLA_HEADS * HEAD_SLOT)
    w_q2 = jnp.concatenate([_rot_half(uq[..., MLA_NOPE:]), zq], axis=-1).reshape(MLA_Q_RANK, MLA_HEADS * LANES)
    ukv = w_ukv.reshape(MLA_KV_RANK, MLA_HEADS, MLA_NOPE + MLA_V)
    w_k = ukv[..., :MLA_NOPE].reshape(MLA_KV_RANK, -1)
    w_vt = ukv[..., MLA_NOPE:].reshape(MLA_KV_RANK, -1).T
    return dict(w_in=w_in_p.astype(BF16), g_q=g_q.reshape(1, -1), g_kv=g_kv.reshape(1, -1),
                w_q1=w_q1.astype(BF16), w_q2=w_q2.astype(BF16), w_k=w_k.astype(BF16),
                w_vt=w_vt.astype(BF16))


def _block_diag_tiles(w):
    per = MXU_DIM // MLSTM_QKV_BLOCK
    w = w.reshape(MLSTM_INNER // MXU_DIM, per, MLSTM_QKV_BLOCK, MLSTM_QKV_BLOCK)
    eye = jnp.eye(per, dtype=w.dtype)
    return jnp.einsum("cgij,gh->cgihj", w, eye).reshape(MLSTM_INNER // MXU_DIM, MXU_DIM, MXU_DIM)


def _prep_mlstm(w_in, conv_w, conv_b, w_q, w_k, w_v, w_ig, b_ig, w_fg, b_fg):
    bd_qk = jnp.concatenate([_block_diag_tiles(w_q), _block_diag_tiles(w_k)], axis=-1)
    zw = jnp.zeros((3 * MLSTM_INNER, LANES - MLSTM_HEADS), F32)
    zb = jnp.zeros((LANES - MLSTM_HEADS,), F32)
    w_gate = jnp.concatenate([w_ig, zw, w_fg, zw], axis=1)
    b_gate = jnp.concatenate([b_ig, zb, b_fg, zb]).reshape(1, 2 * LANES)
    blocks = lambda a: a.reshape(MLSTM_INNER // MLSTM_QKV_BLOCK, MLSTM_QKV_BLOCK, 2 * LANES)
    fold = lambda bd, a: jnp.einsum("gij,gjn->gin", bd, blocks(a), precision=lax.Precision.HIGHEST)
    gate_q, gate_k, gate_v = (w_gate[n * MLSTM_INNER:(n + 1) * MLSTM_INNER] for n in range(3))
    w_gate_xc = (fold(w_q, gate_q) + fold(w_k, gate_k)).reshape(MLSTM_INNER, 2 * LANES)
    w_gate_x = jnp.dot(w_in[:, :MLSTM_INNER], fold(w_v, gate_v).reshape(MLSTM_INNER, 2 * LANES),
                       precision=lax.Precision.HIGHEST)
    return dict(w_in=jnp.concatenate([w_in, w_gate_x], axis=1).astype(BF16), conv_w=conv_w,
                conv_b=conv_b.reshape(1, -1), bd_qk=bd_qk.astype(BF16),
                bd_v=_block_diag_tiles(w_v).astype(BF16), w_gate=w_gate_xc.astype(BF16), b_gate=b_gate)


def kernel(x, positions, router_w, router_b, mla_w_in, mla_g_q, mla_w_uq, mla_g_kv, mla_w_ukv, mla_w_o, mlstm_w_in, mlstm_conv_w, mlstm_conv_b, mlstm_w_q, mlstm_w_k, mlstm_w_v, mlstm_w_ig, mlstm_b_ig, mlstm_w_fg, mlstm_b_fg, mlstm_g_out, mlstm_skip, mlstm_w_out, moe_w_gate, moe_w_up, moe_w_down, ln_g, ln_b):
    batch, seq, _ = x.shape
    tiles = _tiles(seq)
    rwt = router_w.T.astype(BF16)
    rb = router_b.reshape(N_EXPERTS, 1)
    rw_pad = jnp.concatenate([router_w, jnp.zeros((D_MODEL, LANES - N_EXPERTS), F32)], axis=1).astype(BF16)
    t = batch * seq
    cos, sin = _rope_tables(positions, tiles["rope"])
    xt = x.reshape(t, D_MODEL)
    for i in range(DEPTH):
        j = i // N_MIXERS
        g0, b0 = ln_g[i, 0].reshape(1, -1), ln_b[i, 0].reshape(1, -1)
        g1, b1 = ln_g[i, 1].reshape(1, -1), ln_b[i, 1].reshape(1, -1)
        if i % N_MIXERS == 0:
            w = _prep_mla(mla_w_in[j], mla_g_q[j], mla_w_uq[j], mla_g_kv[j], mla_w_ukv[j])
            w_out = mla_w_o[j].astype(BF16)
        else:
            w = _prep_mlstm(mlstm_w_in[j], mlstm_conv_w[j], mlstm_conv_b[j], mlstm_w_q[j], mlstm_w_k[j],
                            mlstm_w_v[j], mlstm_w_ig[j], mlstm_b_ig[j], mlstm_w_fg[j], mlstm_b_fg[j])
            w_out = mlstm_w_out[j].astype(BF16)
        if i % N_MIXERS == 0:
            q, k, vt = _mla_proj(xt, cos, sin, w, tiles["attn"])
            a = _attention(q, k, vt, batch, seq, tiles["attn"])
        else:
            q, k, v, xc, sz, gates = _mlstm_pre(xt, w, batch, seq, tiles["pre"])
            a = _mlstm_chunks(q, k, v, gates, xc, sz, mlstm_g_out[j].reshape(1, -1),
                              mlstm_skip[j].reshape(1, -1), batch, seq, tiles["chunk"])
        x1, cls = _post(a, w_out, xt, g0, b0, rwt, rb, tiles["post"])
        xt = _moe(x1, cls.reshape(t), rw_pad, i, moe_w_gate, moe_w_up, moe_w_down, g1, b1, tiles["moe"])
    return xt.reshape(batch, seq, D_MODEL)
```

```python
import functools

import jax
import jax.numpy as jnp
from jax import lax
from jax.experimental import pallas as pl
from jax.experimental.pallas import tpu as pltpu
from jax.experimental.pallas import tpu_sc as plsc

D_MODEL = 1024
DEPTH = 4
N_MIXERS = 2
MLA_HEADS = 8
MLA_Q_RANK = 256
MLA_KV_RANK = 256
MLA_NOPE = 128
MLA_ROPE = 64
MLA_V = 128
ROPE_THETA = 10000.0
MLSTM_INNER = 2 * D_MODEL
MLSTM_HEADS = 4
MLSTM_HD = MLSTM_INNER // MLSTM_HEADS
MLSTM_CONV = 4
MLSTM_QKV_BLOCK = 4
N_EXPERTS = 16
N_GROUPS = 4
EXPERTS_PER_GROUP = N_EXPERTS // N_GROUPS
EXPERT_FF = 512
DN_ALPHA = (2.0 * DEPTH) ** 0.25
LN_EPS = 1e-5
LOG2_E = 1.4426950408889634

LANES = 128
MXU_DIM = 256
HEAD_SLOT = 2 * LANES
ATTN_HEADS_PER_STEP = 4
N_PAIRS = EXPERTS_PER_GROUP * (EXPERTS_PER_GROUP - 1) // 2
N_CLASSES = N_GROUPS * N_PAIRS
SC_WORKERS = 32
SC_MAX_ROWS = 48
MOE_SPLIT = 4
MOE_SUB = 256
POST_SUB = 512
VMEM_LIMIT = 56 * 1024 * 1024

F32 = jnp.float32
BF16 = jnp.bfloat16


def _tiles(seq):
    return dict(
        rope=min(seq, 2048),
        attn=min(seq, 512),
        post=min(seq, 1024),
        pre=min(seq, 256),
        chunk=min(seq, 256),
        moe=512,
    )


def _params(*sem):
    return pltpu.CompilerParams(dimension_semantics=sem, vmem_limit_bytes=VMEM_LIMIT)


def _dot(a, b):
    return jnp.dot(a, b, preferred_element_type=F32)


def _dot_nt(a, b):
    return lax.dot_general(a, b, (((1,), (1,)), ((), ())), preferred_element_type=F32)


def _dot_tn(a, b):
    return lax.dot_general(a, b, (((0,), (0,)), ((), ())), preferred_element_type=F32)


def _layer_norm(v, g, b):
    mu = jnp.mean(v, axis=-1, keepdims=True)
    c = v - mu
    var = jnp.mean(c * c, axis=-1, keepdims=True)
    return c * lax.rsqrt(var + LN_EPS) * g + b


def _rms_norm(v, g):
    return v * lax.rsqrt(jnp.mean(v * v, axis=-1, keepdims=True) + LN_EPS) * g


def _sigmoid(v):
    return 1.0 / (1.0 + jnp.exp(-v))


def _silu(v):
    return v * _sigmoid(v)


def _rope_kernel(pos_ref, invf_ref, cos_ref, sin_ref):
    ang = pos_ref[...].astype(F32) * invf_ref[...]
    cos_ref[...] = jnp.cos(ang)
    sin_ref[...] = jnp.sin(ang)


def _rope_tables(positions, tm):
    t = positions.size
    inv_freq = 1.0 / (ROPE_THETA ** (jnp.arange(0, MLA_ROPE, 2, dtype=F32) / MLA_ROPE))
    invf = jnp.tile(inv_freq, LANES // (MLA_ROPE // 2)).reshape(1, LANES)
    out = jax.ShapeDtypeStruct((t, LANES), F32)
    return pl.pallas_call(
        _rope_kernel,
        grid=(t // tm,),
        in_specs=[pl.BlockSpec((tm, 1), lambda i: (i, 0)),
                  pl.BlockSpec((1, LANES), lambda i: (0, 0))],
        out_specs=[pl.BlockSpec((tm, LANES), lambda i: (i, 0))] * 2,
        out_shape=[out, out],
        compiler_params=_params("parallel"),
        name="rope_tables",
    )(positions.reshape(t, 1), invf)


def _mla_proj_kernel(x_ref, cos_ref, sin_ref, w_in_ref, gq_ref, gkv_ref, wq1_ref, wq2_ref,
                     wk_ref, wvt_ref, q_ref, k_ref, vt_ref):
    xb = x_ref[...].astype(BF16)
    c = _dot(xb, w_in_ref[...])
    cq = _rms_norm(c[:, :MLA_Q_RANK], gq_ref[...]).astype(BF16)
    ckv = _rms_norm(c[:, MLA_Q_RANK:MLA_Q_RANK + MLA_KV_RANK], gkv_ref[...]).astype(BF16)
    cos = cos_ref[...]
    sin = sin_ref[...]
    base = MLA_Q_RANK + MLA_KV_RANK
    k_pe = (c[:, base:base + LANES] * cos + c[:, base + LANES:base + 2 * LANES] * sin).astype(BF16)
    scale = (MLA_NOPE + MLA_ROPE) ** -0.5 * LOG2_E
    cos_s = cos * scale
    sin_s = sin * scale
    q1 = _dot(cq, wq1_ref[...])
    q2 = _dot(cq, wq2_ref[...])
    kn = _dot(ckv, wk_ref[...])
    for h in range(MLA_HEADS):
        s0 = h * HEAD_SLOT
        q_ref[:, s0:s0 + LANES] = (q1[:, s0:s0 + LANES] * scale).astype(BF16)
        q_ref[:, s0 + LANES:s0 + HEAD_SLOT] = (
            q1[:, s0 + LANES:s0 + HEAD_SLOT] * cos_s
            + q2[:, h * LANES:(h + 1) * LANES] * sin_s).astype(BF16)
        k_ref[:, s0:s0 + LANES] = kn[:, h * LANES:(h + 1) * LANES].astype(BF16)
        k_ref[:, s0 + LANES:s0 + HEAD_SLOT] = k_pe
    vt_ref[...] = _dot_nt(wvt_ref[...], ckv).astype(BF16)


def _mla_proj(x, cos, sin, w, tm):
    t = x.shape[0]
    row = lambda n: pl.BlockSpec((tm, n), lambda i: (i, 0))
    full = lambda a: pl.BlockSpec(a.shape, lambda i: (0,) * a.ndim)
    consts = (w["w_in"], w["g_q"], w["g_kv"], w["w_q1"], w["w_q2"], w["w_k"], w["w_vt"])
    return pl.pallas_call(
        _mla_proj_kernel,
        grid=(t // tm,),
        in_specs=[row(D_MODEL), row(LANES), row(LANES)] + [full(a) for a in consts],
        out_specs=[row(MLA_HEADS * HEAD_SLOT), row(MLA_HEADS * HEAD_SLOT),
                   pl.BlockSpec((None, MLA_HEADS * MLA_V, tm), lambda i: (i, 0, 0))],
        out_shape=[jax.ShapeDtypeStruct((t, MLA_HEADS * HEAD_SLOT), BF16),
                   jax.ShapeDtypeStruct((t, MLA_HEADS * HEAD_SLOT), BF16),
                   jax.ShapeDtypeStruct((t // tm, MLA_HEADS * MLA_V, tm), BF16)],
        compiler_params=_params("parallel"),
        name="mla_proj",
    )(x, cos, sin, *consts)


def _attn_kernel(q_ref, k_ref, vt_ref, o_ref, s_ref, m_ref, l_ref, acc_ref, *, tile):
    qi = pl.program_id(2)
    m_ref[...] = jnp.full(m_ref.shape, -jnp.inf, F32)
    l_ref[...] = jnp.zeros(l_ref.shape, F32)
    acc_ref[...] = jnp.zeros(acc_ref.shape, F32)
    heads = range(ATTN_HEADS_PER_STEP)

    def scores(ki, buf):
        start = pl.multiple_of(ki * tile, tile)
        for g in heads:
            slot = slice(g * HEAD_SLOT, (g + 1) * HEAD_SLOT)
            s_ref[buf, g] = _dot_nt(k_ref[pl.ds(start, tile), slot], q_ref[:, slot])

    def absorb(ki, buf, masked):
        for g in heads:
            s = s_ref[buf, g]
            if masked:
                keys = lax.broadcasted_iota(jnp.int32, s.shape, 0)
                queries = lax.broadcasted_iota(jnp.int32, s.shape, 1)
                s = jnp.where(keys <= queries, s, -jnp.inf)
            m_old = m_ref[g]
            m_new = jnp.maximum(m_old, jnp.max(s, axis=0, keepdims=True))
            p = jnp.exp2(s - m_new)
            alpha = jnp.exp2(m_old - m_new)
            l_ref[g] = alpha * l_ref[g] + jnp.sum(p, axis=0, keepdims=True)
            acc_ref[g] = alpha * acc_ref[g] + _dot(vt_ref[ki, g * MLA_V:(g + 1) * MLA_V, :], p.astype(BF16))
            m_ref[g] = m_new

    scores(0, 0)

    def body(j, carry):
        scores(2 * j + 1, 1)
        absorb(2 * j, 0, False)
        scores(2 * j + 2, 0)
        absorb(2 * j + 1, 1, False)
        return carry

    lax.fori_loop(0, qi // 2, body, 0)

    @pl.when(qi % 2 == 1)
    def _():
        scores(qi, 1)
        absorb(qi - 1, 0, False)
        absorb(qi, 1, True)

    @pl.when(qi % 2 == 0)
    def _():
        absorb(qi, 0, True)

    for g in heads:
        o_ref[:, g * MLA_V:(g + 1) * MLA_V] = jnp.transpose(acc_ref[g] / l_ref[g]).astype(o_ref.dtype)


def _attention(q, k, vt, batch, seq, tile):
    t = q.shape[0]
    nq = seq // tile
    hps = ATTN_HEADS_PER_STEP
    return pl.pallas_call(
        functools.partial(_attn_kernel, tile=tile),
        grid=(batch, MLA_HEADS // hps, nq),
        in_specs=[pl.BlockSpec((tile, hps * HEAD_SLOT), lambda b, h, i: (b * nq + i, h)),
                  pl.BlockSpec((seq, hps * HEAD_SLOT), lambda b, h, i: (b, h)),
                  pl.BlockSpec((nq, hps * MLA_V, tile), lambda b, h, i: (b, h, 0))],
        out_specs=pl.BlockSpec((tile, hps * MLA_V), lambda b, h, i: (b * nq + i, h)),
        out_shape=jax.ShapeDtypeStruct((t, MLA_HEADS * MLA_V), BF16),
        scratch_shapes=[pltpu.VMEM((2, hps, tile, tile), F32),
                        pltpu.VMEM((hps, 1, tile), F32), pltpu.VMEM((hps, 1, tile), F32),
                        pltpu.VMEM((hps, MLA_V, tile), F32)],
        compiler_params=_params("parallel", "parallel", "arbitrary"),
        name="mla_attention",
    )(q, k, vt)


def _route_rows(sel):
    best = None
    for g in range(N_GROUPS):
        a = [sel[g * EXPERTS_PER_GROUP + j:g * EXPERTS_PER_GROUP + j + 1, :]
             for j in range(EXPERTS_PER_GROUP)]
        v1 = functools.reduce(jnp.maximum, a)
        i1 = jnp.full(v1.shape, EXPERTS_PER_GROUP - 1, jnp.int32)
        for j in reversed(range(EXPERTS_PER_GROUP - 1)):
            i1 = jnp.where(a[j] == v1, j, i1)
        r = [jnp.where(i1 == j, -jnp.inf, a[j]) for j in range(EXPERTS_PER_GROUP)]
        v2 = functools.reduce(jnp.maximum, r)
        i2 = jnp.full(v1.shape, EXPERTS_PER_GROUP - 1, jnp.int32)
        for j in reversed(range(EXPERTS_PER_GROUP - 1)):
            i2 = jnp.where(r[j] == v2, j, i2)
        total = v1 + v2
        if best is None:
            best = (total, jnp.zeros_like(i1), i1, i2)
        else:
            better = total > best[0]
            best = (jnp.where(better, total, best[0]), jnp.where(better, g, best[1]),
                    jnp.where(better, i1, best[2]), jnp.where(better, i2, best[3]))
    _, grp, i1, i2 = best
    lo = jnp.minimum(i1, i2)
    hi = jnp.maximum(i1, i2)
    pair = jnp.where(lo == 0, hi - 1, jnp.where(lo == 1, hi + 1, N_PAIRS - 1))
    return grp * N_PAIRS + pair


def _post_kernel(a_ref, w_ref, x_ref, g_ref, b_ref, rwt_ref, rb_ref, y_ref, cls_ref):
    subs = [slice(r, r + POST_SUB) for r in range(0, a_ref.shape[0], POST_SUB)]
    y = _dot(a_ref[subs[0], :], w_ref[...])
    for n, rows in enumerate(subs):
        y_next = _dot(a_ref[subs[n + 1], :], w_ref[...]) if n + 1 < len(subs) else None
        x1 = _layer_norm(DN_ALPHA * x_ref[rows, :] + y, g_ref[...], b_ref[...])
        y_ref[rows, :] = x1
        logits = _dot_nt(rwt_ref[...], x1.astype(BF16))
        cls_ref[:, rows] = _route_rows(_sigmoid(logits) + rb_ref[...])
        y = y_next


def _post(a, w, x, g, b, rwt, rb, tm):
    t, k = a.shape
    row = lambda n: pl.BlockSpec((tm, n), lambda i: (i, 0))
    full = lambda arr: pl.BlockSpec(arr.shape, lambda i: (0,) * arr.ndim)
    return pl.pallas_call(
        _post_kernel,
        grid=(t // tm,),
        in_specs=[row(k), full(w), row(D_MODEL), full(g), full(b), full(rwt), full(rb)],
        out_specs=[row(D_MODEL), pl.BlockSpec((1, tm), lambda i: (0, i))],
        out_shape=[jax.ShapeDtypeStruct((t, D_MODEL), F32),
                   jax.ShapeDtypeStruct((1, t), jnp.int32)],
        compiler_params=_params("parallel"),
        name="mixer_out_norm_route",
    )(a, w, x, g, b, rwt, rb)


def _expert_kernel(ea_ref, eb_ref, fresh_ref, nblk_ref, x_ref, rw_ref, wga_ref, wua_ref, wda_ref,
                   wgb_ref, wub_ref, wdb_ref, g_ref, b_ref, prev_ref, o_ref, wga, wua, wda, wgb, wub, wdb):
    del prev_ref
    i = pl.program_id(0)

    @pl.when(fresh_ref[i] == 1)
    def _():
        for src, dst in ((wga_ref, wga), (wua_ref, wua), (wda_ref, wda),
                         (wgb_ref, wgb), (wub_ref, wub), (wdb_ref, wdb)):
            dst[...] = src[...].astype(BF16)

    @pl.when(i < nblk_ref[0])
    def _():
        lane = lax.broadcasted_iota(jnp.int32, (MOE_SUB, LANES), 1)
        subs = [slice(r, r + MOE_SUB) for r in range(0, x_ref.shape[0], MOE_SUB)]

        def up_dots(rows):
            xb = x_ref[rows, :].astype(BF16)
            return (_dot(xb, rw_ref[...]), _dot(xb, wga[...]), _dot(xb, wua[...]),
                    _dot(xb, wgb[...]), _dot(xb, wub[...]))

        cur = up_dots(subs[0])
        for n, rows in enumerate(subs):
            nxt = up_dots(subs[n + 1]) if n + 1 < len(subs) else None
            logits, ga, ua, gb, ub = cur
            sig = _sigmoid(logits)
            sa = jnp.sum(jnp.where(lane == ea_ref[i], sig, 0.0), axis=-1, keepdims=True)
            sb = jnp.sum(jnp.where(lane == eb_ref[i], sig, 0.0), axis=-1, keepdims=True)
            tot = sa + sb
            ha = (_silu(ga) * ua).astype(BF16)
            hb = (_silu(gb) * ub).astype(BF16)
            y = (sa / tot) * _dot(ha, wda[...]) + (sb / tot) * _dot(hb, wdb[...])
            o_ref[rows, :] = _layer_norm(DN_ALPHA * x_ref[rows, :] + y, g_ref[...], b_ref[...])
            cur = nxt

    @pl.when(i >= nblk_ref[0])
    def _():
        o_ref[...] = jnp.zeros(o_ref.shape, o_ref.dtype)


def _experts(xs, ea, eb, fresh, nblk, rw, layer, wg, wu, wd, g, b, bm, blk0, total_rows, prev):
    p = xs.shape[0]
    sel = lambda which: (lambda i, ea, eb, fr, nb: (layer, (ea if which == 0 else eb)[i], 0, 0))
    up = lambda which: pl.BlockSpec((None, None, D_MODEL, EXPERT_FF), sel(which))
    down = lambda which: pl.BlockSpec((None, None, EXPERT_FF, D_MODEL), sel(which))
    full = lambda arr: pl.BlockSpec(arr.shape, lambda i, ea, eb, fr, nb: (0,) * arr.ndim)
    row = pl.BlockSpec((bm, D_MODEL), lambda i, ea, eb, fr, nb: (i, 0))
    out_row = pl.BlockSpec((bm, D_MODEL), lambda i, ea, eb, fr, nb: (i + blk0, 0))
    up_s = pltpu.VMEM((D_MODEL, EXPERT_FF), BF16)
    down_s = pltpu.VMEM((EXPERT_FF, D_MODEL), BF16)
    n_prefetch = 4
    if prev is None:
        prev = jnp.zeros((8, LANES), F32)
        aliases = {}
    else:
        aliases = {n_prefetch + 10: 0}
    return pl.pallas_call(
        _expert_kernel,
        grid_spec=pltpu.PrefetchScalarGridSpec(
            num_scalar_prefetch=n_prefetch,
            grid=(p // bm,),
            in_specs=[row, full(rw), up(0), up(0), down(0), up(1), up(1), down(1), full(g), full(b),
                      pl.BlockSpec(memory_space=pl.ANY)],
            out_specs=out_row,
            scratch_shapes=[up_s, up_s, down_s, up_s, up_s, down_s]),
        out_shape=jax.ShapeDtypeStruct((total_rows, D_MODEL), F32),
        input_output_aliases=aliases,
        compiler_params=_params("arbitrary"),
        name="moe_experts",
    )(ea, eb, fresh, nblk, xs, rw, wg, wu, wd, wg, wu, wd, g, b, prev)


def _row_gather(src, idx):
    m = idx.shape[0]
    d = src.shape[1]
    per = m // SC_WORKERS
    assert per * SC_WORKERS == m
    sc_rows = next(r for r in range(SC_MAX_ROWS, 0, -8) if per % r == 0 and (per // r) % 2 == 0)
    n = per // sc_rows
    mesh = plsc.VectorSubcoreMesh(core_axis_name="core", subcore_axis_name="subcore")

    def body(src_hbm, idx_hbm, out_hbm, idx_v, buf, gsem, psem):
        worker = lax.axis_index("core") * (SC_WORKERS // 2) + lax.axis_index("subcore")
        base = worker * per
        pltpu.sync_copy(idx_hbm.at[pl.ds(base, per)], idx_v)

        def gather(c, slot):
            return pltpu.make_async_copy(src_hbm.at[idx_v.at[pl.ds(c * sc_rows, sc_rows)]],
                                         buf.at[slot], gsem.at[slot])

        def put(c, slot):
            return pltpu.make_async_copy(buf.at[slot], out_hbm.at[pl.ds(base + c * sc_rows, sc_rows)],
                                         psem.at[slot])

        gather(0, 0).start()

        @pl.loop(0, n // 2)
        def _(j):
            for slot in range(2):
                c = 2 * j + slot
                gather(c, slot).wait()
                put(c, slot).start()

                @pl.when(c >= 1)
                def _():
                    put(c - 1, 1 - slot).wait()

                @pl.when(c + 1 < n)
                def _():
                    gather(c + 1, 1 - slot).start()

        put(n - 1, 1).wait()

    return pl.kernel(
        body,
        out_type=jax.ShapeDtypeStruct((m, d), src.dtype),
        mesh=mesh,
        scratch_types=[pltpu.VMEM((per,), jnp.int32), pltpu.VMEM((2, sc_rows, d), src.dtype),
                       pltpu.SemaphoreType.DMA((2,)), pltpu.SemaphoreType.DMA((2,))],
    )(src, idx)


def _moe(x1, cls, rw, layer, wg, wu, wd, g, b, bm):
    t = x1.shape[0]
    slice_rows = bm * MOE_SPLIT
    p_rows = -(-(t + N_CLASSES * bm) // slice_rows) * slice_rows
    nblk = p_rows // bm
    classes = jnp.arange(N_CLASSES, dtype=jnp.int32)
    counts = jnp.sum((cls[:, None] == classes[None, :]).astype(jnp.int32), axis=0)
    padded = ((counts + bm - 1) // bm) * bm
    off = jnp.cumsum(counts) - counts
    pend = jnp.cumsum(padded)
    poff = pend - padded
    tok = jnp.arange(t, dtype=jnp.int32)
    cls_sorted, order = lax.sort((cls, tok), num_keys=1)
    shift = jnp.sum(jnp.where(cls_sorted[:, None] == classes[None, :], (poff - off)[None, :], 0), axis=1)
    _, dest = lax.sort((order, shift + tok), num_keys=1)
    blk_start = jnp.arange(nblk, dtype=jnp.int32) * bm
    blk_cls = jnp.minimum(jnp.sum((blk_start[:, None] >= pend[None, :]).astype(jnp.int32), axis=1),
                          N_CLASSES - 1)
    n_used = (pend[-1] // bm).astype(jnp.int32).reshape(1)
    j = jnp.arange(bm, dtype=jnp.int32)[None, :] + (blk_start - poff[blk_cls])[:, None]
    rank = off[blk_cls][:, None] + jnp.minimum(j, counts[blk_cls][:, None] - 1)
    src = order[jnp.clip(rank, 0, t - 1).reshape(p_rows)]
    grp = blk_cls // N_PAIRS
    pair = blk_cls % N_PAIRS
    lo = jnp.where(pair < 3, 0, jnp.where(pair < 5, 1, 2))
    hi = jnp.where(pair < 3, pair + 1, jnp.where(pair < 5, pair - 1, 3))
    ea = (grp * EXPERTS_PER_GROUP + lo).astype(jnp.int32)
    eb = (grp * EXPERTS_PER_GROUP + hi).astype(jnp.int32)
    fresh = jnp.concatenate([jnp.ones((1,), jnp.int32), (blk_cls[1:] != blk_cls[:-1]).astype(jnp.int32)])
    nb = nblk // MOE_SPLIT
    xs = [_row_gather(x1, src[k * nb * bm:(k + 1) * nb * bm]) for k in range(MOE_SPLIT)]
    ys = None
    for k in range(MOE_SPLIT):
        part = slice(k * nb, (k + 1) * nb)
        fresh_k = jnp.concatenate([jnp.ones((1,), jnp.int32), fresh[part][1:]])
        ys = _experts(xs[k], ea[part], eb[part], fresh_k, n_used - k * nb, rw, layer, wg, wu, wd, g, b, bm,
                      k * nb, p_rows, ys)
    return _row_gather(ys, dest)


def _mlstm_pre_kernel(x_ref, w_in_ref, cw_ref, cb_ref, bdqk_ref, bdv_ref, wgate_ref, bgate_ref,
                      q_ref, k_ref, v_ref, xc_ref, sz_ref, gate_ref, ext_ref, *, tm):
    halo = 8
    xb = x_ref[...].astype(BF16)

    @pl.when(pl.program_id(1) == 0)
    def _():
        ext_ref[0:halo, :] = jnp.zeros((halo, MLSTM_INNER), F32)

    @pl.when(pl.program_id(1) != 0)
    def _():
        ext_ref[0:halo, :] = ext_ref[tm:tm + halo, :]

    xm = _dot(xb, w_in_ref[:, :MLSTM_INNER])
    ext_ref[halo:halo + tm, :] = xm
    conv = cb_ref[...] + cw_ref[MLSTM_CONV - 1:MLSTM_CONV, :] * xm
    for kk in range(MLSTM_CONV - 1):
        lag = MLSTM_CONV - 1 - kk
        conv = conv + cw_ref[kk:kk + 1, :] * ext_ref[halo - lag:halo - lag + tm, :]
    xc = _silu(conv)
    xc_b = xc.astype(BF16)
    xm_b = xm.astype(BF16)
    xc_ref[...] = xc_b
    sz_ref[...] = _silu(_dot(xb, w_in_ref[:, MLSTM_INNER:2 * MLSTM_INNER])).astype(BF16)

    gates = bgate_ref[...] + _dot(xb, w_in_ref[:, 2 * MLSTM_INNER:]) + _dot(xc_b, wgate_ref[...])
    k_scale = MLSTM_HD ** -0.5
    for c in range(MLSTM_INNER // MXU_DIM):
        lo, hi = c * MXU_DIM, (c + 1) * MXU_DIM
        qk = _dot(xc_b[:, lo:hi], bdqk_ref[c])
        q_ref[:, lo:hi] = qk[:, :MXU_DIM].astype(BF16)
        k_ref[:, lo:hi] = (qk[:, MXU_DIM:] * k_scale).astype(BF16)
        v_ref[:, lo:hi] = _dot(xm_b[:, lo:hi], bdv_ref[c]).astype(BF16)
    fg = gates[:, LANES:]
    gate_ref[:, :LANES] = gates[:, :LANES]
    gate_ref[:, LANES:] = jnp.minimum(fg, 0.0) - jnp.log(1.0 + jnp.exp(-jnp.abs(fg)))


def _mlstm_pre(x, w, batch, seq, tm):
    t = x.shape[0]
    ns = seq // tm
    row = lambda n: pl.BlockSpec((tm, n), lambda b, s: (b * ns + s, 0))
    full = lambda a: pl.BlockSpec(a.shape, lambda b, s: (0,) * a.ndim)
    consts = (w["w_in"], w["conv_w"], w["conv_b"], w["bd_qk"], w["bd_v"], w["w_gate"], w["b_gate"])
    act = jax.ShapeDtypeStruct((t, MLSTM_INNER), BF16)
    return pl.pallas_call(
        functools.partial(_mlstm_pre_kernel, tm=tm),
        grid=(batch, ns),
        in_specs=[row(D_MODEL)] + [full(a) for a in consts],
        out_specs=[row(MLSTM_INNER)] * 5 + [row(2 * LANES)],
        out_shape=[act] * 5 + [jax.ShapeDtypeStruct((t, 2 * LANES), F32)],
        scratch_shapes=[pltpu.VMEM((tm + 8, MLSTM_INNER), F32)],
        compiler_params=_params("parallel", "arbitrary"),
        name="mlstm_pre",
    )(x, *consts)


def _split3(v):
    a = v.astype(BF16)
    r = v - a.astype(F32)
    b = r.astype(BF16)
    c = (r - b.astype(F32)).astype(BF16)
    return a, b, c


def _mlstm_chunk_kernel(q_ref, k_ref, v_ref, gate_ref, xc_ref, sz_ref, gout_ref, skip_ref, o_ref,
                        c_ref, n_ref, m_ref, *, chunk):
    @pl.when(pl.program_id(1) == 0)
    def _():
        c_ref[...] = jnp.zeros(c_ref.shape, F32)
        n_ref[...] = jnp.zeros(n_ref.shape, F32)
        m_ref[...] = jnp.zeros(m_ref.shape, F32)

    ig_all = gate_ref[:, :LANES]
    lf_all = gate_ref[:, LANES:]
    rows = lax.broadcasted_iota(jnp.int32, (chunk, chunk), 0)
    cols = lax.broadcasted_iota(jnp.int32, (chunk, chunk), 1)
    causal = cols <= rows
    tril = jnp.where(causal, 1.0, 0.0).astype(BF16)
    b_all = functools.reduce(lambda u, w: u + w, [_dot(tril, piece) for piece in _split3(lf_all)])
    row_all = jnp.transpose(ig_all - b_all)

    for h in range(MLSTM_HEADS):
        lo, hi = h * MLSTM_HD, (h + 1) * MLSTM_HD
        q = q_ref[:, lo:hi]
        k = k_ref[:, lo:hi]
        v = v_ref[:, lo:hi]
        b = b_all[:, h:h + 1]
        ig = ig_all[:, h:h + 1]
        m_prev = m_ref[h]
        d = jnp.where(causal, b + row_all[h:h + 1, :], -jnp.inf)
        inter = b + m_prev
        m_i = jnp.maximum(inter, jnp.max(d, axis=-1, keepdims=True))
        w_intra = jnp.exp(d - m_i)
        w_inter = jnp.exp(inter - m_i)
        s = _dot_nt(q, k) * w_intra
        c_old = c_ref[h]
        n_old = n_ref[h]
        num = _dot(s.astype(BF16), v) + w_inter * _dot(q, c_old.astype(BF16))
        qn = jnp.sum(q.astype(F32) * n_old, axis=-1, keepdims=True)
        den = jnp.sum(s, axis=-1, keepdims=True) + w_inter * qn
        hh = num / jnp.maximum(jnp.abs(den), jnp.exp(-m_i))

        b_last = b[chunk - 1:chunk, :]
        gg = b_last - b + ig
        m_new = jnp.maximum(b_last + m_prev, jnp.max(gg, axis=0, keepdims=True))
        kw = k.astype(F32) * jnp.exp(gg - m_new)
        decay = jnp.exp(b_last + m_prev - m_new)
        c_ref[h] = decay * c_old + _dot_tn(kw.astype(BF16), v)
        n_ref[h] = decay * n_old + jnp.sum(kw, axis=0, keepdims=True)
        m_ref[h] = m_new

        mu = jnp.mean(hh, axis=-1, keepdims=True)
        cen = hh - mu
        var = jnp.mean(cen * cen, axis=-1, keepdims=True)
        hn = cen * lax.rsqrt(var + LN_EPS) * gout_ref[:, lo:hi]
        o_ref[:, lo:hi] = ((hn + skip_ref[:, lo:hi] * xc_ref[:, lo:hi].astype(F32))
                           * sz_ref[:, lo:hi].astype(F32)).astype(o_ref.dtype)


def _mlstm_chunks(q, k, v, gates, xc, sz, g_out, skip, batch, seq, chunk):
    t = q.shape[0]
    nc = seq // chunk
    row = lambda n: pl.BlockSpec((chunk, n), lambda b, c: (b * nc + c, 0))
    full = lambda a: pl.BlockSpec(a.shape, lambda b, c: (0,) * a.ndim)
    return pl.pallas_call(
        functools.partial(_mlstm_chunk_kernel, chunk=chunk),
        grid=(batch, nc),
        in_specs=[row(MLSTM_INNER)] * 3 + [row(2 * LANES)] + [row(MLSTM_INNER)] * 2
                 + [full(g_out), full(skip)],
        out_specs=row(MLSTM_INNER),
        out_shape=jax.ShapeDtypeStruct((t, MLSTM_INNER), BF16),
        scratch_shapes=[pltpu.VMEM((MLSTM_HEADS, MLSTM_HD, MLSTM_HD), F32),
                        pltpu.VMEM((MLSTM_HEADS, 1, MLSTM_HD), F32),
                        pltpu.VMEM((MLSTM_HEADS, 1, 1), F32)],
        compiler_params=_params("parallel", "arbitrary"),
        name="mlstm_chunks",
    )(q, k, v, gates, xc, sz, g_out, skip)


def _rot_half(w):
    half = MLA_ROPE // 2
    return jnp.concatenate([-w[..., half:], w[..., :half]], axis=-1)


def _prep_mla(w_in, g_q, w_uq, g_kv, w_ukv):
    pad = jnp.zeros((D_MODEL, LANES - MLA_ROPE), F32)
    w_pe = w_in[:, MLA_Q_RANK + MLA_KV_RANK:]
    w_in_p = jnp.concatenate([w_in[:, :MLA_Q_RANK + MLA_KV_RANK], w_pe, pad, _rot_half(w_pe), pad], axis=1)
    uq = w_uq.reshape(MLA_Q_RANK, MLA_HEADS, MLA_NOPE + MLA_ROPE)
    zq = jnp.zeros((MLA_Q_RANK, MLA_HEADS, LANES - MLA_ROPE), F32)
    w_q1 = jnp.concatenate([uq, zq], axis=-1).reshape(MLA_Q_RANK, MLA_HEADS * HEAD_SLOT)
    w_q2 = jnp.concatenate([_rot_half(uq[..., MLA_NOPE:]), zq], axis=-1).reshape(MLA_Q_RANK, MLA_HEADS * LANES)
    ukv = w_ukv.reshape(MLA_KV_RANK, MLA_HEADS, MLA_NOPE + MLA_V)
    w_k = ukv[..., :MLA_NOPE].reshape(MLA_KV_RANK, -1)
    w_vt = ukv[..., MLA_NOPE:].reshape(MLA_KV_RANK, -1).T
    return dict(w_in=w_in_p.astype(BF16), g_q=g_q.reshape(1, -1), g_kv=g_kv.reshape(1, -1),
                w_q1=w_q1.astype(BF16), w_q2=w_q2.astype(BF16), w_k=w_k.astype(BF16),
                w_vt=w_vt.astype(BF16))


def _block_diag_tiles(w):
    per = MXU_DIM // MLSTM_QKV_BLOCK
    w = w.reshape(MLSTM_INNER // MXU_DIM, per, MLSTM_QKV_BLOCK, MLSTM_QKV_BLOCK)
    eye = jnp.eye(per, dtype=w.dtype)
    return jnp.einsum("cgij,gh->cgihj", w, eye).reshape(MLSTM_INNER // MXU_DIM, MXU_DIM, MXU_DIM)


def _prep_mlstm(w_in, conv_w, conv_b, w_q, w_k, w_v, w_ig, b_ig, w_fg, b_fg):
    bd_qk = jnp.concatenate([_block_diag_tiles(w_q), _block_diag_tiles(w_k)], axis=-1)
    zw = jnp.zeros((3 * MLSTM_INNER, LANES - MLSTM_HEADS), F32)
    zb = jnp.zeros((LANES - MLSTM_HEADS,), F32)
    w_gate = jnp.concatenate([w_ig, zw, w_fg, zw], axis=1)
    b_gate = jnp.concatenate([b_ig, zb, b_fg, zb]).reshape(1, 2 * LANES)
    blocks = lambda a: a.reshape(MLSTM_INNER // MLSTM_QKV_BLOCK, MLSTM_QKV_BLOCK, 2 * LANES)
    fold = lambda bd, a: jnp.einsum("gij,gjn->gin", bd, blocks(a), precision=lax.Precision.HIGHEST)
    gate_q, gate_k, gate_v = (w_gate[n * MLSTM_INNER:(n + 1) * MLSTM_INNER] for n in range(3))
    w_gate_xc = (fold(w_q, gate_q) + fold(w_k, gate_k)).reshape(MLSTM_INNER, 2 * LANES)
    w_gate_x = jnp.dot(w_in[:, :MLSTM_INNER], fold(w_v, gate_v).reshape(MLSTM_INNER, 2 * LANES),
                       precision=lax.Precision.HIGHEST)
    return dict(w_in=jnp.concatenate([w_in, w_gate_x], axis=1).astype(BF16), conv_w=conv_w,
                conv_b=conv_b.reshape(1, -1), bd_qk=bd_qk.astype(BF16),
                bd_v=_block_diag_tiles(w_v).astype(BF16), w_gate=w_gate_xc.astype(BF16), b_gate=b_gate)


def kernel(x, positions, router_w, router_b, mla_w_in, mla_g_q, mla_w_uq, mla_g_kv, mla_w_ukv, mla_w_o, mlstm_w_in, mlstm_conv_w, mlstm_conv_b, mlstm_w_q, mlstm_w_k, mlstm_w_v, mlstm_w_ig, mlstm_b_ig, mlstm_w_fg, mlstm_b_fg, mlstm_g_out, mlstm_skip, mlstm_w_out, moe_w_gate, moe_w_up, moe_w_down, ln_g, ln_b):
    batch, seq, _ = x.shape
    tiles = _tiles(seq)
    rwt = router_w.T.astype(BF16)
    rb = router_b.reshape(N_EXPERTS, 1)
    rw_pad = jnp.concatenate([router_w, jnp.zeros((D_MODEL, LANES - N_EXPERTS), F32)], axis=1).astype(BF16)
    t = batch * seq
    cos, sin = _rope_tables(positions, tiles["rope"])
    xt = x.reshape(t, D_MODEL)
    for i in range(DEPTH):
        j = i // N_MIXERS
        g0, b0 = ln_g[i, 0].reshape(1, -1), ln_b[i, 0].reshape(1, -1)
        g1, b1 = ln_g[i, 1].reshape(1, -1), ln_b[i, 1].reshape(1, -1)
        if i % N_MIXERS == 0:
            w = _prep_mla(mla_w_in[j], mla_g_q[j], mla_w_uq[j], mla_g_kv[j], mla_w_ukv[j])
            w_out = mla_w_o[j].astype(BF16)
        else:
            w = _prep_mlstm(mlstm_w_in[j], mlstm_conv_w[j], mlstm_conv_b[j], mlstm_w_q[j], mlstm_w_k[j],
                            mlstm_w_v[j], mlstm_w_ig[j], mlstm_b_ig[j], mlstm_w_fg[j], mlstm_b_fg[j])
            w_out = mlstm_w_out[j].astype(BF16)
        if i % N_MIXERS == 0:
            q, k, vt = _mla_proj(xt, cos, sin, w, tiles["attn"])
            a = _attention(q, k, vt, batch, seq, tiles["attn"])
        else:
            q, k, v, xc, sz, gates = _mlstm_pre(xt, w, batch, seq, tiles["pre"])
            a = _mlstm_chunks(q, k, v, gates, xc, sz, mlstm_g_out[j].reshape(1, -1),
                              mlstm_skip[j].reshape(1, -1), batch, seq, tiles["chunk"])
        x1, cls = _post(a, w_out, xt, g0, b0, rwt, rb, tiles["post"])
        xt = _moe(x1, cls.reshape(t), rw_pad, i, moe_w_gate, moe_w_up, moe_w_down, g1, b1, tiles["moe"])
    return xt.reshape(batch, seq, D_MODEL)
```

```python
import functools

import jax
import jax.numpy as jnp
from jax import lax
from jax.experimental import pallas as pl
from jax.experimental.pallas import tpu as pltpu
from jax.experimental.pallas import tpu_sc as plsc

D_MODEL = 1024
DEPTH = 4
N_MIXERS = 2
MLA_HEADS = 8
MLA_Q_RANK = 256
MLA_KV_RANK = 256
MLA_NOPE = 128
MLA_ROPE = 64
MLA_V = 128
ROPE_THETA = 10000.0
MLSTM_INNER = 2 * D_MODEL
MLSTM_HEADS = 4
MLSTM_HD = MLSTM_INNER // MLSTM_HEADS
MLSTM_CONV = 4
MLSTM_QKV_BLOCK = 4
N_EXPERTS = 16
N_GROUPS = 4
EXPERTS_PER_GROUP = N_EXPERTS // N_GROUPS
EXPERT_FF = 512
DN_ALPHA = (2.0 * DEPTH) ** 0.25
LN_EPS = 1e-5
LOG2_E = 1.4426950408889634

LANES = 128
MXU_DIM = 256
HEAD_SLOT = 2 * LANES
ATTN_HEADS_PER_STEP = 4
N_PAIRS = EXPERTS_PER_GROUP * (EXPERTS_PER_GROUP - 1) // 2
N_CLASSES = N_GROUPS * N_PAIRS
SC_WORKERS = 32
SC_MAX_ROWS = 48
MOE_SPLIT = 4
POST_SUB = 512
VMEM_LIMIT = 56 * 1024 * 1024

F32 = jnp.float32
BF16 = jnp.bfloat16


def _tiles(seq):
    return dict(
        rope=min(seq, 2048),
        attn=min(seq, 512),
        post=min(seq, 1024),
        pre=min(seq, 256),
        chunk=min(seq, 256),
        moe=256,
    )


def _params(*sem):
    return pltpu.CompilerParams(dimension_semantics=sem, vmem_limit_bytes=VMEM_LIMIT)


def _dot(a, b):
    return jnp.dot(a, b, preferred_element_type=F32)


def _dot_nt(a, b):
    return lax.dot_general(a, b, (((1,), (1,)), ((), ())), preferred_element_type=F32)


def _dot_tn(a, b):
    return lax.dot_general(a, b, (((0,), (0,)), ((), ())), preferred_element_type=F32)


def _layer_norm(v, g, b):
    mu = jnp.mean(v, axis=-1, keepdims=True)
    c = v - mu
    var = jnp.mean(c * c, axis=-1, keepdims=True)
    return c * lax.rsqrt(var + LN_EPS) * g + b


def _rms_norm(v, g):
    return v * lax.rsqrt(jnp.mean(v * v, axis=-1, keepdims=True) + LN_EPS) * g


def _sigmoid(v):
    return 1.0 / (1.0 + jnp.exp(-v))


def _silu(v):
    return v * _sigmoid(v)


def _rope_kernel(pos_ref, invf_ref, cos_ref, sin_ref):
    ang = pos_ref[...].astype(F32) * invf_ref[...]
    cos_ref[...] = jnp.cos(ang)
    sin_ref[...] = jnp.sin(ang)


def _rope_tables(positions, tm):
    t = positions.size
    inv_freq = 1.0 / (ROPE_THETA ** (jnp.arange(0, MLA_ROPE, 2, dtype=F32) / MLA_ROPE))
    invf = jnp.tile(inv_freq, LANES // (MLA_ROPE // 2)).reshape(1, LANES)
    out = jax.ShapeDtypeStruct((t, LANES), F32)
    return pl.pallas_call(
        _rope_kernel,
        grid=(t // tm,),
        in_specs=[pl.BlockSpec((tm, 1), lambda i: (i, 0)),
                  pl.BlockSpec((1, LANES), lambda i: (0, 0))],
        out_specs=[pl.BlockSpec((tm, LANES), lambda i: (i, 0))] * 2,
        out_shape=[out, out],
        compiler_params=_params("parallel"),
        name="rope_tables",
    )(positions.reshape(t, 1), invf)


def _mla_proj_kernel(x_ref, cos_ref, sin_ref, w_in_ref, gq_ref, gkv_ref, wq1_ref, wq2_ref,
                     wk_ref, wvt_ref, q_ref, k_ref, vt_ref):
    xb = x_ref[...].astype(BF16)
    c = _dot(xb, w_in_ref[...])
    cq = _rms_norm(c[:, :MLA_Q_RANK], gq_ref[...]).astype(BF16)
    ckv = _rms_norm(c[:, MLA_Q_RANK:MLA_Q_RANK + MLA_KV_RANK], gkv_ref[...]).astype(BF16)
    cos = cos_ref[...]
    sin = sin_ref[...]
    base = MLA_Q_RANK + MLA_KV_RANK
    k_pe = (c[:, base:base + LANES] * cos + c[:, base + LANES:base + 2 * LANES] * sin).astype(BF16)
    scale = (MLA_NOPE + MLA_ROPE) ** -0.5 * LOG2_E
    cos_s = cos * scale
    sin_s = sin * scale
    q1 = _dot(cq, wq1_ref[...])
    q2 = _dot(cq, wq2_ref[...])
    kn = _dot(ckv, wk_ref[...])
    for h in range(MLA_HEADS):
        s0 = h * HEAD_SLOT
        q_ref[:, s0:s0 + LANES] = (q1[:, s0:s0 + LANES] * scale).astype(BF16)
        q_ref[:, s0 + LANES:s0 + HEAD_SLOT] = (
            q1[:, s0 + LANES:s0 + HEAD_SLOT] * cos_s
            + q2[:, h * LANES:(h + 1) * LANES] * sin_s).astype(BF16)
        k_ref[:, s0:s0 + LANES] = kn[:, h * LANES:(h + 1) * LANES].astype(BF16)
        k_ref[:, s0 + LANES:s0 + HEAD_SLOT] = k_pe
    vt_ref[...] = _dot_nt(wvt_ref[...], ckv).astype(BF16)


def _mla_proj(x, cos, sin, w, tm):
    t = x.shape[0]
    row = lambda n: pl.BlockSpec((tm, n), lambda i: (i, 0))
    full = lambda a: pl.BlockSpec(a.shape, lambda i: (0,) * a.ndim)
    consts = (w["w_in"], w["g_q"], w["g_kv"], w["w_q1"], w["w_q2"], w["w_k"], w["w_vt"])
    return pl.pallas_call(
        _mla_proj_kernel,
        grid=(t // tm,),
        in_specs=[row(D_MODEL), row(LANES), row(LANES)] + [full(a) for a in consts],
        out_specs=[row(MLA_HEADS * HEAD_SLOT), row(MLA_HEADS * HEAD_SLOT),
                   pl.BlockSpec((None, MLA_HEADS * MLA_V, tm), lambda i: (i, 0, 0))],
        out_shape=[jax.ShapeDtypeStruct((t, MLA_HEADS * HEAD_SLOT), BF16),
                   jax.ShapeDtypeStruct((t, MLA_HEADS * HEAD_SLOT), BF16),
                   jax.ShapeDtypeStruct((t // tm, MLA_HEADS * MLA_V, tm), BF16)],
        compiler_params=_params("parallel"),
        name="mla_proj",
    )(x, cos, sin, *consts)


def _attn_kernel(q_ref, k_ref, vt_ref, o_ref, s_ref, smax_ref, m_ref, l_ref, acc_ref, *, tile):
    qi = pl.program_id(2)
    m_ref[...] = jnp.full(m_ref.shape, -jnp.inf, F32)
    l_ref[...] = jnp.zeros(l_ref.shape, F32)
    acc_ref[...] = jnp.zeros(acc_ref.shape, F32)
    heads = range(ATTN_HEADS_PER_STEP)

    def scores(ki, buf):
        start = pl.multiple_of(ki * tile, tile)
        for g in heads:
            slot = slice(g * HEAD_SLOT, (g + 1) * HEAD_SLOT)
            s = _dot_nt(k_ref[pl.ds(start, tile), slot], q_ref[:, slot])
            s_ref[buf, g] = s
            smax_ref[buf, g] = jnp.max(s, axis=0, keepdims=True)

    def absorb(ki, buf, masked):
        for g in heads:
            s = s_ref[buf, g]
            if masked:
                keys = lax.broadcasted_iota(jnp.int32, s.shape, 0)
                queries = lax.broadcasted_iota(jnp.int32, s.shape, 1)
                s = jnp.where(keys <= queries, s, -jnp.inf)
                s_max = jnp.max(s, axis=0, keepdims=True)
            else:
                s_max = smax_ref[buf, g]
            m_old = m_ref[g]
            m_new = jnp.maximum(m_old, s_max)
            p = jnp.exp2(s - m_new)
            alpha = jnp.exp2(m_old - m_new)
            l_ref[g] = alpha * l_ref[g] + jnp.sum(p, axis=0, keepdims=True)
            acc_ref[g] = alpha * acc_ref[g] + _dot(vt_ref[ki, g * MLA_V:(g + 1) * MLA_V, :], p.astype(BF16))
            m_ref[g] = m_new

    scores(0, 0)

    def body(j, carry):
        scores(2 * j + 1, 1)
        absorb(2 * j, 0, False)
        scores(2 * j + 2, 0)
        absorb(2 * j + 1, 1, False)
        return carry

    lax.fori_loop(0, qi // 2, body, 0)

    @pl.when(qi % 2 == 1)
    def _():
        scores(qi, 1)
        absorb(qi - 1, 0, False)
        absorb(qi, 1, True)

    @pl.when(qi % 2 == 0)
    def _():
        absorb(qi, 0, True)

    for g in heads:
        o_ref[:, g * MLA_V:(g + 1) * MLA_V] = jnp.transpose(acc_ref[g] / l_ref[g]).astype(o_ref.dtype)


def _attention(q, k, vt, batch, seq, tile):
    t = q.shape[0]
    nq = seq // tile
    hps = ATTN_HEADS_PER_STEP
    return pl.pallas_call(
        functools.partial(_attn_kernel, tile=tile),
        grid=(batch, MLA_HEADS // hps, nq),
        in_specs=[pl.BlockSpec((tile, hps * HEAD_SLOT), lambda b, h, i: (b * nq + i, h)),
                  pl.BlockSpec((seq, hps * HEAD_SLOT), lambda b, h, i: (b, h)),
                  pl.BlockSpec((nq, hps * MLA_V, tile), lambda b, h, i: (b, h, 0))],
        out_specs=pl.BlockSpec((tile, hps * MLA_V), lambda b, h, i: (b * nq + i, h)),
        out_shape=jax.ShapeDtypeStruct((t, MLA_HEADS * MLA_V), BF16),
        scratch_shapes=[pltpu.VMEM((2, hps, tile, tile), F32), pltpu.VMEM((2, hps, 1, tile), F32),
                        pltpu.VMEM((hps, 1, tile), F32), pltpu.VMEM((hps, 1, tile), F32),
                        pltpu.VMEM((hps, MLA_V, tile), F32)],
        compiler_params=_params("parallel", "parallel", "arbitrary"),
        name="mla_attention",
    )(q, k, vt)


def _route_rows(sel):
    best = None
    for g in range(N_GROUPS):
        a = [sel[g * EXPERTS_PER_GROUP + j:g * EXPERTS_PER_GROUP + j + 1, :]
             for j in range(EXPERTS_PER_GROUP)]
        v1 = functools.reduce(jnp.maximum, a)
        i1 = jnp.full(v1.shape, EXPERTS_PER_GROUP - 1, jnp.int32)
        for j in reversed(range(EXPERTS_PER_GROUP - 1)):
            i1 = jnp.where(a[j] == v1, j, i1)
        r = [jnp.where(i1 == j, -jnp.inf, a[j]) for j in range(EXPERTS_PER_GROUP)]
        v2 = functools.reduce(jnp.maximum, r)
        i2 = jnp.full(v1.shape, EXPERTS_PER_GROUP - 1, jnp.int32)
        for j in reversed(range(EXPERTS_PER_GROUP - 1)):
            i2 = jnp.where(r[j] == v2, j, i2)
        total = v1 + v2
        if best is None:
            best = (total, jnp.zeros_like(i1), i1, i2)
        else:
            better = total > best[0]
            best = (jnp.where(better, total, best[0]), jnp.where(better, g, best[1]),
                    jnp.where(better, i1, best[2]), jnp.where(better, i2, best[3]))
    _, grp, i1, i2 = best
    lo = jnp.minimum(i1, i2)
    hi = jnp.maximum(i1, i2)
    pair = jnp.where(lo == 0, hi - 1, jnp.where(lo == 1, hi + 1, N_PAIRS - 1))
    return grp * N_PAIRS + pair


def _post_kernel(a_ref, w_ref, x_ref, g_ref, b_ref, rwt_ref, rb_ref, y_ref, cls_ref):
    subs = [slice(r, r + POST_SUB) for r in range(0, a_ref.shape[0], POST_SUB)]
    y = _dot(a_ref[subs[0], :], w_ref[...])
    for n, rows in enumerate(subs):
        y_next = _dot(a_ref[subs[n + 1], :], w_ref[...]) if n + 1 < len(subs) else None
        x1 = _layer_norm(DN_ALPHA * x_ref[rows, :] + y, g_ref[...], b_ref[...])
        y_ref[rows, :] = x1
        logits = _dot_nt(rwt_ref[...], x1.astype(BF16))
        cls_ref[:, rows] = _route_rows(_sigmoid(logits) + rb_ref[...])
        y = y_next


def _post(a, w, x, g, b, rwt, rb, tm):
    t, k = a.shape
    row = lambda n: pl.BlockSpec((tm, n), lambda i: (i, 0))
    full = lambda arr: pl.BlockSpec(arr.shape, lambda i: (0,) * arr.ndim)
    return pl.pallas_call(
        _post_kernel,
        grid=(t // tm,),
        in_specs=[row(k), full(w), row(D_MODEL), full(g), full(b), full(rwt), full(rb)],
        out_specs=[row(D_MODEL), pl.BlockSpec((1, tm), lambda i: (0, i))],
        out_shape=[jax.ShapeDtypeStruct((t, D_MODEL), F32),
                   jax.ShapeDtypeStruct((1, t), jnp.int32)],
        compiler_params=_params("parallel"),
        name="mixer_out_norm_route",
    )(a, w, x, g, b, rwt, rb)


def _expert_kernel(ea_ref, eb_ref, fresh_ref, nblk_ref, x_ref, rw_ref, wga_ref, wua_ref, wda_ref,
                   wgb_ref, wub_ref, wdb_ref, g_ref, b_ref, prev_ref, o_ref, wga, wua, wda, wgb, wub, wdb):
    del prev_ref
    i = pl.program_id(0)

    @pl.when(fresh_ref[i] == 1)
    def _():
        for src, dst in ((wga_ref, wga), (wua_ref, wua), (wda_ref, wda),
                         (wgb_ref, wgb), (wub_ref, wub), (wdb_ref, wdb)):
            dst[...] = src[...].astype(BF16)

    @pl.when(i < nblk_ref[0])
    def _():
        x = x_ref[...]
        xb = x.astype(BF16)
        sig = _sigmoid(_dot(xb, rw_ref[...]))
        lane = lax.broadcasted_iota(jnp.int32, sig.shape, 1)
        sa = jnp.sum(jnp.where(lane == ea_ref[i], sig, 0.0), axis=-1, keepdims=True)
        sb = jnp.sum(jnp.where(lane == eb_ref[i], sig, 0.0), axis=-1, keepdims=True)
        tot = sa + sb
        ha = _silu(_dot(xb, wga[...])) * _dot(xb, wua[...])
        hb = _silu(_dot(xb, wgb[...])) * _dot(xb, wub[...])
        y = (sa / tot) * _dot(ha.astype(BF16), wda[...]) + (sb / tot) * _dot(hb.astype(BF16), wdb[...])
        o_ref[...] = _layer_norm(DN_ALPHA * x + y, g_ref[...], b_ref[...])

    @pl.when(i >= nblk_ref[0])
    def _():
        o_ref[...] = jnp.zeros(o_ref.shape, o_ref.dtype)


def _experts(xs, ea, eb, fresh, nblk, rw, layer, wg, wu, wd, g, b, bm, blk0, total_rows, prev):
    p = xs.shape[0]
    sel = lambda which: (lambda i, ea, eb, fr, nb: (layer, (ea if which == 0 else eb)[i], 0, 0))
    up = lambda which: pl.BlockSpec((None, None, D_MODEL, EXPERT_FF), sel(which))
    down = lambda which: pl.BlockSpec((None, None, EXPERT_FF, D_MODEL), sel(which))
    full = lambda arr: pl.BlockSpec(arr.shape, lambda i, ea, eb, fr, nb: (0,) * arr.ndim)
    row = pl.BlockSpec((bm, D_MODEL), lambda i, ea, eb, fr, nb: (i, 0))
    out_row = pl.BlockSpec((bm, D_MODEL), lambda i, ea, eb, fr, nb: (i + blk0, 0))
    up_s = pltpu.VMEM((D_MODEL, EXPERT_FF), BF16)
    down_s = pltpu.VMEM((EXPERT_FF, D_MODEL), BF16)
    n_prefetch = 4
    if prev is None:
        prev = jnp.zeros((8, LANES), F32)
        aliases = {}
    else:
        aliases = {n_prefetch + 10: 0}
    return pl.pallas_call(
        _expert_kernel,
        grid_spec=pltpu.PrefetchScalarGridSpec(
            num_scalar_prefetch=n_prefetch,
            grid=(p // bm,),
            in_specs=[row, full(rw), up(0), up(0), down(0), up(1), up(1), down(1), full(g), full(b),
                      pl.BlockSpec(memory_space=pl.ANY)],
            out_specs=out_row,
            scratch_shapes=[up_s, up_s, down_s, up_s, up_s, down_s]),
        out_shape=jax.ShapeDtypeStruct((total_rows, D_MODEL), F32),
        input_output_aliases=aliases,
        compiler_params=_params("arbitrary"),
        name="moe_experts",
    )(ea, eb, fresh, nblk, xs, rw, wg, wu, wd, wg, wu, wd, g, b, prev)


def _row_gather(src, idx):
    m = idx.shape[0]
    d = src.shape[1]
    per = m // SC_WORKERS
    assert per * SC_WORKERS == m
    sc_rows = next(r for r in range(SC_MAX_ROWS, 0, -8) if per % r == 0 and (per // r) % 2 == 0)
    n = per // sc_rows
    mesh = plsc.VectorSubcoreMesh(core_axis_name="core", subcore_axis_name="subcore")

    def body(src_hbm, idx_hbm, out_hbm, idx_v, buf, gsem, psem):
        worker = lax.axis_index("core") * (SC_WORKERS // 2) + lax.axis_index("subcore")
        base = worker * per
        pltpu.sync_copy(idx_hbm.at[pl.ds(base, per)], idx_v)

        def gather(c, slot):
            return pltpu.make_async_copy(src_hbm.at[idx_v.at[pl.ds(c * sc_rows, sc_rows)]],
                                         buf.at[slot], gsem.at[slot])

        def put(c, slot):
            return pltpu.make_async_copy(buf.at[slot], out_hbm.at[pl.ds(base + c * sc_rows, sc_rows)],
                                         psem.at[slot])

        gather(0, 0).start()

        @pl.loop(0, n // 2)
        def _(j):
            for slot in range(2):
                c = 2 * j + slot
                gather(c, slot).wait()
                put(c, slot).start()

                @pl.when(c >= 1)
                def _():
                    put(c - 1, 1 - slot).wait()

                @pl.when(c + 1 < n)
                def _():
                    gather(c + 1, 1 - slot).start()

        put(n - 1, 1).wait()

    return pl.kernel(
        body,
        out_type=jax.ShapeDtypeStruct((m, d), src.dtype),
        mesh=mesh,
        scratch_types=[pltpu.VMEM((per,), jnp.int32), pltpu.VMEM((2, sc_rows, d), src.dtype),
                       pltpu.SemaphoreType.DMA((2,)), pltpu.SemaphoreType.DMA((2,))],
    )(src, idx)


def _moe(x1, cls, rw, layer, wg, wu, wd, g, b, bm):
    t = x1.shape[0]
    p_rows = t + N_CLASSES * bm
    nblk = p_rows // bm
    classes = jnp.arange(N_CLASSES, dtype=jnp.int32)
    counts = jnp.sum((cls[:, None] == classes[None, :]).astype(jnp.int32), axis=0)
    padded = ((counts + bm - 1) // bm) * bm
    off = jnp.cumsum(counts) - counts
    pend = jnp.cumsum(padded)
    poff = pend - padded
    tok = jnp.arange(t, dtype=jnp.int32)
    cls_sorted, order = lax.sort((cls, tok), num_keys=1)
    shift = jnp.sum(jnp.where(cls_sorted[:, None] == classes[None, :], (poff - off)[None, :], 0), axis=1)
    _, dest = lax.sort((order, shift + tok), num_keys=1)
    blk_start = jnp.arange(nblk, dtype=jnp.int32) * bm
    blk_cls = jnp.minimum(jnp.sum((blk_start[:, None] >= pend[None, :]).astype(jnp.int32), axis=1),
                          N_CLASSES - 1)
    n_used = (pend[-1] // bm).astype(jnp.int32).reshape(1)
    j = jnp.arange(bm, dtype=jnp.int32)[None, :] + (blk_start - poff[blk_cls])[:, None]
    rank = off[blk_cls][:, None] + jnp.minimum(j, counts[blk_cls][:, None] - 1)
    src = order[jnp.clip(rank, 0, t - 1).reshape(p_rows)]
    grp = blk_cls // N_PAIRS
    pair = blk_cls % N_PAIRS
    lo = jnp.where(pair < 3, 0, jnp.where(pair < 5, 1, 2))
    hi = jnp.where(pair < 3, pair + 1, jnp.where(pair < 5, pair - 1, 3))
    ea = (grp * EXPERTS_PER_GROUP + lo).astype(jnp.int32)
    eb = (grp * EXPERTS_PER_GROUP + hi).astype(jnp.int32)
    fresh = jnp.concatenate([jnp.ones((1,), jnp.int32), (blk_cls[1:] != blk_cls[:-1]).astype(jnp.int32)])
    nb = nblk // MOE_SPLIT
    xs = [_row_gather(x1, src[k * nb * bm:(k + 1) * nb * bm]) for k in range(MOE_SPLIT)]
    ys = None
    for k in range(MOE_SPLIT):
        part = slice(k * nb, (k + 1) * nb)
        fresh_k = jnp.concatenate([jnp.ones((1,), jnp.int32), fresh[part][1:]])
        ys = _experts(xs[k], ea[part], eb[part], fresh_k, n_used - k * nb, rw, layer, wg, wu, wd, g, b, bm,
                      k * nb, p_rows, ys)
    return _row_gather(ys, dest)


def _mlstm_pre_kernel(x_ref, w_in_ref, cw_ref, cb_ref, bdqk_ref, bdv_ref, wgate_ref, bgate_ref,
                      q_ref, k_ref, v_ref, xc_ref, sz_ref, gate_ref, ext_ref, *, tm):
    halo = 8
    xb = x_ref[...].astype(BF16)

    @pl.when(pl.program_id(1) == 0)
    def _():
        ext_ref[0:halo, :] = jnp.zeros((halo, MLSTM_INNER), F32)

    @pl.when(pl.program_id(1) != 0)
    def _():
        ext_ref[0:halo, :] = ext_ref[tm:tm + halo, :]

    xm = _dot(xb, w_in_ref[:, :MLSTM_INNER])
    ext_ref[halo:halo + tm, :] = xm
    conv = cb_ref[...] + cw_ref[MLSTM_CONV - 1:MLSTM_CONV, :] * xm
    for kk in range(MLSTM_CONV - 1):
        lag = MLSTM_CONV - 1 - kk
        conv = conv + cw_ref[kk:kk + 1, :] * ext_ref[halo - lag:halo - lag + tm, :]
    xc = _silu(conv)
    xc_b = xc.astype(BF16)
    xm_b = xm.astype(BF16)
    xc_ref[...] = xc_b
    sz_ref[...] = _silu(_dot(xb, w_in_ref[:, MLSTM_INNER:2 * MLSTM_INNER])).astype(BF16)

    gates = bgate_ref[...] + _dot(xb, w_in_ref[:, 2 * MLSTM_INNER:]) + _dot(xc_b, wgate_ref[...])
    k_scale = MLSTM_HD ** -0.5
    for c in range(MLSTM_INNER // MXU_DIM):
        lo, hi = c * MXU_DIM, (c + 1) * MXU_DIM
        qk = _dot(xc_b[:, lo:hi], bdqk_ref[c])
        q_ref[:, lo:hi] = qk[:, :MXU_DIM].astype(BF16)
        k_ref[:, lo:hi] = (qk[:, MXU_DIM:] * k_scale).astype(BF16)
        v_ref[:, lo:hi] = _dot(xm_b[:, lo:hi], bdv_ref[c]).astype(BF16)
    fg = gates[:, LANES:]
    gate_ref[:, :LANES] = gates[:, :LANES]
    gate_ref[:, LANES:] = jnp.minimum(fg, 0.0) - jnp.log(1.0 + jnp.exp(-jnp.abs(fg)))


def _mlstm_pre(x, w, batch, seq, tm):
    t = x.shape[0]
    ns = seq // tm
    row = lambda n: pl.BlockSpec((tm, n), lambda b, s: (b * ns + s, 0))
    full = lambda a: pl.BlockSpec(a.shape, lambda b, s: (0,) * a.ndim)
    consts = (w["w_in"], w["conv_w"], w["conv_b"], w["bd_qk"], w["bd_v"], w["w_gate"], w["b_gate"])
    act = jax.ShapeDtypeStruct((t, MLSTM_INNER), BF16)
    return pl.pallas_call(
        functools.partial(_mlstm_pre_kernel, tm=tm),
        grid=(batch, ns),
        in_specs=[row(D_MODEL)] + [full(a) for a in consts],
        out_specs=[row(MLSTM_INNER)] * 5 + [row(2 * LANES)],
        out_shape=[act] * 5 + [jax.ShapeDtypeStruct((t, 2 * LANES), F32)],
        scratch_shapes=[pltpu.VMEM((tm + 8, MLSTM_INNER), F32)],
        compiler_params=_params("parallel", "arbitrary"),
        name="mlstm_pre",
    )(x, *consts)


def _split3(v):
    a = v.astype(BF16)
    r = v - a.astype(F32)
    b = r.astype(BF16)
    c = (r - b.astype(F32)).astype(BF16)
    return a, b, c


def _mlstm_chunk_kernel(q_ref, k_ref, v_ref, gate_ref, xc_ref, sz_ref, gout_ref, skip_ref, o_ref,
                        c_ref, n_ref, m_ref, *, chunk):
    @pl.when(pl.program_id(1) == 0)
    def _():
        c_ref[...] = jnp.zeros(c_ref.shape, F32)
        n_ref[...] = jnp.zeros(n_ref.shape, F32)
        m_ref[...] = jnp.zeros(m_ref.shape, F32)

    ig_all = gate_ref[:, :LANES]
    lf_all = gate_ref[:, LANES:]
    rows = lax.broadcasted_iota(jnp.int32, (chunk, chunk), 0)
    cols = lax.broadcasted_iota(jnp.int32, (chunk, chunk), 1)
    causal = cols <= rows
    tril = jnp.where(causal, 1.0, 0.0).astype(BF16)
    b_all = functools.reduce(lambda u, w: u + w, [_dot(tril, piece) for piece in _split3(lf_all)])
    row_all = jnp.transpose(ig_all - b_all)

    for h in range(MLSTM_HEADS):
        lo, hi = h * MLSTM_HD, (h + 1) * MLSTM_HD
        q = q_ref[:, lo:hi]
        k = k_ref[:, lo:hi]
        v = v_ref[:, lo:hi]
        b = b_all[:, h:h + 1]
        ig = ig_all[:, h:h + 1]
        m_prev = m_ref[h]
        d = jnp.where(causal, b + row_all[h:h + 1, :], -jnp.inf)
        inter = b + m_prev
        m_i = jnp.maximum(inter, jnp.max(d, axis=-1, keepdims=True))
        w_intra = jnp.exp(d - m_i)
        w_inter = jnp.exp(inter - m_i)
        s = _dot_nt(q, k) * w_intra
        c_old = c_ref[h]
        n_old = n_ref[h]
        num = _dot(s.astype(BF16), v) + w_inter * _dot(q, c_old.astype(BF16))
        qn = jnp.sum(q.astype(F32) * n_old, axis=-1, keepdims=True)
        den = jnp.sum(s, axis=-1, keepdims=True) + w_inter * qn
        hh = num / jnp.maximum(jnp.abs(den), jnp.exp(-m_i))

        b_last = b[chunk - 1:chunk, :]
        gg = b_last - b + ig
        m_new = jnp.maximum(b_last + m_prev, jnp.max(gg, axis=0, keepdims=True))
        kw = k.astype(F32) * jnp.exp(gg - m_new)
        decay = jnp.exp(b_last + m_prev - m_new)
        c_ref[h] = decay * c_old + _dot_tn(kw.astype(BF16), v)
        n_ref[h] = decay * n_old + jnp.sum(kw, axis=0, keepdims=True)
        m_ref[h] = m_new

        mu = jnp.mean(hh, axis=-1, keepdims=True)
        cen = hh - mu
        var = jnp.mean(cen * cen, axis=-1, keepdims=True)
        hn = cen * lax.rsqrt(var + LN_EPS) * gout_ref[:, lo:hi]
        o_ref[:, lo:hi] = ((hn + skip_ref[:, lo:hi] * xc_ref[:, lo:hi].astype(F32))
                           * sz_ref[:, lo:hi].astype(F32)).astype(o_ref.dtype)


def _mlstm_chunks(q, k, v, gates, xc, sz, g_out, skip, batch, seq, chunk):
    t = q.shape[0]
    nc = seq // chunk
    row = lambda n: pl.BlockSpec((chunk, n), lambda b, c: (b * nc + c, 0))
    full = lambda a: pl.BlockSpec(a.shape, lambda b, c: (0,) * a.ndim)
    return pl.pallas_call(
        functools.partial(_mlstm_chunk_kernel, chunk=chunk),
        grid=(batch, nc),
        in_specs=[row(MLSTM_INNER)] * 3 + [row(2 * LANES)] + [row(MLSTM_INNER)] * 2
                 + [full(g_out), full(skip)],
        out_specs=row(MLSTM_INNER),
        out_shape=jax.ShapeDtypeStruct((t, MLSTM_INNER), BF16),
        scratch_shapes=[pltpu.VMEM((MLSTM_HEADS, MLSTM_HD, MLSTM_HD), F32),
                        pltpu.VMEM((MLSTM_HEADS, 1, MLSTM_HD), F32),
                        pltpu.VMEM((MLSTM_HEADS, 1, 1), F32)],
        compiler_params=_params("parallel", "arbitrary"),
        name="mlstm_chunks",
    )(q, k, v, gates, xc, sz, g_out, skip)


def _rot_half(w):
    half = MLA_ROPE // 2
    return jnp.concatenate([-w[..., half:], w[..., :half]], axis=-1)


def _prep_mla(w_in, g_q, w_uq, g_kv, w_ukv):
    pad = jnp.zeros((D_MODEL, LANES - MLA_ROPE), F32)
    w_pe = w_in[:, MLA_Q_RANK + MLA_KV_RANK:]
    w_in_p = jnp.concatenate([w_in[:, :MLA_Q_RANK + MLA_KV_RANK], w_pe, pad, _rot_half(w_pe), pad], axis=1)
    uq = w_uq.reshape(MLA_Q_RANK, MLA_HEADS, MLA_NOPE + MLA_ROPE)
    zq = jnp.zeros((MLA_Q_RANK, MLA_HEADS, LANES - MLA_ROPE), F32)
    w_q1 = jnp.concatenate([uq, zq], axis=-1).reshape(MLA_Q_RANK, MLA_HEADS * HEAD_SLOT)
    w_q2 = jnp.concatenate([_rot_half(uq[..., MLA_NOPE:]), zq], axis=-1).reshape(MLA_Q_RANK, MLA_HEADS * LANES)
    ukv = w_ukv.reshape(MLA_KV_RANK, MLA_HEADS, MLA_NOPE + MLA_V)
    w_k = ukv[..., :MLA_NOPE].reshape(MLA_KV_RANK, -1)
    w_vt = ukv[..., MLA_NOPE:].reshape(MLA_KV_RANK, -1).T
    return dict(w_in=w_in_p.astype(BF16), g_q=g_q.reshape(1, -1), g_kv=g_kv.reshape(1, -1),
                w_q1=w_q1.astype(BF16), w_q2=w_q2.astype(BF16), w_k=w_k.astype(BF16),
                w_vt=w_vt.astype(BF16))


def _block_diag_tiles(w):
    per = MXU_DIM // MLSTM_QKV_BLOCK
    w = w.reshape(MLSTM_INNER // MXU_DIM, per, MLSTM_QKV_BLOCK, MLSTM_QKV_BLOCK)
    eye = jnp.eye(per, dtype=w.dtype)
    return jnp.einsum("cgij,gh->cgihj", w, eye).reshape(MLSTM_INNER // MXU_DIM, MXU_DIM, MXU_DIM)


def _prep_mlstm(w_in, conv_w, conv_b, w_q, w_k, w_v, w_ig, b_ig, w_fg, b_fg):
    bd_qk = jnp.concatenate([_block_diag_tiles(w_q), _block_diag_tiles(w_k)], axis=-1)
    zw = jnp.zeros((3 * MLSTM_INNER, LANES - MLSTM_HEADS), F32)
    zb = jnp.zeros((LANES - MLSTM_HEADS,), F32)
    w_gate = jnp.concatenate([w_ig, zw, w_fg, zw], axis=1)
    b_gate = jnp.concatenate([b_ig, zb, b_fg, zb]).reshape(1, 2 * LANES)
    blocks = lambda a: a.reshape(MLSTM_INNER // MLSTM_QKV_BLOCK, MLSTM_QKV_BLOCK, 2 * LANES)
    fold = lambda bd, a: jnp.einsum("gij,gjn->gin", bd, blocks(a), precision=lax.Precision.HIGHEST)
    gate_q, gate_k, gate_v = (w_gate[n * MLSTM_INNER:(n + 1) * MLSTM_INNER] for n in range(3))
    w_gate_xc = (fold(w_q, gate_q) + fold(w_k, gate_k)).reshape(MLSTM_INNER, 2 * LANES)
    w_gate_x = jnp.dot(w_in[:, :MLSTM_INNER], fold(w_v, gate_v).reshape(MLSTM_INNER, 2 * LANES),
                       precision=lax.Precision.HIGHEST)
    return dict(w_in=jnp.concatenate([w_in, w_gate_x], axis=1).astype(BF16), conv_w=conv_w,
                conv_b=conv_b.reshape(1, -1), bd_qk=bd_qk.astype(BF16),
                bd_v=_block_diag_tiles(w_v).astype(BF16), w_gate=w_gate_xc.astype(BF16), b_gate=b_gate)


def kernel(x, positions, router_w, router_b, mla_w_in, mla_g_q, mla_w_uq, mla_g_kv, mla_w_ukv, mla_w_o, mlstm_w_in, mlstm_conv_w, mlstm_conv_b, mlstm_w_q, mlstm_w_k, mlstm_w_v, mlstm_w_ig, mlstm_b_ig, mlstm_w_fg, mlstm_b_fg, mlstm_g_out, mlstm_skip, mlstm_w_out, moe_w_gate, moe_w_up, moe_w_down, ln_g, ln_b):
    batch, seq, _ = x.shape
    tiles = _tiles(seq)
    rwt = router_w.T.astype(BF16)
    rb = router_b.reshape(N_EXPERTS, 1)
    rw_pad = jnp.concatenate([router_w, jnp.zeros((D_MODEL, LANES - N_EXPERTS), F32)], axis=1).astype(BF16)
    t = batch * seq
    cos, sin = _rope_tables(positions, tiles["rope"])
    xt = x.reshape(t, D_MODEL)
    for i in range(DEPTH):
        j = i // N_MIXERS
        g0, b0 = ln_g[i, 0].reshape(1, -1), ln_b[i, 0].reshape(1, -1)
        g1, b1 = ln_g[i, 1].reshape(1, -1), ln_b[i, 1].reshape(1, -1)
        if i % N_MIXERS == 0:
            w = _prep_mla(mla_w_in[j], mla_g_q[j], mla_w_uq[j], mla_g_kv[j], mla_w_ukv[j])
            w_out = mla_w_o[j].astype(BF16)
        else:
            w = _prep_mlstm(mlstm_w_in[j], mlstm_conv_w[j], mlstm_conv_b[j], mlstm_w_q[j], mlstm_w_k[j],
                            mlstm_w_v[j], mlstm_w_ig[j], mlstm_b_ig[j], mlstm_w_fg[j], mlstm_b_fg[j])
            w_out = mlstm_w_out[j].astype(BF16)
        if i % N_MIXERS == 0:
            q, k, vt = _mla_proj(xt, cos, sin, w, tiles["attn"])
            a = _attention(q, k, vt, batch, seq, tiles["attn"])
        else:
            q, k, v, xc, sz, gates = _mlstm_pre(xt, w, batch, seq, tiles["pre"])
            a = _mlstm_chunks(q, k, v, gates, xc, sz, mlstm_g_out[j].reshape(1, -1),
                              mlstm_skip[j].reshape(1, -1), batch, seq, tiles["chunk"])
        x1, cls = _post(a, w_out, xt, g0, b0, rwt, rb, tiles["post"])
        xt = _moe(x1, cls.reshape(t), rw_pad, i, moe_w_gate, moe_w_up, moe_w_down, g1, b1, tiles["moe"])
    return xt.reshape(batch, seq, D_MODEL)
```

```python
import functools

import jax
import jax.numpy as jnp
from jax import lax
from jax.experimental import pallas as pl
from jax.experimental.pallas import tpu as pltpu
from jax.experimental.pallas import tpu_sc as plsc

D_MODEL = 1024
DEPTH = 4
N_MIXERS = 2
MLA_HEADS = 8
MLA_Q_RANK = 256
MLA_KV_RANK = 256
MLA_NOPE = 128
MLA_ROPE = 64
MLA_V = 128
ROPE_THETA = 10000.0
MLSTM_INNER = 2 * D_MODEL
MLSTM_HEADS = 4
MLSTM_HD = MLSTM_INNER // MLSTM_HEADS
MLSTM_CONV = 4
MLSTM_QKV_BLOCK = 4
N_EXPERTS = 16
N_GROUPS = 4
EXPERTS_PER_GROUP = N_EXPERTS // N_GROUPS
EXPERT_FF = 512
DN_ALPHA = (2.0 * DEPTH) ** 0.25
LN_EPS = 1e-5
LOG2_E = 1.4426950408889634

LANES = 128
MXU_DIM = 256
HEAD_SLOT = 2 * LANES
ATTN_HEADS_PER_STEP = 4
N_PAIRS = EXPERTS_PER_GROUP * (EXPERTS_PER_GROUP - 1) // 2
N_CLASSES = N_GROUPS * N_PAIRS
SC_WORKERS = 32
SC_MAX_ROWS = 48
MOE_SPLIT = 4
CHUNKS_PER_STEP = 2
POST_SUB = 512
VMEM_LIMIT = 56 * 1024 * 1024

F32 = jnp.float32
BF16 = jnp.bfloat16


def _tiles(seq):
    return dict(
        rope=min(seq, 2048),
        attn=min(seq, 512),
        post=min(seq, 1024),
        pre=min(seq, 256),
        chunk=min(seq, 256),
        moe=256,
    )


def _params(*sem):
    return pltpu.CompilerParams(dimension_semantics=sem, vmem_limit_bytes=VMEM_LIMIT)


def _dot(a, b):
    return jnp.dot(a, b, preferred_element_type=F32)


def _dot_nt(a, b):
    return lax.dot_general(a, b, (((1,), (1,)), ((), ())), preferred_element_type=F32)


def _dot_tn(a, b):
    return lax.dot_general(a, b, (((0,), (0,)), ((), ())), preferred_element_type=F32)


def _layer_norm(v, g, b):
    mu = jnp.mean(v, axis=-1, keepdims=True)
    c = v - mu
    var = jnp.mean(c * c, axis=-1, keepdims=True)
    return c * lax.rsqrt(var + LN_EPS) * g + b


def _rms_norm(v, g):
    return v * lax.rsqrt(jnp.mean(v * v, axis=-1, keepdims=True) + LN_EPS) * g


def _sigmoid(v):
    return 1.0 / (1.0 + jnp.exp(-v))


def _silu(v):
    return v * _sigmoid(v)


def _rope_kernel(pos_ref, invf_ref, cos_ref, sin_ref):
    ang = pos_ref[...].astype(F32) * invf_ref[...]
    cos_ref[...] = jnp.cos(ang)
    sin_ref[...] = jnp.sin(ang)


def _rope_tables(positions, tm):
    t = positions.size
    inv_freq = 1.0 / (ROPE_THETA ** (jnp.arange(0, MLA_ROPE, 2, dtype=F32) / MLA_ROPE))
    invf = jnp.tile(inv_freq, LANES // (MLA_ROPE // 2)).reshape(1, LANES)
    out = jax.ShapeDtypeStruct((t, LANES), F32)
    return pl.pallas_call(
        _rope_kernel,
        grid=(t // tm,),
        in_specs=[pl.BlockSpec((tm, 1), lambda i: (i, 0)),
                  pl.BlockSpec((1, LANES), lambda i: (0, 0))],
        out_specs=[pl.BlockSpec((tm, LANES), lambda i: (i, 0))] * 2,
        out_shape=[out, out],
        compiler_params=_params("parallel"),
        name="rope_tables",
    )(positions.reshape(t, 1), invf)


def _mla_proj_kernel(x_ref, cos_ref, sin_ref, w_in_ref, gq_ref, gkv_ref, wq1_ref, wq2_ref,
                     wk_ref, wvt_ref, q_ref, k_ref, vt_ref):
    xb = x_ref[...].astype(BF16)
    c = _dot(xb, w_in_ref[...])
    cq = _rms_norm(c[:, :MLA_Q_RANK], gq_ref[...]).astype(BF16)
    ckv = _rms_norm(c[:, MLA_Q_RANK:MLA_Q_RANK + MLA_KV_RANK], gkv_ref[...]).astype(BF16)
    cos = cos_ref[...]
    sin = sin_ref[...]
    base = MLA_Q_RANK + MLA_KV_RANK
    k_pe = (c[:, base:base + LANES] * cos + c[:, base + LANES:base + 2 * LANES] * sin).astype(BF16)
    scale = (MLA_NOPE + MLA_ROPE) ** -0.5 * LOG2_E
    cos_s = cos * scale
    sin_s = sin * scale
    q1 = _dot(cq, wq1_ref[...])
    q2 = _dot(cq, wq2_ref[...])
    kn = _dot(ckv, wk_ref[...])
    for h in range(MLA_HEADS):
        s0 = h * HEAD_SLOT
        q_ref[:, s0:s0 + LANES] = (q1[:, s0:s0 + LANES] * scale).astype(BF16)
        q_ref[:, s0 + LANES:s0 + HEAD_SLOT] = (
            q1[:, s0 + LANES:s0 + HEAD_SLOT] * cos_s
            + q2[:, h * LANES:(h + 1) * LANES] * sin_s).astype(BF16)
        k_ref[:, s0:s0 + LANES] = kn[:, h * LANES:(h + 1) * LANES].astype(BF16)
        k_ref[:, s0 + LANES:s0 + HEAD_SLOT] = k_pe
    vt_ref[...] = _dot_nt(wvt_ref[...], ckv).astype(BF16)


def _mla_proj(x, cos, sin, w, tm):
    t = x.shape[0]
    row = lambda n: pl.BlockSpec((tm, n), lambda i: (i, 0))
    full = lambda a: pl.BlockSpec(a.shape, lambda i: (0,) * a.ndim)
    consts = (w["w_in"], w["g_q"], w["g_kv"], w["w_q1"], w["w_q2"], w["w_k"], w["w_vt"])
    return pl.pallas_call(
        _mla_proj_kernel,
        grid=(t // tm,),
        in_specs=[row(D_MODEL), row(LANES), row(LANES)] + [full(a) for a in consts],
        out_specs=[row(MLA_HEADS * HEAD_SLOT), row(MLA_HEADS * HEAD_SLOT),
                   pl.BlockSpec((None, MLA_HEADS * MLA_V, tm), lambda i: (i, 0, 0))],
        out_shape=[jax.ShapeDtypeStruct((t, MLA_HEADS * HEAD_SLOT), BF16),
                   jax.ShapeDtypeStruct((t, MLA_HEADS * HEAD_SLOT), BF16),
                   jax.ShapeDtypeStruct((t // tm, MLA_HEADS * MLA_V, tm), BF16)],
        compiler_params=_params("parallel"),
        name="mla_proj",
    )(x, cos, sin, *consts)


def _attn_kernel(q_ref, k_ref, vt_ref, o_ref, s_ref, smax_ref, m_ref, l_ref, acc_ref, *, tile):
    qi = pl.program_id(2)
    m_ref[...] = jnp.full(m_ref.shape, -jnp.inf, F32)
    l_ref[...] = jnp.zeros(l_ref.shape, F32)
    acc_ref[...] = jnp.zeros(acc_ref.shape, F32)
    heads = range(ATTN_HEADS_PER_STEP)

    def scores(ki, buf):
        start = pl.multiple_of(ki * tile, tile)
        for g in heads:
            slot = slice(g * HEAD_SLOT, (g + 1) * HEAD_SLOT)
            s = _dot_nt(k_ref[pl.ds(start, tile), slot], q_ref[:, slot])
            s_ref[buf, g] = s
            smax_ref[buf, g] = jnp.max(s, axis=0, keepdims=True)

    def absorb(ki, buf, masked):
        for g in heads:
            s = s_ref[buf, g]
            if masked:
                keys = lax.broadcasted_iota(jnp.int32, s.shape, 0)
                queries = lax.broadcasted_iota(jnp.int32, s.shape, 1)
                s = jnp.where(keys <= queries, s, -jnp.inf)
                s_max = jnp.max(s, axis=0, keepdims=True)
            else:
                s_max = smax_ref[buf, g]
            m_old = m_ref[g]
            m_new = jnp.maximum(m_old, s_max)
            p = jnp.exp2(s - m_new)
            alpha = jnp.exp2(m_old - m_new)
            l_ref[g] = alpha * l_ref[g] + jnp.sum(p, axis=0, keepdims=True)
            acc_ref[g] = alpha * acc_ref[g] + _dot(vt_ref[ki, g * MLA_V:(g + 1) * MLA_V, :], p.astype(BF16))
            m_ref[g] = m_new

    scores(0, 0)

    def body(j, carry):
        scores(2 * j + 1, 1)
        absorb(2 * j, 0, False)
        scores(2 * j + 2, 0)
        absorb(2 * j + 1, 1, False)
        return carry

    lax.fori_loop(0, qi // 2, body, 0)

    @pl.when(qi % 2 == 1)
    def _():
        scores(qi, 1)
        absorb(qi - 1, 0, False)
        absorb(qi, 1, True)

    @pl.when(qi % 2 == 0)
    def _():
        absorb(qi, 0, True)

    for g in heads:
        o_ref[:, g * MLA_V:(g + 1) * MLA_V] = jnp.transpose(acc_ref[g] / l_ref[g]).astype(o_ref.dtype)


def _attention(q, k, vt, batch, seq, tile):
    t = q.shape[0]
    nq = seq // tile
    hps = ATTN_HEADS_PER_STEP
    return pl.pallas_call(
        functools.partial(_attn_kernel, tile=tile),
        grid=(batch, MLA_HEADS // hps, nq),
        in_specs=[pl.BlockSpec((tile, hps * HEAD_SLOT), lambda b, h, i: (b * nq + i, h)),
                  pl.BlockSpec((seq, hps * HEAD_SLOT), lambda b, h, i: (b, h)),
                  pl.BlockSpec((nq, hps * MLA_V, tile), lambda b, h, i: (b, h, 0))],
        out_specs=pl.BlockSpec((tile, hps * MLA_V), lambda b, h, i: (b * nq + i, h)),
        out_shape=jax.ShapeDtypeStruct((t, MLA_HEADS * MLA_V), BF16),
        scratch_shapes=[pltpu.VMEM((2, hps, tile, tile), F32), pltpu.VMEM((2, hps, 1, tile), F32),
                        pltpu.VMEM((hps, 1, tile), F32), pltpu.VMEM((hps, 1, tile), F32),
                        pltpu.VMEM((hps, MLA_V, tile), F32)],
        compiler_params=_params("parallel", "parallel", "arbitrary"),
        name="mla_attention",
    )(q, k, vt)


def _route_rows(sel):
    best = None
    for g in range(N_GROUPS):
        a = [sel[g * EXPERTS_PER_GROUP + j:g * EXPERTS_PER_GROUP + j + 1, :]
             for j in range(EXPERTS_PER_GROUP)]
        v1 = functools.reduce(jnp.maximum, a)
        i1 = jnp.full(v1.shape, EXPERTS_PER_GROUP - 1, jnp.int32)
        for j in reversed(range(EXPERTS_PER_GROUP - 1)):
            i1 = jnp.where(a[j] == v1, j, i1)
        r = [jnp.where(i1 == j, -jnp.inf, a[j]) for j in range(EXPERTS_PER_GROUP)]
        v2 = functools.reduce(jnp.maximum, r)
        i2 = jnp.full(v1.shape, EXPERTS_PER_GROUP - 1, jnp.int32)
        for j in reversed(range(EXPERTS_PER_GROUP - 1)):
            i2 = jnp.where(r[j] == v2, j, i2)
        total = v1 + v2
        if best is None:
            best = (total, jnp.zeros_like(i1), i1, i2)
        else:
            better = total > best[0]
            best = (jnp.where(better, total, best[0]), jnp.where(better, g, best[1]),
                    jnp.where(better, i1, best[2]), jnp.where(better, i2, best[3]))
    _, grp, i1, i2 = best
    lo = jnp.minimum(i1, i2)
    hi = jnp.maximum(i1, i2)
    pair = jnp.where(lo == 0, hi - 1, jnp.where(lo == 1, hi + 1, N_PAIRS - 1))
    return grp * N_PAIRS + pair


def _post_kernel(a_ref, w_ref, x_ref, g_ref, b_ref, rwt_ref, rb_ref, y_ref, cls_ref):
    subs = [slice(r, r + POST_SUB) for r in range(0, a_ref.shape[0], POST_SUB)]
    y = _dot(a_ref[subs[0], :], w_ref[...])
    for n, rows in enumerate(subs):
        y_next = _dot(a_ref[subs[n + 1], :], w_ref[...]) if n + 1 < len(subs) else None
        x1 = _layer_norm(DN_ALPHA * x_ref[rows, :] + y, g_ref[...], b_ref[...])
        y_ref[rows, :] = x1
        logits = _dot_nt(rwt_ref[...], x1.astype(BF16))
        cls_ref[:, rows] = _route_rows(_sigmoid(logits) + rb_ref[...])
        y = y_next


def _post(a, w, x, g, b, rwt, rb, tm):
    t, k = a.shape
    row = lambda n: pl.BlockSpec((tm, n), lambda i: (i, 0))
    full = lambda arr: pl.BlockSpec(arr.shape, lambda i: (0,) * arr.ndim)
    return pl.pallas_call(
        _post_kernel,
        grid=(t // tm,),
        in_specs=[row(k), full(w), row(D_MODEL), full(g), full(b), full(rwt), full(rb)],
        out_specs=[row(D_MODEL), pl.BlockSpec((1, tm), lambda i: (0, i))],
        out_shape=[jax.ShapeDtypeStruct((t, D_MODEL), F32),
                   jax.ShapeDtypeStruct((1, t), jnp.int32)],
        compiler_params=_params("parallel"),
        name="mixer_out_norm_route",
    )(a, w, x, g, b, rwt, rb)


def _expert_kernel(ea_ref, eb_ref, fresh_ref, nblk_ref, x_ref, rw_ref, wga_ref, wua_ref, wda_ref,
                   wgb_ref, wub_ref, wdb_ref, g_ref, b_ref, prev_ref, o_ref, wga, wua, wda, wgb, wub, wdb):
    del prev_ref
    i = pl.program_id(0)

    @pl.when(fresh_ref[i] == 1)
    def _():
        for src, dst in ((wga_ref, wga), (wua_ref, wua), (wda_ref, wda),
                         (wgb_ref, wgb), (wub_ref, wub), (wdb_ref, wdb)):
            dst[...] = src[...].astype(BF16)

    @pl.when(i < nblk_ref[0])
    def _():
        x = x_ref[...]
        xb = x.astype(BF16)
        sig = _sigmoid(_dot(xb, rw_ref[...]))
        lane = lax.broadcasted_iota(jnp.int32, sig.shape, 1)
        sa = jnp.sum(jnp.where(lane == ea_ref[i], sig, 0.0), axis=-1, keepdims=True)
        sb = jnp.sum(jnp.where(lane == eb_ref[i], sig, 0.0), axis=-1, keepdims=True)
        tot = sa + sb
        ha = _silu(_dot(xb, wga[...])) * _dot(xb, wua[...])
        hb = _silu(_dot(xb, wgb[...])) * _dot(xb, wub[...])
        y = (sa / tot) * _dot(ha.astype(BF16), wda[...]) + (sb / tot) * _dot(hb.astype(BF16), wdb[...])
        o_ref[...] = _layer_norm(DN_ALPHA * x + y, g_ref[...], b_ref[...])

    @pl.when(i >= nblk_ref[0])
    def _():
        o_ref[...] = jnp.zeros(o_ref.shape, o_ref.dtype)


def _experts(xs, ea, eb, fresh, nblk, rw, layer, wg, wu, wd, g, b, bm, blk0, total_rows, prev):
    p = xs.shape[0]
    sel = lambda which: (lambda i, ea, eb, fr, nb: (layer, (ea if which == 0 else eb)[i], 0, 0))
    up = lambda which: pl.BlockSpec((None, None, D_MODEL, EXPERT_FF), sel(which))
    down = lambda which: pl.BlockSpec((None, None, EXPERT_FF, D_MODEL), sel(which))
    full = lambda arr: pl.BlockSpec(arr.shape, lambda i, ea, eb, fr, nb: (0,) * arr.ndim)
    row = pl.BlockSpec((bm, D_MODEL), lambda i, ea, eb, fr, nb: (i, 0))
    out_row = pl.BlockSpec((bm, D_MODEL), lambda i, ea, eb, fr, nb: (i + blk0, 0))
    up_s = pltpu.VMEM((D_MODEL, EXPERT_FF), BF16)
    down_s = pltpu.VMEM((EXPERT_FF, D_MODEL), BF16)
    n_prefetch = 4
    if prev is None:
        prev = jnp.zeros((8, LANES), F32)
        aliases = {}
    else:
        aliases = {n_prefetch + 10: 0}
    return pl.pallas_call(
        _expert_kernel,
        grid_spec=pltpu.PrefetchScalarGridSpec(
            num_scalar_prefetch=n_prefetch,
            grid=(p // bm,),
            in_specs=[row, full(rw), up(0), up(0), down(0), up(1), up(1), down(1), full(g), full(b),
                      pl.BlockSpec(memory_space=pl.ANY)],
            out_specs=out_row,
            scratch_shapes=[up_s, up_s, down_s, up_s, up_s, down_s]),
        out_shape=jax.ShapeDtypeStruct((total_rows, D_MODEL), F32),
        input_output_aliases=aliases,
        compiler_params=_params("arbitrary"),
        name="moe_experts",
    )(ea, eb, fresh, nblk, xs, rw, wg, wu, wd, wg, wu, wd, g, b, prev)


def _row_gather(src, idx):
    m = idx.shape[0]
    d = src.shape[1]
    per = m // SC_WORKERS
    assert per * SC_WORKERS == m
    sc_rows = next(r for r in range(SC_MAX_ROWS, 0, -8) if per % r == 0 and (per // r) % 2 == 0)
    n = per // sc_rows
    mesh = plsc.VectorSubcoreMesh(core_axis_name="core", subcore_axis_name="subcore")

    def body(src_hbm, idx_hbm, out_hbm, idx_v, buf, gsem, psem):
        worker = lax.axis_index("core") * (SC_WORKERS // 2) + lax.axis_index("subcore")
        base = worker * per
        pltpu.sync_copy(idx_hbm.at[pl.ds(base, per)], idx_v)

        def gather(c, slot):
            return pltpu.make_async_copy(src_hbm.at[idx_v.at[pl.ds(c * sc_rows, sc_rows)]],
                                         buf.at[slot], gsem.at[slot])

        def put(c, slot):
            return pltpu.make_async_copy(buf.at[slot], out_hbm.at[pl.ds(base + c * sc_rows, sc_rows)],
                                         psem.at[slot])

        gather(0, 0).start()

        @pl.loop(0, n // 2)
        def _(j):
            for slot in range(2):
                c = 2 * j + slot
                gather(c, slot).wait()
                put(c, slot).start()

                @pl.when(c >= 1)
                def _():
                    put(c - 1, 1 - slot).wait()

                @pl.when(c + 1 < n)
                def _():
                    gather(c + 1, 1 - slot).start()

        put(n - 1, 1).wait()

    return pl.kernel(
        body,
        out_type=jax.ShapeDtypeStruct((m, d), src.dtype),
        mesh=mesh,
        scratch_types=[pltpu.VMEM((per,), jnp.int32), pltpu.VMEM((2, sc_rows, d), src.dtype),
                       pltpu.SemaphoreType.DMA((2,)), pltpu.SemaphoreType.DMA((2,))],
    )(src, idx)


def _moe(x1, cls, rw, layer, wg, wu, wd, g, b, bm):
    t = x1.shape[0]
    p_rows = t + N_CLASSES * bm
    nblk = p_rows // bm
    classes = jnp.arange(N_CLASSES, dtype=jnp.int32)
    counts = jnp.sum((cls[:, None] == classes[None, :]).astype(jnp.int32), axis=0)
    padded = ((counts + bm - 1) // bm) * bm
    off = jnp.cumsum(counts) - counts
    pend = jnp.cumsum(padded)
    poff = pend - padded
    tok = jnp.arange(t, dtype=jnp.int32)
    cls_sorted, order = lax.sort((cls, tok), num_keys=1)
    shift = jnp.sum(jnp.where(cls_sorted[:, None] == classes[None, :], (poff - off)[None, :], 0), axis=1)
    _, dest = lax.sort((order, shift + tok), num_keys=1)
    blk_start = jnp.arange(nblk, dtype=jnp.int32) * bm
    blk_cls = jnp.minimum(jnp.sum((blk_start[:, None] >= pend[None, :]).astype(jnp.int32), axis=1),
                          N_CLASSES - 1)
    n_used = (pend[-1] // bm).astype(jnp.int32).reshape(1)
    j = jnp.arange(bm, dtype=jnp.int32)[None, :] + (blk_start - poff[blk_cls])[:, None]
    rank = off[blk_cls][:, None] + jnp.minimum(j, counts[blk_cls][:, None] - 1)
    src = order[jnp.clip(rank, 0, t - 1).reshape(p_rows)]
    grp = blk_cls // N_PAIRS
    pair = blk_cls % N_PAIRS
    lo = jnp.where(pair < 3, 0, jnp.where(pair < 5, 1, 2))
    hi = jnp.where(pair < 3, pair + 1, jnp.where(pair < 5, pair - 1, 3))
    ea = (grp * EXPERTS_PER_GROUP + lo).astype(jnp.int32)
    eb = (grp * EXPERTS_PER_GROUP + hi).astype(jnp.int32)
    fresh = jnp.concatenate([jnp.ones((1,), jnp.int32), (blk_cls[1:] != blk_cls[:-1]).astype(jnp.int32)])
    nb = nblk // MOE_SPLIT
    xs = [_row_gather(x1, src[k * nb * bm:(k + 1) * nb * bm]) for k in range(MOE_SPLIT)]
    ys = None
    for k in range(MOE_SPLIT):
        part = slice(k * nb, (k + 1) * nb)
        fresh_k = jnp.concatenate([jnp.ones((1,), jnp.int32), fresh[part][1:]])
        ys = _experts(xs[k], ea[part], eb[part], fresh_k, n_used - k * nb, rw, layer, wg, wu, wd, g, b, bm,
                      k * nb, p_rows, ys)
    return _row_gather(ys, dest)


def _mlstm_pre_kernel(x_ref, w_in_ref, cw_ref, cb_ref, bdqk_ref, bdv_ref, wgate_ref, bgate_ref,
                      q_ref, k_ref, v_ref, xc_ref, sz_ref, gate_ref, ext_ref, *, tm):
    halo = 8
    xb = x_ref[...].astype(BF16)

    @pl.when(pl.program_id(1) == 0)
    def _():
        ext_ref[0:halo, :] = jnp.zeros((halo, MLSTM_INNER), F32)

    @pl.when(pl.program_id(1) != 0)
    def _():
        ext_ref[0:halo, :] = ext_ref[tm:tm + halo, :]

    xm = _dot(xb, w_in_ref[:, :MLSTM_INNER])
    ext_ref[halo:halo + tm, :] = xm
    conv = cb_ref[...] + cw_ref[MLSTM_CONV - 1:MLSTM_CONV, :] * xm
    for kk in range(MLSTM_CONV - 1):
        lag = MLSTM_CONV - 1 - kk
        conv = conv + cw_ref[kk:kk + 1, :] * ext_ref[halo - lag:halo - lag + tm, :]
    xc = _silu(conv)
    xc_b = xc.astype(BF16)
    xm_b = xm.astype(BF16)
    xc_ref[...] = xc_b
    sz_ref[...] = _silu(_dot(xb, w_in_ref[:, MLSTM_INNER:2 * MLSTM_INNER])).astype(BF16)

    gates = bgate_ref[...] + _dot(xb, w_in_ref[:, 2 * MLSTM_INNER:]) + _dot(xc_b, wgate_ref[...])
    k_scale = MLSTM_HD ** -0.5
    for c in range(MLSTM_INNER // MXU_DIM):
        lo, hi = c * MXU_DIM, (c + 1) * MXU_DIM
        qk = _dot(xc_b[:, lo:hi], bdqk_ref[c])
        q_ref[:, lo:hi] = qk[:, :MXU_DIM].astype(BF16)
        k_ref[:, lo:hi] = (qk[:, MXU_DIM:] * k_scale).astype(BF16)
        v_ref[:, lo:hi] = _dot(xm_b[:, lo:hi], bdv_ref[c]).astype(BF16)
    fg = gates[:, LANES:]
    gate_ref[:, :LANES] = gates[:, :LANES]
    gate_ref[:, LANES:] = jnp.minimum(fg, 0.0) - jnp.log(1.0 + jnp.exp(-jnp.abs(fg)))


def _mlstm_pre(x, w, batch, seq, tm):
    t = x.shape[0]
    ns = seq // tm
    row = lambda n: pl.BlockSpec((tm, n), lambda b, s: (b * ns + s, 0))
    full = lambda a: pl.BlockSpec(a.shape, lambda b, s: (0,) * a.ndim)
    consts = (w["w_in"], w["conv_w"], w["conv_b"], w["bd_qk"], w["bd_v"], w["w_gate"], w["b_gate"])
    act = jax.ShapeDtypeStruct((t, MLSTM_INNER), BF16)
    return pl.pallas_call(
        functools.partial(_mlstm_pre_kernel, tm=tm),
        grid=(batch, ns),
        in_specs=[row(D_MODEL)] + [full(a) for a in consts],
        out_specs=[row(MLSTM_INNER)] * 5 + [row(2 * LANES)],
        out_shape=[act] * 5 + [jax.ShapeDtypeStruct((t, 2 * LANES), F32)],
        scratch_shapes=[pltpu.VMEM((tm + 8, MLSTM_INNER), F32)],
        compiler_params=_params("parallel", "arbitrary"),
        name="mlstm_pre",
    )(x, *consts)


def _split3(v):
    a = v.astype(BF16)
    r = v - a.astype(F32)
    b = r.astype(BF16)
    c = (r - b.astype(F32)).astype(BF16)
    return a, b, c


def _mlstm_chunk_kernel(q_ref, k_ref, v_ref, gate_ref, xc_ref, sz_ref, gout_ref, skip_ref, o_ref,
                        c_ref, n_ref, m_ref, *, chunk):
    @pl.when(pl.program_id(1) == 0)
    def _():
        c_ref[...] = jnp.zeros(c_ref.shape, F32)
        n_ref[...] = jnp.zeros(n_ref.shape, F32)
        m_ref[...] = jnp.zeros(m_ref.shape, F32)

    for sub in range(CHUNKS_PER_STEP):
        rows = pl.ds(sub * chunk, chunk)
        _mlstm_chunk_body(q_ref.at[rows], k_ref.at[rows], v_ref.at[rows], gate_ref.at[rows], xc_ref.at[rows],
                          sz_ref.at[rows], gout_ref, skip_ref, o_ref.at[rows], c_ref, n_ref, m_ref, chunk)


def _mlstm_chunk_body(q_ref, k_ref, v_ref, gate_ref, xc_ref, sz_ref, gout_ref, skip_ref, o_ref,
                      c_ref, n_ref, m_ref, chunk):
    ig_all = gate_ref[:, :LANES]
    lf_all = gate_ref[:, LANES:]
    rows = lax.broadcasted_iota(jnp.int32, (chunk, chunk), 0)
    cols = lax.broadcasted_iota(jnp.int32, (chunk, chunk), 1)
    causal = cols <= rows
    tril = jnp.where(causal, 1.0, 0.0).astype(BF16)
    b_all = functools.reduce(lambda u, w: u + w, [_dot(tril, piece) for piece in _split3(lf_all)])
    row_all = jnp.transpose(ig_all - b_all)

    for h in range(MLSTM_HEADS):
        lo, hi = h * MLSTM_HD, (h + 1) * MLSTM_HD
        q = q_ref[:, lo:hi]
        k = k_ref[:, lo:hi]
        v = v_ref[:, lo:hi]
        b = b_all[:, h:h + 1]
        ig = ig_all[:, h:h + 1]
        m_prev = m_ref[h]
        d = jnp.where(causal, b + row_all[h:h + 1, :], -jnp.inf)
        inter = b + m_prev
        m_i = jnp.maximum(inter, jnp.max(d, axis=-1, keepdims=True))
        w_intra = jnp.exp(d - m_i)
        w_inter = jnp.exp(inter - m_i)
        s = _dot_nt(q, k) * w_intra
        c_old = c_ref[h]
        n_old = n_ref[h]
        num = _dot(s.astype(BF16), v) + w_inter * _dot(q, c_old.astype(BF16))
        qn = jnp.sum(q.astype(F32) * n_old, axis=-1, keepdims=True)
        den = jnp.sum(s, axis=-1, keepdims=True) + w_inter * qn
        hh = num / jnp.maximum(jnp.abs(den), jnp.exp(-m_i))

        b_last = b[chunk - 1:chunk, :]
        gg = b_last - b + ig
        m_new = jnp.maximum(b_last + m_prev, jnp.max(gg, axis=0, keepdims=True))
        kw = k.astype(F32) * jnp.exp(gg - m_new)
        decay = jnp.exp(b_last + m_prev - m_new)
        c_ref[h] = decay * c_old + _dot_tn(kw.astype(BF16), v)
        n_ref[h] = decay * n_old + jnp.sum(kw, axis=0, keepdims=True)
        m_ref[h] = m_new

        mu = jnp.mean(hh, axis=-1, keepdims=True)
        cen = hh - mu
        var = jnp.mean(cen * cen, axis=-1, keepdims=True)
        hn = cen * lax.rsqrt(var + LN_EPS) * gout_ref[:, lo:hi]
        o_ref[:, lo:hi] = ((hn + skip_ref[:, lo:hi] * xc_ref[:, lo:hi].astype(F32))
                           * sz_ref[:, lo:hi].astype(F32)).astype(o_ref.dtype)


def _mlstm_chunks(q, k, v, gates, xc, sz, g_out, skip, batch, seq, chunk):
    t = q.shape[0]
    nc = seq // (CHUNKS_PER_STEP * chunk)
    row = lambda n: pl.BlockSpec((CHUNKS_PER_STEP * chunk, n), lambda b, c: (b * nc + c, 0))
    full = lambda a: pl.BlockSpec(a.shape, lambda b, c: (0,) * a.ndim)
    return pl.pallas_call(
        functools.partial(_mlstm_chunk_kernel, chunk=chunk),
        grid=(batch, nc),
        in_specs=[row(MLSTM_INNER)] * 3 + [row(2 * LANES)] + [row(MLSTM_INNER)] * 2
                 + [full(g_out), full(skip)],
        out_specs=row(MLSTM_INNER),
        out_shape=jax.ShapeDtypeStruct((t, MLSTM_INNER), BF16),
        scratch_shapes=[pltpu.VMEM((MLSTM_HEADS, MLSTM_HD, MLSTM_HD), F32),
                        pltpu.VMEM((MLSTM_HEADS, 1, MLSTM_HD), F32),
                        pltpu.VMEM((MLSTM_HEADS, 1, 1), F32)],
        compiler_params=_params("parallel", "arbitrary"),
        name="mlstm_chunks",
    )(q, k, v, gates, xc, sz, g_out, skip)


def _rot_half(w):
    half = MLA_ROPE // 2
    return jnp.concatenate([-w[..., half:], w[..., :half]], axis=-1)


def _prep_mla(w_in, g_q, w_uq, g_kv, w_ukv):
    pad = jnp.zeros((D_MODEL, LANES - MLA_ROPE), F32)
    w_pe = w_in[:, MLA_Q_RANK + MLA_KV_RANK:]
    w_in_p = jnp.concatenate([w_in[:, :MLA_Q_RANK + MLA_KV_RANK], w_pe, pad, _rot_half(w_pe), pad], axis=1)
    uq = w_uq.reshape(MLA_Q_RANK, MLA_HEADS, MLA_NOPE + MLA_ROPE)
    zq = jnp.zeros((MLA_Q_RANK, MLA_HEADS, LANES - MLA_ROPE), F32)
    w_q1 = jnp.concatenate([uq, zq], axis=-1).reshape(MLA_Q_RANK, MLA_HEADS * HEAD_SLOT)
    w_q2 = jnp.concatenate([_rot_half(uq[..., MLA_NOPE:]), zq], axis=-1).reshape(MLA_Q_RANK, MLA_HEADS * LANES)
    ukv = w_ukv.reshape(MLA_KV_RANK, MLA_HEADS, MLA_NOPE + MLA_V)
    w_k = ukv[..., :MLA_NOPE].reshape(MLA_KV_RANK, -1)
    w_vt = ukv[..., MLA_NOPE:].reshape(MLA_KV_RANK, -1).T
    return dict(w_in=w_in_p.astype(BF16), g_q=g_q.reshape(1, -1), g_kv=g_kv.reshape(1, -1),
                w_q1=w_q1.astype(BF16), w_q2=w_q2.astype(BF16), w_k=w_k.astype(BF16),
                w_vt=w_vt.astype(BF16))


def _block_diag_tiles(w):
    per = MXU_DIM // MLSTM_QKV_BLOCK
    w = w.reshape(MLSTM_INNER // MXU_DIM, per, MLSTM_QKV_BLOCK, MLSTM_QKV_BLOCK)
    eye = jnp.eye(per, dtype=w.dtype)
    return jnp.einsum("cgij,gh->cgihj", w, eye).reshape(MLSTM_INNER // MXU_DIM, MXU_DIM, MXU_DIM)


def _prep_mlstm(w_in, conv_w, conv_b, w_q, w_k, w_v, w_ig, b_ig, w_fg, b_fg):
    bd_qk = jnp.concatenate([_block_diag_tiles(w_q), _block_diag_tiles(w_k)], axis=-1)
    zw = jnp.zeros((3 * MLSTM_INNER, LANES - MLSTM_HEADS), F32)
    zb = jnp.zeros((LANES - MLSTM_HEADS,), F32)
    w_gate = jnp.concatenate([w_ig, zw, w_fg, zw], axis=1)
    b_gate = jnp.concatenate([b_ig, zb, b_fg, zb]).reshape(1, 2 * LANES)
    blocks = lambda a: a.reshape(MLSTM_INNER // MLSTM_QKV_BLOCK, MLSTM_QKV_BLOCK, 2 * LANES)
    fold = lambda bd, a: jnp.einsum("gij,gjn->gin", bd, blocks(a), precision=lax.Precision.HIGHEST)
    gate_q, gate_k, gate_v = (w_gate[n * MLSTM_INNER:(n + 1) * MLSTM_INNER] for n in range(3))
    w_gate_xc = (fold(w_q, gate_q) + fold(w_k, gate_k)).reshape(MLSTM_INNER, 2 * LANES)
    w_gate_x = jnp.dot(w_in[:, :MLSTM_INNER], fold(w_v, gate_v).reshape(MLSTM_INNER, 2 * LANES),
                       precision=lax.Precision.HIGHEST)
    return dict(w_in=jnp.concatenate([w_in, w_gate_x], axis=1).astype(BF16), conv_w=conv_w,
                conv_b=conv_b.reshape(1, -1), bd_qk=bd_qk.astype(BF16),
                bd_v=_block_diag_tiles(w_v).astype(BF16), w_gate=w_gate_xc.astype(BF16), b_gate=b_gate)


def kernel(x, positions, router_w, router_b, mla_w_in, mla_g_q, mla_w_uq, mla_g_kv, mla_w_ukv, mla_w_o, mlstm_w_in, mlstm_conv_w, mlstm_conv_b, mlstm_w_q, mlstm_w_k, mlstm_w_v, mlstm_w_ig, mlstm_b_ig, mlstm_w_fg, mlstm_b_fg, mlstm_g_out, mlstm_skip, mlstm_w_out, moe_w_gate, moe_w_up, moe_w_down, ln_g, ln_b):
    batch, seq, _ = x.shape
    tiles = _tiles(seq)
    rwt = router_w.T.astype(BF16)
    rb = router_b.reshape(N_EXPERTS, 1)
    rw_pad = jnp.concatenate([router_w, jnp.zeros((D_MODEL, LANES - N_EXPERTS), F32)], axis=1).astype(BF16)
    t = batch * seq
    cos, sin = _rope_tables(positions, tiles["rope"])
    xt = x.reshape(t, D_MODEL)
    for i in range(DEPTH):
        j = i // N_MIXERS
        g0, b0 = ln_g[i, 0].reshape(1, -1), ln_b[i, 0].reshape(1, -1)
        g1, b1 = ln_g[i, 1].reshape(1, -1), ln_b[i, 1].reshape(1, -1)
        if i % N_MIXERS == 0:
            w = _prep_mla(mla_w_in[j], mla_g_q[j], mla_w_uq[j], mla_g_kv[j], mla_w_ukv[j])
            w_out = mla_w_o[j].astype(BF16)
        else:
            w = _prep_mlstm(mlstm_w_in[j], mlstm_conv_w[j], mlstm_conv_b[j], mlstm_w_q[j], mlstm_w_k[j],
                            mlstm_w_v[j], mlstm_w_ig[j], mlstm_b_ig[j], mlstm_w_fg[j], mlstm_b_fg[j])
            w_out = mlstm_w_out[j].astype(BF16)
        if i % N_MIXERS == 0:
            q, k, vt = _mla_proj(xt, cos, sin, w, tiles["attn"])
            a = _attention(q, k, vt, batch, seq, tiles["attn"])
        else:
            q, k, v, xc, sz, gates = _mlstm_pre(xt, w, batch, seq, tiles["pre"])
            a = _mlstm_chunks(q, k, v, gates, xc, sz, mlstm_g_out[j].reshape(1, -1),
                              mlstm_skip[j].reshape(1, -1), batch, seq, tiles["chunk"])
        x1, cls = _post(a, w_out, xt, g0, b0, rwt, rb, tiles["post"])
        xt = _moe(x1, cls.reshape(t), rw_pad, i, moe_w_gate, moe_w_up, moe_w_down, g1, b1, tiles["moe"])
    return xt.reshape(batch, seq, D_MODEL)
```
